```python
import math
import jax
import jax.numpy as jnp
from jax import lax
import numpy as np

D_MODEL = 1024
BATCH = 2
SEQ = 8192
DEPTH = 1
DEC_BATCH = 128
DEC_SEQ = 4
PAST_LEN = 8192
PAGE_SIZE = 128

N_HEADS = 16
N_KV_HEADS = 4
HEAD_DIM = 64
Q_PER_KV = N_HEADS // N_KV_HEADS
MOBA_BLOCK = 256
MOBA_TOPK = 3
MOBA_QCHUNK = 64
SSM_HEADS = 16
SSM_HEAD_DIM = 64
D_INNER = SSM_HEADS * SSM_HEAD_DIM
SSM_GROUPS = 4
D_STATE = 128
CONV_WIDTH = 4
CONV_DIM = D_INNER + 2 * SSM_GROUPS * D_STATE
SSD_CHUNK = 128
MEM_TOKENS = 256
MEM_HEADS = 4
MEM_HEAD_DIM = 256
N_BRANCHES = 3
D_FF = ((8 * D_MODEL // 3 + 255) // 256) * 256
EPS = 1e-6
IN_SIZES = (N_HEADS * HEAD_DIM, N_KV_HEADS * HEAD_DIM, N_KV_HEADS * HEAD_DIM, D_INNER, CONV_DIM, SSM_HEADS, MEM_HEADS * MEM_HEAD_DIM, N_BRANCHES * D_MODEL)
D_IN = sum(IN_SIZES)

kernel_name = 'hybrid_moba_ssd_memory_step'


def rmsnorm(x, g):
    xf = x.astype(jnp.float32)
    y = xf * lax.rsqrt(jnp.mean(xf * xf, axis=-1, keepdims=True) + EPS)
    return (y * g.astype(jnp.float32)).astype(x.dtype)


def group_rmsnorm(x, g, groups):
    shape = x.shape
    xg = x.reshape(shape[:-1] + (groups, shape[-1] // groups))
    return rmsnorm(xg, g.reshape(groups, -1)).reshape(shape)


def alibi_slopes():
    return jnp.exp2(-8.0 * jnp.arange(1, N_HEADS + 1, dtype=jnp.float32) / N_HEADS)


def project_inputs(x, norm_mix, w_in, q_norm, k_norm, mq_norm):
    b, t, _ = x.shape
    p = rmsnorm(x, norm_mix) @ w_in
    offs = np.cumsum(IN_SIZES)[:-1].tolist()
    q, k, v, z, xbc, dt_raw, mq, gates = jnp.split(p, offs, axis=-1)
    q = rmsnorm(q.reshape(b, t, N_HEADS, HEAD_DIM), q_norm)
    k = rmsnorm(k.reshape(b, t, N_KV_HEADS, HEAD_DIM), k_norm)
    v = v.reshape(b, t, N_KV_HEADS, HEAD_DIM)
    mq = rmsnorm(mq.reshape(b, t, MEM_HEADS, MEM_HEAD_DIM), mq_norm)
    return q, k, v, z, xbc, dt_raw, mq, gates


def moba_core(q, pos_q, kb, vb, kmean, slopes):
    kvh = jnp.arange(N_HEADS, dtype=jnp.int32) // Q_PER_KV
    nblk = kb.shape[1]
    qf = q.astype(jnp.float32)
    gate = jnp.einsum('hqd,hnd->hqn', qf, kmean[kvh].astype(jnp.float32))
    own = pos_q // MOBA_BLOCK
    is_past = jnp.arange(nblk, dtype=jnp.int32)[None, :] < own[:, None]
    gate = jnp.where(is_past[None], gate, -jnp.inf)
    top_s, top_i = lax.top_k(gate, min(MOBA_TOPK, nblk))
    h, tq = q.shape[0], q.shape[1]
    idx = jnp.concatenate([top_i.astype(jnp.int32), jnp.broadcast_to(own[None, :, None], (h, tq, 1))], axis=-1)
    valid = jnp.concatenate([jnp.isfinite(top_s), jnp.ones((h, tq, 1), bool)], axis=-1)
    k_sel = kb[kvh[:, None, None], idx].astype(jnp.float32)
    v_sel = vb[kvh[:, None, None], idx].astype(jnp.float32)
    key_pos = idx[..., None] * MOBA_BLOCK + jnp.arange(MOBA_BLOCK, dtype=jnp.int32)
    dist = (pos_q[None, :, None, None] - key_pos).astype(jnp.float32)
    s = jnp.einsum('hqd,hqjsd->hqjs', qf, k_sel) * (HEAD_DIM ** -0.5) - slopes[:, None, None, None] * dist
    mask = valid[..., None] & (key_pos <= pos_q[None, :, None, None])
    s = jnp.where(mask, s, -jnp.inf).reshape(h, tq, -1)
    p = jax.nn.softmax(s, axis=-1).reshape(k_sel.shape[:-1])
    o = jnp.einsum('hqjs,hqjsd->qhd', p, v_sel)
    return o.astype(q.dtype)


def moba_prompt(q, k, v, slopes):
    b, s = q.shape[0], q.shape[1]
    nblk = -(-s // MOBA_BLOCK)
    pad = nblk * MOBA_BLOCK - s

    def blocks(t):
        t = jnp.pad(t, ((0, 0), (0, pad), (0, 0), (0, 0)))
        return t.reshape(b, nblk, MOBA_BLOCK, N_KV_HEADS, HEAD_DIM).transpose(0, 3, 1, 2, 4)

    kb, vb = blocks(k), blocks(v)
    kmean = jnp.mean(kb.astype(jnp.float32), axis=3)
    nc = s // MOBA_QCHUNK
    qs = q.reshape(b, nc, MOBA_QCHUNK, N_HEADS, HEAD_DIM).transpose(0, 1, 3, 2, 4).reshape(b * nc, N_HEADS, MOBA_QCHUNK, HEAD_DIM)
    flat = jnp.arange(b * nc, dtype=jnp.int32)
    bidx, cidx = flat // nc, flat % nc

    def body(args):
        qc, bi, ci = args
        pos = ci * MOBA_QCHUNK + jnp.arange(MOBA_QCHUNK, dtype=jnp.int32)
        return moba_core(qc, pos, kb[bi], vb[bi], kmean[bi], slopes)

    o = lax.map(body, (qs, bidx, cidx))
    return o.reshape(b, s, N_HEADS * HEAD_DIM)


def moba_sample(q, k_new, v_new, k_pool, v_pool, page_table, slopes):
    db, t = q.shape[0], q.shape[1]
    past = page_table.shape[1] * k_pool.shape[1]
    nblk = -(-(past + t) // MOBA_BLOCK)
    pad = nblk * MOBA_BLOCK - past - t

    def blocks(pool, pt, new):
        rows = pool[pt].reshape(past, N_KV_HEADS, HEAD_DIM)
        rows = jnp.concatenate([rows, new.astype(rows.dtype), jnp.zeros((pad, N_KV_HEADS, HEAD_DIM), rows.dtype)], axis=0)
        return rows.reshape(nblk, MOBA_BLOCK, N_KV_HEADS, HEAD_DIM).transpose(2, 0, 1, 3)

    def body(args):
        qi, kn, vn, pt = args
        kb = blocks(k_pool, pt, kn)
        vb = blocks(v_pool, pt, vn)
        kmean = jnp.mean(kb.astype(jnp.float32), axis=2)
        pos = past + jnp.arange(t, dtype=jnp.int32)
        return moba_core(qi.transpose(1, 0, 2), pos, kb, vb, kmean, slopes)

    o = lax.map(body, (q, k_new, v_new, page_table))
    return o.reshape(db, t, N_HEADS * HEAD_DIM)


def ssd_scan(xs, dt, a, bm, cm, h0, chunk):
    b, t, nh, p = xs.shape
    g, n = bm.shape[2], bm.shape[3]
    r = nh // g
    c = t // chunk
    f32 = jnp.float32
    xdt = (xs.astype(f32) * dt[..., None]).reshape(b, c, chunk, g, r, p)
    cs = jnp.cumsum((dt * a).reshape(b, c, chunk, g, r), axis=2)
    bc = bm.astype(f32).reshape(b, c, chunk, g, n)
    cc = cm.astype(f32).reshape(b, c, chunk, g, n)
    causal = jnp.tril(jnp.ones((chunk, chunk), bool))
    seg = cs[:, :, :, None] - cs[:, :, None, :]
    decay = jnp.exp(jnp.where(causal[:, :, None, None], seg, -jnp.inf))
    cb = jnp.einsum('bclgn,bcsgn->bclsg', cc, bc)
    y_diag = jnp.einsum('bclsg,bclsgr,bcsgrp->bclgrp', cb, decay, xdt)
    decay_to_end = jnp.exp(cs[:, :, -1:] - cs)
    chunk_states = jnp.einsum('bclgn,bclgr,bclgrp->bcgrpn', bc, decay_to_end, xdt)
    chunk_decay = jnp.exp(cs[:, :, -1])

    def step(h, inp):
        st, dec = inp
        return dec[..., None, None] * h + st, h

    h_last, h_prev = lax.scan(step, h0.astype(f32).reshape(b, g, r, p, n), (jnp.moveaxis(chunk_states, 1, 0), jnp.moveaxis(chunk_decay, 1, 0)))
    h_prev = jnp.moveaxis(h_prev, 0, 1)
    y_off = jnp.einsum('bclgn,bcgrpn,bclgr->bclgrp', cc, h_prev, jnp.exp(cs))
    y = (y_diag + y_off).reshape(b, t, nh, p)
    return y.astype(xs.dtype), h_last.reshape(b, nh, p, n).astype(h0.dtype)


def ssm_branch(xbc_ext, z, dt_raw, h0, conv_w, conv_b, dt_bias, a_log, d_skip, ssm_norm, chunk):
    b, t, _ = z.shape
    conv = conv_b + conv_w[0] * xbc_ext[:, 0:t]
    for i in range(1, CONV_WIDTH):
        conv = conv + conv_w[i] * xbc_ext[:, i:i + t]
    xbc = jax.nn.silu(conv)
    xs, bm, cm = jnp.split(xbc, [D_INNER, D_INNER + SSM_GROUPS * D_STATE], axis=-1)
    xs = xs.reshape(b, t, SSM_HEADS, SSM_HEAD_DIM)
    bm = bm.reshape(b, t, SSM_GROUPS, D_STATE)
    cm = cm.reshape(b, t, SSM_GROUPS, D_STATE)
    dt = jax.nn.softplus(dt_raw.astype(jnp.float32) + dt_bias.astype(jnp.float32))
    a = -jnp.exp(a_log.astype(jnp.float32))
    y, h_last = ssd_scan(xs, dt, a, bm, cm, h0, chunk)
    y = (y + d_skip[:, None].astype(y.dtype) * xs).reshape(b, t, D_INNER)
    y = group_rmsnorm(y * jax.nn.silu(z), ssm_norm, SSM_GROUPS)
    return y, h_last, xbc_ext[:, -(CONV_WIDTH - 1):]


def memory_kv(mem, mem_norm, w_mem_kv, mk_norm):
    b, m, _ = mem.shape
    kv = rmsnorm(mem, mem_norm) @ w_mem_kv
    k, v = jnp.split(kv, 2, axis=-1)
    k = rmsnorm(k.reshape(b, m, MEM_HEADS, MEM_HEAD_DIM), mk_norm)
    return k, v.reshape(b, m, MEM_HEADS, MEM_HEAD_DIM)


def memory_attend(q, mk, mv):
    b, t = q.shape[0], q.shape[1]
    s = jnp.einsum('bthd,bmhd->bhtm', q.astype(jnp.float32), mk.astype(jnp.float32)) * (MEM_HEAD_DIM ** -0.5)
    p = jax.nn.softmax(s, axis=-1)
    o = jnp.einsum('bhtm,bmhd->bthd', p, mv.astype(jnp.float32))
    return o.reshape(b, t, MEM_HEADS * MEM_HEAD_DIM).astype(q.dtype)


def merge_and_ffn(x, attn_o, ssm_o, mem_o, gates, w_attn_br, w_ssm_br, w_mem_br, w_out, norm_ffn, w_gate, w_up, w_down):
    g_attn, g_ssm, g_mem = jnp.split(jax.nn.sigmoid(gates), N_BRANCHES, axis=-1)
    mixed = g_attn * (attn_o @ w_attn_br) + g_ssm * (ssm_o @ w_ssm_br) + g_mem * (mem_o @ w_mem_br)
    x = x + mixed @ w_out
    h = rmsnorm(x, norm_ffn)
    return x + (jax.nn.silu(h @ w_gate) * (h @ w_up)) @ w_down


def setup_inputs(seed: int = 0) -> dict:
    key = jax.random.key(seed)
    ks = iter(jax.random.split(key, 64))
    f32 = jnp.float32
    L = DEPTH

    def nrm(shape, scale=1.0):
        return scale * jax.random.normal(next(ks), shape, f32)

    def gain(shape):
        return 1.0 + 0.05 * nrm(shape)

    n_pages = PAST_LEN // PAGE_SIZE
    n_pool = (DEC_BATCH * n_pages * 5) // 4
    perm = jax.random.permutation(next(ks), n_pool)[:DEC_BATCH * n_pages]
    page_table = perm.reshape(DEC_BATCH, n_pages).astype(jnp.int32)
    u = jax.random.uniform(next(ks), (L, SSM_HEADS), f32)
    dt0 = jnp.exp(u * (math.log(0.1) - math.log(1e-3)) + math.log(1e-3))
    dt_bias = dt0 + jnp.log(-jnp.expm1(-dt0))
    a_log = jnp.log(jax.random.uniform(next(ks), (L, SSM_HEADS), f32, 1.0, 16.0))
    return {
        'x_prompt': nrm((BATCH, SEQ, D_MODEL)),
        'x_sample': nrm((DEC_BATCH, DEC_SEQ, D_MODEL)),
        'mem_prompt': nrm((BATCH, MEM_TOKENS, D_MODEL)),
        'cache_k': nrm((L, n_pool, PAGE_SIZE, N_KV_HEADS, HEAD_DIM)),
        'cache_v': nrm((L, n_pool, PAGE_SIZE, N_KV_HEADS, HEAD_DIM)),
        'page_table': page_table,
        'state_conv': nrm((L, DEC_BATCH, CONV_WIDTH - 1, CONV_DIM)),
        'state_ssm': nrm((L, DEC_BATCH, SSM_HEADS, SSM_HEAD_DIM, D_STATE), 0.3),
        'cache_mem_k': nrm((L, DEC_BATCH, MEM_TOKENS, MEM_HEADS, MEM_HEAD_DIM)),
        'cache_mem_v': nrm((L, DEC_BATCH, MEM_TOKENS, MEM_HEADS, MEM_HEAD_DIM)),
        'norm_mix': gain((L, D_MODEL)),
        'w_in': nrm((L, D_MODEL, D_IN), D_MODEL ** -0.5),
        'q_norm': gain((L, HEAD_DIM)),
        'k_norm': gain((L, HEAD_DIM)),
        'conv_w': nrm((L, CONV_WIDTH, CONV_DIM), CONV_WIDTH ** -0.5),
        'conv_b': nrm((L, CONV_DIM), 0.02),
        'dt_bias': dt_bias,
        'a_log': a_log,
        'd_skip': gain((L, SSM_HEADS)),
        'ssm_norm': gain((L, D_INNER)),
        'mem_norm': gain((L, D_MODEL)),
        'w_mem_kv': nrm((L, D_MODEL, 2 * MEM_HEADS * MEM_HEAD_DIM), D_MODEL ** -0.5),
        'mq_norm': gain((L, MEM_HEAD_DIM)),
        'mk_norm': gain((L, MEM_HEAD_DIM)),
        'w_attn_br': nrm((L, N_HEADS * HEAD_DIM, D_MODEL), (N_HEADS * HEAD_DIM) ** -0.5),
        'w_ssm_br': nrm((L, D_INNER, D_MODEL), D_INNER ** -0.5),
        'w_mem_br': nrm((L, MEM_HEADS * MEM_HEAD_DIM, D_MODEL), (MEM_HEADS * MEM_HEAD_DIM) ** -0.5),
        'w_out': nrm((L, D_MODEL, D_MODEL), D_MODEL ** -0.5),
        'norm_ffn': gain((L, D_MODEL)),
        'w_gate': nrm((L, D_MODEL, D_FF), D_MODEL ** -0.5),
        'w_up': nrm((L, D_MODEL, D_FF), D_MODEL ** -0.5),
        'w_down': nrm((L, D_FF, D_MODEL), D_FF ** -0.5),
    }


def reference(x_prompt, x_sample, mem_prompt, cache_k, cache_v, page_table, state_conv, state_ssm, cache_mem_k, cache_mem_v,
              norm_mix, w_in, q_norm, k_norm, conv_w, conv_b, dt_bias, a_log, d_skip, ssm_norm, mem_norm, w_mem_kv,
              mq_norm, mk_norm, w_attn_br, w_ssm_br, w_mem_br, w_out, norm_ffn, w_gate, w_up, w_down):
    slopes = alibi_slopes()
    xp, xd = x_prompt, x_sample
    kp_l, vp_l, cp_l, hp_l, mkp_l, mvp_l = [], [], [], [], [], []
    kd_l, vd_l, cd_l, hd_l = [], [], [], []
    for l in range(DEPTH):
        q, k, v, z, xbc, dt_raw, mq, gates = project_inputs(xp, norm_mix[l], w_in[l], q_norm[l], k_norm[l], mq_norm[l])
        attn_o = moba_prompt(q, k, v, slopes)
        xbc_ext = jnp.pad(xbc, ((0, 0), (CONV_WIDTH - 1, 0), (0, 0)))
        h0 = jnp.zeros((xp.shape[0], SSM_HEADS, SSM_HEAD_DIM, D_STATE), xp.dtype)
        ssm_o, h_new, conv_new = ssm_branch(xbc_ext, z, dt_raw, h0, conv_w[l], conv_b[l], dt_bias[l], a_log[l], d_skip[l], ssm_norm[l], SSD_CHUNK)
        mk, mv = memory_kv(mem_prompt, mem_norm[l], w_mem_kv[l], mk_norm[l])
        mem_o = memory_attend(mq, mk, mv)
        xp = merge_and_ffn(xp, attn_o, ssm_o, mem_o, gates, w_attn_br[l], w_ssm_br[l], w_mem_br[l], w_out[l], norm_ffn[l], w_gate[l], w_up[l], w_down[l])
        kp_l.append(k)
        vp_l.append(v)
        cp_l.append(conv_new)
        hp_l.append(h_new)
        mkp_l.append(mk)
        mvp_l.append(mv)
        q, k, v, z, xbc, dt_raw, mq, gates = project_inputs(xd, norm_mix[l], w_in[l], q_norm[l], k_norm[l], mq_norm[l])
        attn_o = moba_sample(q, k, v, cache_k[l], cache_v[l], page_table, slopes)
        xbc_ext = jnp.concatenate([state_conv[l].astype(xbc.dtype), xbc], axis=1)
        ssm_o, h_new, conv_new = ssm_branch(xbc_ext, z, dt_raw, state_ssm[l], conv_w[l], conv_b[l], dt_bias[l], a_log[l], d_skip[l], ssm_norm[l], xd.shape[1])
        mem_o = memory_attend(mq, cache_mem_k[l], cache_mem_v[l])
        xd = merge_and_ffn(xd, attn_o, ssm_o, mem_o, gates, w_attn_br[l], w_ssm_br[l], w_mem_br[l], w_out[l], norm_ffn[l], w_gate[l], w_up[l], w_down[l])
        kd_l.append(k)
        vd_l.append(v)
        cd_l.append(conv_new)
        hd_l.append(h_new)
    k_prompt, v_prompt = jnp.stack(kp_l), jnp.stack(vp_l)
    conv_prompt, ssm_prompt = jnp.stack(cp_l), jnp.stack(hp_l)
    mem_k_prompt, mem_v_prompt = jnp.stack(mkp_l), jnp.stack(mvp_l)
    k_sample, v_sample = jnp.stack(kd_l), jnp.stack(vd_l)
    conv_sample, ssm_sample = jnp.stack(cd_l), jnp.stack(hd_l)
    return (xp, xd, k_prompt, v_prompt, conv_prompt, ssm_prompt, mem_k_prompt, mem_v_prompt, k_sample, v_sample, conv_sample, ssm_sample)
```

```python
import functools
import math

import jax
import jax.numpy as jnp
import numpy as np
from jax import lax
from jax.experimental import pallas as pl
from jax.experimental.pallas import tpu as pltpu

f32 = jnp.float32
bf16 = jnp.bfloat16

D_MODEL = 1024
N_HEADS = 16
N_KV_HEADS = 4
HEAD_DIM = 64
Q_PER_KV = N_HEADS // N_KV_HEADS
MOBA_BLOCK = 256
MOBA_TOPK = 3
SSM_HEADS = 16
SSM_HEAD_DIM = 64
D_INNER = SSM_HEADS * SSM_HEAD_DIM
SSM_GROUPS = 4
D_STATE = 128
CONV_WIDTH = 4
CONV_DIM = D_INNER + 2 * SSM_GROUPS * D_STATE
MEM_TOKENS = 256
MEM_HEADS = 4
MEM_HEAD_DIM = 256
D_FF = 2816
EPS = 1e-6
PAGE_SIZE = 128

LANES = 128
SUBLANES = 8
LOG2E = 1.4426950408889634
NEG = -1e30
VMEM_LIMIT = 56 * 1024 * 1024

OFF_Q, OFF_Z, OFF_XBC, OFF_MQ, OFF_GATES, OFF_K, OFF_V, OFF_DT = 0, 1024, 2048, 4096, 5120, 8192, 8448, 8704
N_PROJ = 8832
SAMPLE_ROWS = 8


def _cparams(sem):
    return pltpu.CompilerParams(dimension_semantics=sem, vmem_limit_bytes=VMEM_LIMIT)


def _split3(x):
    h = x.astype(bf16)
    r = x - h.astype(f32)
    m = r.astype(bf16)
    l = (r - m.astype(f32)).astype(bf16)
    return h, m, l


def _dot(a, b):
    return jnp.dot(a, b, preferred_element_type=f32)


def _dot_nt(a, b):
    return lax.dot_general(a, b, (((1,), (1,)), ((), ())), preferred_element_type=f32)


def _dot_tn(a, b):
    return lax.dot_general(a, b, (((0,), (0,)), ((), ())), preferred_element_type=f32)


def _dot_exact_rhs(x, m_bf16):
    h, m, l = _split3(x)
    return _dot(h, m_bf16) + _dot(m, m_bf16) + _dot(l, m_bf16)


def _dot_exact_lhs(m_bf16, x):
    h, m, l = _split3(x)
    return _dot(m_bf16, h) + _dot(m_bf16, m) + _dot(m_bf16, l)


def _silu(x):
    return x * (1.0 / (1.0 + jnp.exp(-x)))


def _sigmoid(x):
    return 1.0 / (1.0 + jnp.exp(-x))


def _softplus(x):
    return jnp.maximum(x, 0.0) + jnp.log(1.0 + jnp.exp(-jnp.abs(x)))


def _inproj_kernel(x_ref, g_ref, w_ref, o_ref):
    x = x_ref[...]
    ms = jnp.mean(x * x, axis=-1, keepdims=True)
    xn = (x * lax.rsqrt(ms + EPS) * g_ref[...]).astype(bf16)
    o_ref[...] = _dot(xn, w_ref[...])


def inproj(x, gain, w, tm, tn):
    t, d = x.shape
    n = w.shape[1]
    return pl.pallas_call(
        _inproj_kernel,
        grid=(n // tn, t // tm),
        in_specs=[
            pl.BlockSpec((tm, d), lambda j, i: (i, 0)),
            pl.BlockSpec((1, d), lambda j, i: (0, 0)),
            pl.BlockSpec((d, tn), lambda j, i: (0, j)),
        ],
        out_specs=pl.BlockSpec((tm, tn), lambda j, i: (i, j)),
        out_shape=jax.ShapeDtypeStruct((t, n), f32),
        compiler_params=_cparams(("arbitrary", "arbitrary")),
        name="inproj",
    )(x, gain.reshape(1, d), w)


def _seg_ones(width, seg):
    r = lax.broadcasted_iota(jnp.int32, (width, width), 0) // seg
    c = lax.broadcasted_iota(jnp.int32, (width, width), 1) // seg
    return jnp.where(r == c, 1.0, 0.0).astype(bf16)


def _seg_rmsnorm(x, gain_row, seg):
    ss = _dot_exact_rhs(x * x, _seg_ones(x.shape[1], seg))
    return x * lax.rsqrt(ss * (1.0 / seg) + EPS) * gain_row


def _headnorm_kernel(x_ref, g_ref, o_ref, *, seg):
    o_ref[...] = _seg_rmsnorm(x_ref[...], g_ref[...], seg)


def headnorm(x, col_block, width, gain_row, seg, tm):
    t = x.shape[0]
    return pl.pallas_call(
        functools.partial(_headnorm_kernel, seg=seg),
        grid=(t // tm,),
        in_specs=[
            pl.BlockSpec((tm, width), lambda i: (i, col_block)),
            pl.BlockSpec((1, width), lambda i: (0, 0)),
        ],
        out_specs=pl.BlockSpec((tm, width), lambda i: (i, 0)),
        out_shape=jax.ShapeDtypeStruct((t, width), f32),
        compiler_params=_cparams(("arbitrary",)),
        name="headnorm",
    )(x, gain_row)


def _kprep_kernel(k_ref, v_ref, g_ref, kn_ref, kb_ref, vt_ref, km_ref):
    kn = _seg_rmsnorm(k_ref[...], g_ref[...], HEAD_DIM)
    kn_ref[...] = kn
    kb_ref[...] = kn.astype(bf16)
    nb = kn.shape[0] // MOBA_BLOCK
    km_ref[0] = jnp.sum(kn.reshape(nb, MOBA_BLOCK, kn.shape[1]), axis=1) * (1.0 / MOBA_BLOCK)
    vt_ref[0] = v_ref[...].astype(bf16).T


def kprep(proj, k_gain_row, batch, seq, tm):
    kvw = N_KV_HEADS * HEAD_DIM
    nt = seq // tm
    nb = tm // MOBA_BLOCK
    return pl.pallas_call(
        _kprep_kernel,
        grid=(batch, nt),
        in_specs=[
            pl.BlockSpec((tm, kvw), lambda b, i: (b * nt + i, OFF_K // kvw)),
            pl.BlockSpec((tm, kvw), lambda b, i: (b * nt + i, OFF_V // kvw)),
            pl.BlockSpec((1, kvw), lambda b, i: (0, 0)),
        ],
        out_specs=[
            pl.BlockSpec((tm, kvw), lambda b, i: (b * nt + i, 0)),
            pl.BlockSpec((tm, kvw), lambda b, i: (b * nt + i, 0)),
            pl.BlockSpec((1, kvw, tm), lambda b, i: (b, 0, i)),
            pl.BlockSpec((1, nb, kvw), lambda b, i: (b, i, 0)),
        ],
        out_shape=[
            jax.ShapeDtypeStruct((batch * seq, kvw), f32),
            jax.ShapeDtypeStruct((batch * seq, kvw), bf16),
            jax.ShapeDtypeStruct((batch, kvw, seq), bf16),
            jax.ShapeDtypeStruct((batch, seq // MOBA_BLOCK, kvw), f32),
        ],
        compiler_params=_cparams(("arbitrary", "arbitrary")),
        name="kprep",
    )(proj, proj, k_gain_row)


def _top3_rows(gate, jb):
    sel = jnp.zeros(gate.shape, jnp.bool_)
    for _ in range(MOBA_TOPK):
        m = jnp.max(gate, axis=0, keepdims=True)
        idx = jnp.min(jnp.where(gate == m, jb, gate.shape[0]), axis=0, keepdims=True)
        hit = jb == idx
        sel = jnp.logical_or(sel, jnp.logical_and(hit, m > -jnp.inf))
        gate = jnp.where(hit, -jnp.inf, gate)
    return sel


def _moba_prompt_kernel(q_ref, kb_ref, vt_ref, km_ref, qg_ref, sl_ref, o_ref,
                        qbd_ref, pen_ref, ali_ref, acc_ref, m_ref, *, tq):
    g = pl.program_id(1)
    qt = pl.program_id(2)
    per_blk = MOBA_BLOCK // tq
    own = qt // per_blk
    cols = Q_PER_KV * tq
    nblk = km_ref.shape[2]

    qT = q_ref[...].T
    parts = []
    for r in range(Q_PER_KV):
        xr = qT[r * HEAD_DIM:(r + 1) * HEAD_DIM, :]
        ms = jnp.mean(xr * xr, axis=0, keepdims=True)
        parts.append(xr * lax.rsqrt(ms + EPS) * qg_ref[...])
    qn = jnp.concatenate(parts, axis=1)

    km = km_ref[0, 0]
    kh, kmid, kl = _split3(km)
    qh, qmid, ql = _split3(qn)
    gate = _dot(kh, qh) + _dot(kh, qmid) + _dot(kmid, qh) + _dot(kh, ql) + _dot(kl, qh) + _dot(kmid, qmid)
    jb = lax.broadcasted_iota(jnp.int32, (nblk, cols), 0)
    gate = jnp.where(jb < own, gate, -jnp.inf)
    sel = _top3_rows(gate, jb)
    pen_ref[...] = jnp.where(jnp.logical_or(sel, jb == own), 0.0, NEG)

    qbd_ref[...] = jnp.zeros(qbd_ref.shape, bf16)
    qbd_ref[pl.ds(pl.multiple_of(g * HEAD_DIM, HEAD_DIM), HEAD_DIM), :] = (
        qn * (HEAD_DIM ** -0.5 * LOG2E)).astype(bf16)

    slope = sl_ref[0]
    kk = lax.broadcasted_iota(jnp.int32, (MOBA_BLOCK, cols), 0)
    ali_ref[...] = kk.astype(f32) * slope
    ones = jnp.ones((2 * SUBLANES, MOBA_BLOCK), bf16)

    def v_aug(j):
        vj = vt_ref[0, :, pl.ds(pl.multiple_of(j * MOBA_BLOCK, MOBA_BLOCK), MOBA_BLOCK)]
        return jnp.concatenate([vj, ones], axis=0)

    def scores(j):
        kj = kb_ref[pl.ds(pl.multiple_of(j * MOBA_BLOCK, MOBA_BLOCK), MOBA_BLOCK), :]
        return _dot(kj, qbd_ref[...]) + ali_ref[...]

    qin = (qt % per_blk) * tq + lax.broadcasted_iota(jnp.int32, (MOBA_BLOCK, cols), 1) % tq
    s = jnp.where(kk <= qin, scores(own), NEG)
    m0 = jnp.max(s, axis=0, keepdims=True)
    p = jnp.exp2(s - m0)
    acc_ref[...] = _dot(v_aug(own), p.astype(bf16))
    m_ref[...] = m0

    def body(j, c):
        s = scores(j)
        shift = slope * ((j - own) * MOBA_BLOCK).astype(f32) + pen_ref[pl.ds(j, 1), :]
        m_old = m_ref[...]
        m_new = jnp.maximum(m_old, jnp.max(s, axis=0, keepdims=True) + shift)
        alpha = jnp.exp2(m_old - m_new)
        p = jnp.exp2(s - (m_new - shift))
        acc_ref[...] = acc_ref[...] * alpha + _dot(v_aug(j), p.astype(bf16))
        m_ref[...] = m_new
        return c

    lax.fori_loop(0, own, body, 0)

    acc = acc_ref[...]
    o = acc[0:HEAD_DIM, :] / acc[HEAD_DIM:HEAD_DIM + 1, :]
    oT = jnp.concatenate([o[:, r * tq:(r + 1) * tq] for r in range(Q_PER_KV)], axis=0)
    o_ref[...] = oT.T


def moba_prompt(proj, kb, vt, kmean_g, q_gain_col, slopes2, batch, seq, tq):
    nq = seq // tq
    qw = Q_PER_KV * HEAD_DIM
    kvw = N_KV_HEADS * HEAD_DIM
    nblk = seq // MOBA_BLOCK
    cols = Q_PER_KV * tq
    return pl.pallas_call(
        functools.partial(_moba_prompt_kernel, tq=tq),
        grid=(batch, N_KV_HEADS, nq),
        in_specs=[
            pl.BlockSpec((tq, qw), lambda b, g, i: (b * nq + i, OFF_Q // qw + g)),
            pl.BlockSpec((seq, kvw), lambda b, g, i: (b, 0)),
            pl.BlockSpec((1, HEAD_DIM, seq), lambda b, g, i: (b, g, 0)),
            pl.BlockSpec((1, 1, nblk, HEAD_DIM), lambda b, g, i: (b, g, 0, 0)),
            pl.BlockSpec((HEAD_DIM, tq), lambda b, g, i: (0, 0)),
            pl.BlockSpec((1, 1, cols), lambda b, g, i: (g, 0, 0)),
        ],
        out_specs=pl.BlockSpec((tq, qw), lambda b, g, i: (b * nq + i, g)),
        out_shape=jax.ShapeDtypeStruct((batch * seq, N_HEADS * HEAD_DIM), f32),
        scratch_shapes=[
            pltpu.VMEM((kvw, cols), bf16),
            pltpu.VMEM((nblk, cols), f32),
            pltpu.VMEM((MOBA_BLOCK, cols), f32),
            pltpu.VMEM((HEAD_DIM + 2 * SUBLANES, cols), f32),
            pltpu.VMEM((1, cols), f32),
        ],
        compiler_params=_cparams(("arbitrary", "arbitrary", "arbitrary")),
        name="moba_prompt",
    )(proj, kb, vt, kmean_g, q_gain_col, slopes2)


def _head_expand():
    r = lax.broadcasted_iota(jnp.int32, (LANES, D_INNER), 0)
    c = lax.broadcasted_iota(jnp.int32, (LANES, D_INNER), 1) // SSM_HEAD_DIM
    return jnp.where(r == c, 1.0, 0.0).astype(bf16)


def _ssd_conv_act(ext_ref, rows, cw_ref, cb_ref, base):
    conv = cb_ref[...] + cw_ref[0:1, :] * ext_ref[pl.ds(base, rows), :]
    for i in range(1, CONV_WIDTH):
        conv = conv + cw_ref[i:i + 1, :] * ext_ref[pl.ds(base + i, rows), :]
    return _silu(conv)


def _ssd_chunk(act, dt_raw, allowed, dtb_ref, alog_ref):
    rows = act.shape[0]
    xs = act[:, :D_INNER]
    gn = SSM_GROUPS * D_STATE
    bm = act[:, D_INNER:D_INNER + gn]
    cm = act[:, D_INNER + gn:]
    dt = _softplus(dt_raw + dtb_ref[...])
    a = -jnp.exp(alog_ref[...])
    mask_b = jnp.where(allowed, 1.0, 0.0).astype(bf16)
    cs = _dot_exact_lhs(mask_b, dt * a)
    expand = _head_expand()
    dt_e = _dot_exact_rhs(dt, expand)
    cs_e = _dot_exact_rhs(cs, expand)
    xdt = xs * dt_e
    csT = cs.T
    xdt_b = xdt.astype(bf16)
    ypairs = []
    lane = lax.broadcasted_iota(jnp.int32, (rows, LANES), 1)
    for g in range(SSM_GROUPS):
        cb = _dot_nt(cm[:, g * D_STATE:(g + 1) * D_STATE].astype(bf16), bm[:, g * D_STATE:(g + 1) * D_STATE].astype(bf16))
        hpg = SSM_HEADS // SSM_GROUPS
        for pair in range(hpg // 2):
            res = []
            for k in range(2):
                h = g * hpg + pair * 2 + k
                seg = cs[:, h:h + 1] - csT[h:h + 1, :]
                m = (cb * jnp.exp(jnp.where(allowed, seg, NEG))).astype(bf16)
                lo = (g * hpg + pair * 2) * SSM_HEAD_DIM
                res.append(_dot(m, xdt_b[:, lo:lo + LANES]))
            ypairs.append(jnp.where(lane < SSM_HEAD_DIM, res[0], res[1]))
    y_diag = jnp.concatenate(ypairs, axis=1)
    return xs, bm, cm, dt_e, cs_e, xdt, y_diag


def _ssd_finish(y, xs, z, dexp_ref, norm_ref):
    y = (y + dexp_ref[...] * xs) * _silu(z)
    gw = D_INNER // SSM_GROUPS
    outs = []
    for g in range(SSM_GROUPS):
        yg = y[:, g * gw:(g + 1) * gw]
        ms = jnp.mean(yg * yg, axis=-1, keepdims=True)
        outs.append(yg * lax.rsqrt(ms + EPS) * norm_ref[:, g * gw:(g + 1) * gw])
    return jnp.concatenate(outs, axis=1)


def _ssd_prompt_kernel(xbc_ref, z_ref, dt_ref, cw_ref, cb_ref, dtb_ref, alog_ref, dexp_ref, norm_ref,
                       y_ref, h_ref, ext_ref, ht_ref, *, rows):
    c = pl.program_id(1)

    @pl.when(c == 0)
    def _():
        ext_ref[pl.ds(0, SUBLANES), :] = jnp.zeros((SUBLANES, CONV_DIM), f32)
        ht_ref[...] = jnp.zeros(ht_ref.shape, f32)

    xt = xbc_ref[...]
    ext_ref[pl.ds(SUBLANES, rows), :] = xt
    act = _ssd_conv_act(ext_ref, rows, cw_ref, cb_ref, base=SUBLANES - (CONV_WIDTH - 1))
    ext_ref[pl.ds(0, SUBLANES), :] = xt[rows - SUBLANES:rows, :]

    ri = lax.broadcasted_iota(jnp.int32, (rows, rows), 0)
    ci = lax.broadcasted_iota(jnp.int32, (rows, rows), 1)
    xs, bm, cm, dt_e, cs_e, xdt, y = _ssd_chunk(act, dt_ref[...], ci <= ri, dtb_ref, alog_ref)

    cs_last = cs_e[rows - 1:rows, :]
    ecs = jnp.exp(cs_e)
    xdte = (xdt * jnp.exp(cs_last - cs_e)).astype(bf16)
    dec = jnp.exp(cs_last)
    hw = D_INNER // SSM_GROUPS
    yoff = []
    for g in range(SSM_GROUPS):
        ht_g = ht_ref[:, g * hw:(g + 1) * hw]
        yoff.append(_dot(cm[:, g * D_STATE:(g + 1) * D_STATE].astype(bf16), ht_g.astype(bf16)))
        upd = _dot_tn(bm[:, g * D_STATE:(g + 1) * D_STATE].astype(bf16), xdte[:, g * hw:(g + 1) * hw])
        ht_ref[:, g * hw:(g + 1) * hw] = ht_g * dec[:, g * hw:(g + 1) * hw] + upd
    y = y + jnp.concatenate(yoff, axis=1) * ecs
    y_ref[...] = _ssd_finish(y, xs, z_ref[...], dexp_ref, norm_ref)

    @pl.when(c == pl.num_programs(1) - 1)
    def _():
        h_ref[0] = ht_ref[...].T.reshape(SSM_HEADS, SSM_HEAD_DIM, D_STATE)


def ssd_prompt(proj, conv_w, conv_b_row, dtb_row, alog_row, dexp_row, norm_row, batch, seq, rows):
    nc = seq // rows
    return pl.pallas_call(
        functools.partial(_ssd_prompt_kernel, rows=rows),
        grid=(batch, nc),
        in_specs=[
            pl.BlockSpec((rows, CONV_DIM), lambda b, c: (b * nc + c, OFF_XBC // CONV_DIM)),
            pl.BlockSpec((rows, D_INNER), lambda b, c: (b * nc + c, OFF_Z // D_INNER)),
            pl.BlockSpec((rows, LANES), lambda b, c: (b * nc + c, OFF_DT // LANES)),
            pl.BlockSpec((CONV_WIDTH, CONV_DIM), lambda b, c: (0, 0)),
            pl.BlockSpec((1, CONV_DIM), lambda b, c: (0, 0)),
            pl.BlockSpec((1, LANES), lambda b, c: (0, 0)),
            pl.BlockSpec((1, LANES), lambda b, c: (0, 0)),
            pl.BlockSpec((1, D_INNER), lambda b, c: (0, 0)),
            pl.BlockSpec((1, D_INNER), lambda b, c: (0, 0)),
        ],
        out_specs=[
            pl.BlockSpec((rows, D_INNER), lambda b, c: (b * nc + c, 0)),
            pl.BlockSpec((1, SSM_HEADS, SSM_HEAD_DIM, D_STATE), lambda b, c: (b, 0, 0, 0)),
        ],
        out_shape=[
            jax.ShapeDtypeStruct((batch * seq, D_INNER), f32),
            jax.ShapeDtypeStruct((batch, SSM_HEADS, SSM_HEAD_DIM, D_STATE), f32),
        ],
        scratch_shapes=[
            pltpu.VMEM((rows + SUBLANES, CONV_DIM), f32),
            pltpu.VMEM((D_STATE, D_INNER), f32),
        ],
        compiler_params=_cparams(("arbitrary", "arbitrary")),
        name="ssd_prompt",
    )(proj, proj, proj, conv_w, conv_b_row, dtb_row, alog_row, dexp_row, norm_row)


def _mem_attend_kernel(q_ref, mk_ref, mv_ref, g_ref, o_ref):
    outs = []
    for h in range(MEM_HEADS):
        lo = h * MEM_HEAD_DIM
        q = q_ref[:, lo:lo + MEM_HEAD_DIM]
        ms = jnp.mean(q * q, axis=-1, keepdims=True)
        qn = (q * lax.rsqrt(ms + EPS) * g_ref[...] * (MEM_HEAD_DIM ** -0.5)).astype(bf16)
        s = _dot_nt(qn, mk_ref[0, :, lo:lo + MEM_HEAD_DIM].astype(bf16))
        m = jnp.max(s, axis=-1, keepdims=True)
        p = jnp.exp(s - m)
        l = jnp.sum(p, axis=-1, keepdims=True)
        outs.append(_dot(p.astype(bf16), mv_ref[0, :, lo:lo + MEM_HEAD_DIM].astype(bf16)) / l)
    o_ref[...] = jnp.concatenate(outs, axis=1)


def mem_attend(proj, mk, mv, mq_gain_row, nseq, rows_per_seq, tq):
    w = MEM_HEADS * MEM_HEAD_DIM
    nq = rows_per_seq // tq
    return pl.pallas_call(
        _mem_attend_kernel,
        grid=(nseq, nq),
        in_specs=[
            pl.BlockSpec((tq, w), lambda b, i: (b * nq + i, OFF_MQ // w)),
            pl.BlockSpec((1, MEM_TOKENS, w), lambda b, i: (b, 0, 0)),
            pl.BlockSpec((1, MEM_TOKENS, w), lambda b, i: (b, 0, 0)),
            pl.BlockSpec((1, MEM_HEAD_DIM), lambda b, i: (0, 0)),
        ],
        out_specs=pl.BlockSpec((tq, w), lambda b, i: (b * nq + i, 0)),
        out_shape=jax.ShapeDtypeStruct((nseq * rows_per_seq, w), f32),
        compiler_params=_cparams(("arbitrary", "arbitrary")),
        name="mem_attend",
    )(proj, mk, mv, mq_gain_row)


def _merge_kernel(x_ref, a_ref, s_ref, m_ref, ga_ref, gs_ref, gm_ref, wa_ref, ws_ref, wm_ref, wo_ref, o_ref):
    mixed = _sigmoid(ga_ref[...]) * _dot(a_ref[...].astype(bf16), wa_ref[...])
    mixed = mixed + _sigmoid(gs_ref[...]) * _dot(s_ref[...].astype(bf16), ws_ref[...])
    mixed = mixed + _sigmoid(gm_ref[...]) * _dot(m_ref[...].astype(bf16), wm_ref[...])
    o_ref[...] = x_ref[...] + _dot(mixed.astype(bf16), wo_ref[...])


def merge(x, attn_o, ssm_o, mem_o, proj, wa, ws, wm, wo, tm):
    t, d = x.shape
    row = lambda i: (i, 0)
    const = lambda i: (0, 0)
    gate_spec = lambda k: pl.BlockSpec((tm, d), lambda i: (i, OFF_GATES // d + k))
    w_spec = pl.BlockSpec((d, d), const)
    return pl.pallas_call(
        _merge_kernel,
        grid=(t // tm,),
        in_specs=[pl.BlockSpec((tm, d), row)] * 4 + [gate_spec(0), gate_spec(1), gate_spec(2)] + [w_spec] * 4,
        out_specs=pl.BlockSpec((tm, d), row),
        out_shape=jax.ShapeDtypeStruct((t, d), f32),
        compiler_params=_cparams(("arbitrary",)),
        name="merge",
    )(x, attn_o, ssm_o, mem_o, proj, proj, proj, wa, ws, wm, wo)


def _ffn_kernel(x_ref, g_ref, wg_ref, wu_ref, wd_ref, o_ref):
    x = x_ref[...]
    ms = jnp.mean(x * x, axis=-1, keepdims=True)
    h = (x * lax.rsqrt(ms + EPS) * g_ref[...]).astype(bf16)
    act = _silu(_dot(h, wg_ref[...])) * _dot(h, wu_ref[...])
    o_ref[...] = x + _dot(act.astype(bf16), wd_ref[...])


def ffn(x, gain_row, wg, wu, wd, tm):
    t, d = x.shape
    dff = wg.shape[1]
    const = lambda i: (0, 0)
    single = pl.Buffered(1)
    return pl.pallas_call(
        _ffn_kernel,
        grid=(t // tm,),
        in_specs=[
            pl.BlockSpec((tm, d), lambda i: (i, 0)),
            pl.BlockSpec((1, d), const),
            pl.BlockSpec((d, dff), const, pipeline_mode=single),
            pl.BlockSpec((d, dff), const, pipeline_mode=single),
            pl.BlockSpec((dff, d), const, pipeline_mode=single),
        ],
        out_specs=pl.BlockSpec((tm, d), lambda i: (i, 0)),
        out_shape=jax.ShapeDtypeStruct((t, d), f32),
        compiler_params=_cparams(("arbitrary",)),
        name="ffn",
    )(x, gain_row, wg, wu, wd)


def _top3_cols(gate, jb):
    sel = jnp.zeros(gate.shape, jnp.bool_)
    for _ in range(MOBA_TOPK):
        m = jnp.max(gate, axis=-1, keepdims=True)
        idx = jnp.min(jnp.where(gate == m, jb, gate.shape[1]), axis=-1, keepdims=True)
        hit = jb == idx
        sel = jnp.logical_or(sel, jnp.logical_and(hit, m > -jnp.inf))
        gate = jnp.where(hit, -jnp.inf, gate)
    return sel


def _moba_sample_kernel(pt_ref, q_ref, kn_ref, vn_ref, qg_ref, sl_ref, ck_hbm, cv_hbm, o_ref,
                        kbuf, vbuf, s_ref, sem, *, n_pages):
    b = pl.program_id(0)
    kvw = N_KV_HEADS * HEAD_DIM
    nrow = Q_PER_KV * N_KV_HEADS * SAMPLE_ROWS
    nblk = n_pages * PAGE_SIZE // MOBA_BLOCK
    ppb = MOBA_BLOCK // PAGE_SIZE
    past = n_pages * PAGE_SIZE

    def k_copy(p):
        return pltpu.make_async_copy(ck_hbm.at[pt_ref[b, p]], kbuf.at[p], sem.at[0])

    def v_copy(p):
        return pltpu.make_async_copy(cv_hbm.at[pt_ref[b, p]], vbuf.at[p], sem.at[1])

    for p in range(n_pages):
        k_copy(p).start()
    for p in range(n_pages):
        v_copy(p).start()

    qb = q_ref[...]
    lane_g = lax.broadcasted_iota(jnp.int32, (SAMPLE_ROWS, kvw), 1) // HEAD_DIM
    pieces = []
    for r in range(Q_PER_KV):
        slab = qb[:, r * kvw:(r + 1) * kvw]
        for g in range(N_KV_HEADS):
            pieces.append(jnp.where(lane_g == g, slab, 0.0))
    qbd = jnp.concatenate(pieces, axis=0)
    ms = jnp.sum(qbd * qbd, axis=-1, keepdims=True) * (1.0 / HEAD_DIM)
    qn = qbd * lax.rsqrt(ms + EPS) * qg_ref[...]
    qs = (qn * (HEAD_DIM ** -0.5 * LOG2E)).astype(bf16)

    for p in range(n_pages):
        k_copy(p).wait()

    kmean = jnp.sum(kbuf[...].reshape(nblk, MOBA_BLOCK, kvw), axis=1) * (1.0 / MOBA_BLOCK)
    kh, kmid, kl = _split3(kmean)
    qh, qmid, ql = _split3(qn)
    gate = (_dot_nt(qh, kh) + _dot_nt(qh, kmid) + _dot_nt(qmid, kh) + _dot_nt(qh, kl) + _dot_nt(ql, kh)
            + _dot_nt(qmid, kmid))
    jb = lax.broadcasted_iota(jnp.int32, (nrow, nblk), 1)
    pen = jnp.where(_top3_cols(gate, jb), 0.0, NEG)

    slope = sl_ref[...]
    slope2 = jnp.concatenate([slope, slope], axis=1)
    kk = lax.broadcasted_iota(jnp.int32, (nrow, MOBA_BLOCK), 1)
    for j in range(nblk):
        kj = kbuf[j * ppb:(j + 1) * ppb].reshape(MOBA_BLOCK, kvw).astype(bf16)
        rel = (kk + (j * MOBA_BLOCK - past)).astype(f32)
        s_ref[:, j * MOBA_BLOCK:(j + 1) * MOBA_BLOCK] = _dot_nt(qs, kj) + slope2 * rel + pen[:, j:j + 1]

    zpad = jnp.zeros((LANES - SAMPLE_ROWS, kvw), f32)
    knew = jnp.concatenate([kn_ref[...], zpad], axis=0).astype(bf16)
    vnew = jnp.concatenate([vn_ref[...], zpad], axis=0).astype(bf16)
    kn_lane = lax.broadcasted_iota(jnp.int32, (nrow, LANES), 1)
    tok = lax.broadcasted_iota(jnp.int32, (nrow, LANES), 0) % SAMPLE_ROWS
    s_own = _dot_nt(qs, knew) + slope * kn_lane.astype(f32)
    s_ref[:, past:past + LANES] = jnp.where(kn_lane <= tok, s_own, NEG)

    m = jnp.max(s_ref[...], axis=-1, keepdims=True)
    p_own = jnp.exp2(s_ref[:, past:past + LANES] - m)
    l = jnp.sum(p_own, axis=-1, keepdims=True)
    acc = _dot(p_own.astype(bf16), vnew)

    for p in range(n_pages):
        v_copy(p).wait()

    for j in range(nblk):
        pj = jnp.exp2(s_ref[:, j * MOBA_BLOCK:(j + 1) * MOBA_BLOCK] - m)
        l = l + jnp.sum(pj, axis=-1, keepdims=True)
        vj = vbuf[j * ppb:(j + 1) * ppb].reshape(MOBA_BLOCK, kvw).astype(bf16)
        acc = acc + _dot(pj.astype(bf16), vj)
    o = acc / l

    slabs = []
    for r in range(Q_PER_KV):
        slab = jnp.zeros((SAMPLE_ROWS, kvw), f32)
        for g in range(N_KV_HEADS):
            lo = (r * N_KV_HEADS + g) * SAMPLE_ROWS
            slab = slab + jnp.where(lane_g == g, o[lo:lo + SAMPLE_ROWS, :], 0.0)
        slabs.append(slab)
    o_ref[...] = jnp.concatenate(slabs, axis=1)


def moba_sample(page_table, proj, kn_new, ck, cv, q_gain_row, slope_rows, nseq):
    kvw = N_KV_HEADS * HEAD_DIM
    qw = N_HEADS * HEAD_DIM
    n_pages = page_table.shape[1]
    nrow = Q_PER_KV * N_KV_HEADS * SAMPLE_ROWS
    past = n_pages * PAGE_SIZE
    grid_spec = pltpu.PrefetchScalarGridSpec(
        num_scalar_prefetch=1,
        grid=(nseq,),
        in_specs=[
            pl.BlockSpec((SAMPLE_ROWS, qw), lambda b, pt: (b, OFF_Q // qw)),
            pl.BlockSpec((SAMPLE_ROWS, kvw), lambda b, pt: (b, 0)),
            pl.BlockSpec((SAMPLE_ROWS, kvw), lambda b, pt: (b, OFF_V // kvw)),
            pl.BlockSpec((1, kvw), lambda b, pt: (0, 0)),
            pl.BlockSpec((nrow, LANES), lambda b, pt: (0, 0)),
            pl.BlockSpec(memory_space=pl.ANY),
            pl.BlockSpec(memory_space=pl.ANY),
        ],
        out_specs=pl.BlockSpec((SAMPLE_ROWS, qw), lambda b, pt: (b, 0)),
        scratch_shapes=[
            pltpu.VMEM((n_pages, PAGE_SIZE, kvw), f32),
            pltpu.VMEM((n_pages, PAGE_SIZE, kvw), f32),
            pltpu.VMEM((nrow, past + LANES), f32),
            pltpu.SemaphoreType.DMA((2,)),
        ],
    )
    return pl.pallas_call(
        functools.partial(_moba_sample_kernel, n_pages=n_pages),
        grid_spec=grid_spec,
        out_shape=jax.ShapeDtypeStruct((nseq * SAMPLE_ROWS, qw), f32),
        compiler_params=_cparams(("arbitrary",)),
        name="moba_sample",
    )(page_table, proj, kn_new, proj, q_gain_row, slope_rows, ck, cv)


def _ssd_sample_kernel(ext_ref, z_ref, dt_ref, h0_ref, cw_ref, cb_ref, dtb_ref, alog_ref, dexp_ref, norm_ref,
                       y_ref, h_ref, scr_ref, *, nseq_step, n_new):
    rows = nseq_step * SAMPLE_ROWS
    scr_ref[pl.ds(0, rows), :] = ext_ref[...]
    scr_ref[pl.ds(rows, SUBLANES), :] = jnp.zeros((SUBLANES, CONV_DIM), f32)
    act = _ssd_conv_act(scr_ref, rows, cw_ref, cb_ref, base=0)

    ri = lax.broadcasted_iota(jnp.int32, (rows, rows), 0)
    ci = lax.broadcasted_iota(jnp.int32, (rows, rows), 1)
    allowed = jnp.logical_and(ri // SAMPLE_ROWS == ci // SAMPLE_ROWS,
                              jnp.logical_and(ci % SAMPLE_ROWS <= ri % SAMPLE_ROWS, ci % SAMPLE_ROWS < n_new))
    xs, bm, cm, dt_e, cs_e, xdt, y = _ssd_chunk(act, dt_ref[...], allowed, dtb_ref, alog_ref)

    valid = lax.broadcasted_iota(jnp.int32, (SAMPLE_ROWS, 1), 0) < n_new
    ecs = jnp.exp(cs_e)
    hw = D_INNER // SSM_GROUPS
    hpg = SSM_HEADS // SSM_GROUPS
    yoff_rows = []
    for i in range(nseq_step):
        lo = i * SAMPLE_ROWS
        cs_i = cs_e[lo:lo + SAMPLE_ROWS, :]
        cs_last = cs_e[lo + n_new - 1:lo + n_new, :]
        xdte = jnp.where(valid, xdt[lo:lo + SAMPLE_ROWS, :] * jnp.exp(cs_last - cs_i), 0.0)
        yg = []
        upd = []
        for g in range(SSM_GROUPS):
            h0g = h0_ref[i, g * hpg:(g + 1) * hpg].reshape(hw, D_STATE)
            yg.append(_dot_nt(cm[lo:lo + SAMPLE_ROWS, g * D_STATE:(g + 1) * D_STATE], h0g))
            upd.append(_dot_tn(xdte[:, g * hw:(g + 1) * hw], bm[lo:lo + SAMPLE_ROWS, g * D_STATE:(g + 1) * D_STATE]))
        yoff_rows.append(jnp.concatenate(yg, axis=1))
        dec_col = jnp.broadcast_to(jnp.exp(cs_last), (SUBLANES, D_INNER)).T[:, 0:1]
        h_new = h0_ref[i].reshape(D_INNER, D_STATE) * dec_col + jnp.concatenate(upd, axis=0)
        h_ref[i] = h_new.reshape(SSM_HEADS, SSM_HEAD_DIM, D_STATE)
    y = y + jnp.concatenate(yoff_rows, axis=0) * ecs
    y_ref[...] = _ssd_finish(y, xs, z_ref[...], dexp_ref, norm_ref)


def ssd_sample(ext, proj, h0, conv_w, conv_b_row, dtb_row, alog_row, dexp_row, norm_row, nseq, nseq_step, n_new):
    rows = nseq_step * SAMPLE_ROWS
    const = lambda i: (0, 0)
    state_spec = pl.BlockSpec((nseq_step, SSM_HEADS, SSM_HEAD_DIM, D_STATE), lambda i: (i, 0, 0, 0))
    return pl.pallas_call(
        functools.partial(_ssd_sample_kernel, nseq_step=nseq_step, n_new=n_new),
        grid=(nseq // nseq_step,),
        in_specs=[
            pl.BlockSpec((rows, CONV_DIM), lambda i: (i, 0)),
            pl.BlockSpec((rows, D_INNER), lambda i: (i, OFF_Z // D_INNER)),
            pl.BlockSpec((rows, LANES), lambda i: (i, OFF_DT // LANES)),
            state_spec,
            pl.BlockSpec((CONV_WIDTH, CONV_DIM), const),
            pl.BlockSpec((1, CONV_DIM), const),
            pl.BlockSpec((1, LANES), const),
            pl.BlockSpec((1, LANES), const),
            pl.BlockSpec((1, D_INNER), const),
            pl.BlockSpec((1, D_INNER), const),
        ],
        out_specs=[pl.BlockSpec((rows, D_INNER), lambda i: (i, 0)), state_spec],
        out_shape=[
            jax.ShapeDtypeStruct((nseq * SAMPLE_ROWS, D_INNER), f32),
            jax.ShapeDtypeStruct(h0.shape, f32),
        ],
        scratch_shapes=[pltpu.VMEM((rows + SUBLANES, CONV_DIM), f32)],
        compiler_params=_cparams(("arbitrary",)),
        name="ssd_sample",
    )(ext, proj, proj, h0, conv_w, conv_b_row, dtb_row, alog_row, dexp_row, norm_row)


IN_SIZES = (N_HEADS * HEAD_DIM, N_KV_HEADS * HEAD_DIM, N_KV_HEADS * HEAD_DIM, D_INNER, CONV_DIM, SSM_HEADS,
            MEM_HEADS * MEM_HEAD_DIM, 3 * D_MODEL)
TM = 512
TN_PROJ = 2944
TQ = 128
SSD_ROWS = 256
SSD_SAMPLE_SEQS = 8


def _pad_lanes(v):
    return jnp.pad(v, (0, LANES - v.shape[0])).reshape(1, LANES)


def kernel(x_prompt, x_sample, mem_prompt, cache_k, cache_v, page_table, state_conv, state_ssm, cache_mem_k, cache_mem_v, norm_mix, w_in, q_norm, k_norm, conv_w, conv_b, dt_bias, a_log, d_skip, ssm_norm, mem_norm, w_mem_kv, mq_norm, mk_norm, w_attn_br, w_ssm_br, w_mem_br, w_out, norm_ffn, w_gate, w_up, w_down):
    assert w_in.shape[0] == 1, "single layer"
    batch, seq, d = x_prompt.shape
    nseq, n_new, _ = x_sample.shape
    kvw = N_KV_HEADS * HEAD_DIM
    qw = N_HEADS * HEAD_DIM

    offs = np.cumsum(IN_SIZES)[:-1].tolist()
    wq, wk, wv, wz, wxbc, wdt, wmq, wgates = jnp.split(w_in[0], offs, axis=1)
    wdt = jnp.pad(wdt, ((0, 0), (0, LANES - SSM_HEADS)))
    tail = [wz, wxbc, wmq, wgates, wk, wv, wdt]
    w_prompt = jnp.concatenate([wq] + tail, axis=1).astype(bf16)
    wq_rgd = wq.reshape(d, N_KV_HEADS, Q_PER_KV, HEAD_DIM).transpose(0, 2, 1, 3).reshape(d, qw)
    w_sample = jnp.concatenate([wq_rgd] + tail, axis=1).astype(bf16)
    wa = w_attn_br[0].astype(bf16)
    wa_rgd = w_attn_br[0].reshape(N_KV_HEADS, Q_PER_KV, HEAD_DIM, d).transpose(1, 0, 2, 3).reshape(qw, d).astype(bf16)
    ws, wm, wo = w_ssm_br[0].astype(bf16), w_mem_br[0].astype(bf16), w_out[0].astype(bf16)
    wg, wu, wd = w_gate[0].astype(bf16), w_up[0].astype(bf16), w_down[0].astype(bf16)

    k_gain = jnp.tile(k_norm[0], N_KV_HEADS).reshape(1, kvw)
    conv_b_row = conv_b[0].reshape(1, CONV_DIM)
    dtb_row, alog_row = _pad_lanes(dt_bias[0]), _pad_lanes(a_log[0])
    dexp_row = jnp.repeat(d_skip[0], SSM_HEAD_DIM).reshape(1, D_INNER)
    ssm_norm_row = ssm_norm[0].reshape(1, D_INNER)
    mq_gain = mq_norm[0].reshape(1, MEM_HEAD_DIM)
    ffn_gain = norm_ffn[0].reshape(1, d)
    slopes2 = jnp.exp2(-8.0 * jnp.arange(1, N_HEADS + 1, dtype=f32) / N_HEADS) * LOG2E

    xp = x_prompt.reshape(batch * seq, d)
    proj = inproj(xp, norm_mix[0], w_prompt, TM, TN_PROJ)
    kn, kb, vt, kmean = kprep(proj, k_gain, batch, seq, 2048)
    kmean_g = kmean.reshape(batch, seq // MOBA_BLOCK, N_KV_HEADS, HEAD_DIM).transpose(0, 2, 1, 3)
    slope_cols = jnp.repeat(slopes2.reshape(N_KV_HEADS, Q_PER_KV), TQ, axis=1).reshape(N_KV_HEADS, 1, Q_PER_KV * TQ)
    q_gain_col = jnp.broadcast_to(q_norm[0][:, None], (HEAD_DIM, TQ))
    attn_o = moba_prompt(proj, kb, vt, kmean_g, q_gain_col, slope_cols, batch, seq, TQ)
    ssm_o, h_prompt = ssd_prompt(proj, conv_w[0], conv_b_row, dtb_row, alog_row, dexp_row, ssm_norm_row,
                                 batch, seq, SSD_ROWS)
    mem_kv = inproj(mem_prompt.reshape(batch * MEM_TOKENS, d), mem_norm[0], w_mem_kv[0].astype(bf16), TM, qw)
    mk = headnorm(mem_kv, 0, qw, jnp.tile(mk_norm[0], MEM_HEADS).reshape(1, qw), MEM_HEAD_DIM, TM)
    mv = mem_kv[:, qw:]
    mem_o = mem_attend(proj, mk.reshape(batch, MEM_TOKENS, qw), mv.reshape(batch, MEM_TOKENS, qw), mq_gain,
                       batch, seq, TM)
    x1 = merge(xp, attn_o, ssm_o, mem_o, proj, wa, ws, wm, wo, TM)
    y_prompt = ffn(x1, ffn_gain, wg, wu, wd, TM).reshape(batch, seq, d)

    k_prompt = kn.reshape(1, batch, seq, N_KV_HEADS, HEAD_DIM)
    v_prompt = proj[:, OFF_V:OFF_V + kvw].reshape(1, batch, seq, N_KV_HEADS, HEAD_DIM)
    conv_prompt = proj[:, OFF_XBC:OFF_XBC + CONV_DIM].reshape(batch, seq, CONV_DIM)[:, seq - (CONV_WIDTH - 1):][None]
    mem_k_prompt = mk.reshape(1, batch, MEM_TOKENS, MEM_HEADS, MEM_HEAD_DIM)
    mem_v_prompt = mv.reshape(1, batch, MEM_TOKENS, MEM_HEADS, MEM_HEAD_DIM)

    xs = jnp.pad(x_sample, ((0, 0), (0, SAMPLE_ROWS - n_new), (0, 0))).reshape(nseq * SAMPLE_ROWS, d)
    proj_s = inproj(xs, norm_mix[0], w_sample, TM, TN_PROJ)
    kn_s = headnorm(proj_s, OFF_K // kvw, kvw, k_gain, HEAD_DIM, TM)
    n_pool = cache_k.shape[1]
    slope_rows = jnp.broadcast_to(
        jnp.repeat(slopes2.reshape(N_KV_HEADS, Q_PER_KV).T.reshape(-1), SAMPLE_ROWS)[:, None], (N_HEADS * SAMPLE_ROWS, LANES))
    attn_s = moba_sample(page_table, proj_s, kn_s, cache_k[0].reshape(n_pool, PAGE_SIZE, kvw),
                         cache_v[0].reshape(n_pool, PAGE_SIZE, kvw), jnp.tile(q_norm[0], N_KV_HEADS).reshape(1, kvw),
                         slope_rows, nseq)
    xbc_new = proj_s[:, OFF_XBC:OFF_XBC + CONV_DIM].reshape(nseq, SAMPLE_ROWS, CONV_DIM)[:, :n_new]
    ext = jnp.concatenate(
        [state_conv[0], xbc_new, jnp.zeros((nseq, SAMPLE_ROWS - (CONV_WIDTH - 1) - n_new, CONV_DIM), f32)], axis=1)
    ssm_s, h_sample = ssd_sample(ext.reshape(nseq * SAMPLE_ROWS, CONV_DIM), proj_s, state_ssm[0], conv_w[0], conv_b_row,
                                 dtb_row, alog_row, dexp_row, ssm_norm_row, nseq, SSD_SAMPLE_SEQS, n_new)
    mem_s = mem_attend(proj_s, cache_mem_k[0].reshape(nseq, MEM_TOKENS, qw), cache_mem_v[0].reshape(nseq, MEM_TOKENS, qw),
                       mq_gain, nseq, SAMPLE_ROWS, SAMPLE_ROWS)
    x1s = merge(xs, attn_s, ssm_s, mem_s, proj_s, wa_rgd, ws, wm, wo, TM)
    y_sample = ffn(x1s, ffn_gain, wg, wu, wd, TM).reshape(nseq, SAMPLE_ROWS, d)[:, :n_new]

    k_sample = kn_s.reshape(nseq, SAMPLE_ROWS, N_KV_HEADS, HEAD_DIM)[:, :n_new][None]
    v_sample = proj_s[:, OFF_V:OFF_V + kvw].reshape(nseq, SAMPLE_ROWS, N_KV_HEADS, HEAD_DIM)[:, :n_new][None]
    conv_sample = ext[:, n_new:n_new + CONV_WIDTH - 1][None]

    return (y_prompt, y_sample, k_prompt, v_prompt, conv_prompt, h_prompt[None], mem_k_prompt, mem_v_prompt,
            k_sample, v_sample, conv_sample, h_sample[None])
```

```python
import functools
import math

import jax
import jax.numpy as jnp
import numpy as np
from jax import lax
from jax.experimental import pallas as pl
from jax.experimental.pallas import tpu as pltpu

f32 = jnp.float32
bf16 = jnp.bfloat16

D_MODEL = 1024
N_HEADS = 16
N_KV_HEADS = 4
HEAD_DIM = 64
Q_PER_KV = N_HEADS // N_KV_HEADS
MOBA_BLOCK = 256
MOBA_TOPK = 3
SSM_HEADS = 16
SSM_HEAD_DIM = 64
D_INNER = SSM_HEADS * SSM_HEAD_DIM
SSM_GROUPS = 4
D_STATE = 128
CONV_WIDTH = 4
CONV_DIM = D_INNER + 2 * SSM_GROUPS * D_STATE
MEM_TOKENS = 256
MEM_HEADS = 4
MEM_HEAD_DIM = 256
D_FF = 2816
EPS = 1e-6
PAGE_SIZE = 128

LANES = 128
SUBLANES = 8
LOG2E = 1.4426950408889634
NEG = -1e30
VMEM_LIMIT = 56 * 1024 * 1024

OFF_Q, OFF_Z, OFF_XBC, OFF_MQ, OFF_GATES, OFF_K, OFF_V, OFF_DT = 0, 1024, 2048, 4096, 5120, 8192, 8448, 8704
N_PROJ = 8832
SAMPLE_ROWS = 8
MOBA_GROUP = 4


def _cparams(sem):
    return pltpu.CompilerParams(dimension_semantics=sem, vmem_limit_bytes=VMEM_LIMIT)


def _split3(x):
    h = x.astype(bf16)
    r = x - h.astype(f32)
    m = r.astype(bf16)
    l = (r - m.astype(f32)).astype(bf16)
    return h, m, l


def _dot(a, b):
    return jnp.dot(a, b, preferred_element_type=f32)


def _dot_nt(a, b):
    return lax.dot_general(a, b, (((1,), (1,)), ((), ())), preferred_element_type=f32)


def _dot_tn(a, b):
    return lax.dot_general(a, b, (((0,), (0,)), ((), ())), preferred_element_type=f32)


def _dot_exact_rhs(x, m_bf16):
    h, m, l = _split3(x)
    return _dot(h, m_bf16) + _dot(m, m_bf16) + _dot(l, m_bf16)


def _dot_exact_lhs(m_bf16, x):
    h, m, l = _split3(x)
    return _dot(m_bf16, h) + _dot(m_bf16, m) + _dot(m_bf16, l)


def _silu(x):
    return x * (1.0 / (1.0 + jnp.exp(-x)))


def _sigmoid(x):
    return 1.0 / (1.0 + jnp.exp(-x))


def _softplus(x):
    return jnp.maximum(x, 0.0) + jnp.log(1.0 + jnp.exp(-jnp.abs(x)))


def _inproj_kernel(x_ref, g_ref, w_ref, o_ref):
    x = x_ref[...]
    ms = jnp.mean(x * x, axis=-1, keepdims=True)
    xn = (x * lax.rsqrt(ms + EPS) * g_ref[...]).astype(bf16)
    o_ref[...] = _dot(xn, w_ref[...])


def inproj(x, gain, w, tm, tn):
    t, d = x.shape
    n = w.shape[1]
    return pl.pallas_call(
        _inproj_kernel,
        grid=(n // tn, t // tm),
        in_specs=[
            pl.BlockSpec((tm, d), lambda j, i: (i, 0)),
            pl.BlockSpec((1, d), lambda j, i: (0, 0)),
            pl.BlockSpec((d, tn), lambda j, i: (0, j)),
        ],
        out_specs=pl.BlockSpec((tm, tn), lambda j, i: (i, j)),
        out_shape=jax.ShapeDtypeStruct((t, n), f32),
        compiler_params=_cparams(("arbitrary", "arbitrary")),
        name="inproj",
    )(x, gain.reshape(1, d), w)


def _seg_ones(width, seg):
    r = lax.broadcasted_iota(jnp.int32, (width, width), 0) // seg
    c = lax.broadcasted_iota(jnp.int32, (width, width), 1) // seg
    return jnp.where(r == c, 1.0, 0.0).astype(bf16)


def _seg_rmsnorm(x, gain_row, seg):
    ss = _dot_exact_rhs(x * x, _seg_ones(x.shape[1], seg))
    return x * lax.rsqrt(ss * (1.0 / seg) + EPS) * gain_row


def _headnorm_kernel(x_ref, g_ref, o_ref, *, seg):
    o_ref[...] = _seg_rmsnorm(x_ref[...], g_ref[...], seg)


def headnorm(x, col_block, width, gain_row, seg, tm):
    t = x.shape[0]
    return pl.pallas_call(
        functools.partial(_headnorm_kernel, seg=seg),
        grid=(t // tm,),
        in_specs=[
            pl.BlockSpec((tm, width), lambda i: (i, col_block)),
            pl.BlockSpec((1, width), lambda i: (0, 0)),
        ],
        out_specs=pl.BlockSpec((tm, width), lambda i: (i, 0)),
        out_shape=jax.ShapeDtypeStruct((t, width), f32),
        compiler_params=_cparams(("arbitrary",)),
        name="headnorm",
    )(x, gain_row)


def _kprep_kernel(k_ref, v_ref, g_ref, kn_ref, kb_ref, vt_ref, km_ref):
    kn = _seg_rmsnorm(k_ref[...], g_ref[...], HEAD_DIM)
    kn_ref[...] = kn
    kb_ref[...] = kn.astype(bf16)
    nb = kn.shape[0] // MOBA_BLOCK
    km_ref[0] = jnp.sum(kn.reshape(nb, MOBA_BLOCK, kn.shape[1]), axis=1) * (1.0 / MOBA_BLOCK)
    vt_ref[0] = v_ref[...].astype(bf16).T


def kprep(proj, k_gain_row, batch, seq, tm):
    kvw = N_KV_HEADS * HEAD_DIM
    nt = seq // tm
    nb = tm // MOBA_BLOCK
    return pl.pallas_call(
        _kprep_kernel,
        grid=(batch, nt),
        in_specs=[
            pl.BlockSpec((tm, kvw), lambda b, i: (b * nt + i, OFF_K // kvw)),
            pl.BlockSpec((tm, kvw), lambda b, i: (b * nt + i, OFF_V // kvw)),
            pl.BlockSpec((1, kvw), lambda b, i: (0, 0)),
        ],
        out_specs=[
            pl.BlockSpec((tm, kvw), lambda b, i: (b * nt + i, 0)),
            pl.BlockSpec((tm, kvw), lambda b, i: (b * nt + i, 0)),
            pl.BlockSpec((1, kvw, tm), lambda b, i: (b, 0, i)),
            pl.BlockSpec((1, nb, kvw), lambda b, i: (b, i, 0)),
        ],
        out_shape=[
            jax.ShapeDtypeStruct((batch * seq, kvw), f32),
            jax.ShapeDtypeStruct((batch * seq, kvw), bf16),
            jax.ShapeDtypeStruct((batch, kvw, seq), bf16),
            jax.ShapeDtypeStruct((batch, seq // MOBA_BLOCK, kvw), f32),
        ],
        compiler_params=_cparams(("arbitrary", "arbitrary")),
        name="kprep",
    )(proj, proj, k_gain_row)


def _top3_rows(gate, jb):
    sel = jnp.zeros(gate.shape, jnp.bool_)
    for _ in range(MOBA_TOPK):
        m = jnp.max(gate, axis=0, keepdims=True)
        idx = jnp.min(jnp.where(gate == m, jb, gate.shape[0]), axis=0, keepdims=True)
        hit = jb == idx
        sel = jnp.logical_or(sel, jnp.logical_and(hit, m > -jnp.inf))
        gate = jnp.where(hit, -jnp.inf, gate)
    return sel


def _moba_prompt_kernel(q_ref, kb_ref, vt_ref, km_ref, qg_ref, sl_ref, o_ref,
                        qbd_ref, pen_ref, ali_ref, acc_ref, m_ref, *, tq):
    g = pl.program_id(1)
    qt = pl.program_id(2)
    per_blk = MOBA_BLOCK // tq
    own = qt // per_blk
    cols = Q_PER_KV * tq
    nblk = km_ref.shape[2]

    qT = q_ref[...].T
    parts = []
    for r in range(Q_PER_KV):
        xr = qT[r * HEAD_DIM:(r + 1) * HEAD_DIM, :]
        ms = jnp.mean(xr * xr, axis=0, keepdims=True)
        parts.append(xr * lax.rsqrt(ms + EPS) * qg_ref[...])
    qn = jnp.concatenate(parts, axis=1)

    km = km_ref[0, 0]
    kh, kmid, kl = _split3(km)
    qh, qmid, ql = _split3(qn)
    gate = _dot(kh, qh) + _dot(kh, qmid) + _dot(kmid, qh) + _dot(kh, ql) + _dot(kl, qh) + _dot(kmid, qmid)
    jb = lax.broadcasted_iota(jnp.int32, (nblk, cols), 0)
    gate = jnp.where(jb < own, gate, -jnp.inf)
    sel = _top3_rows(gate, jb)
    pen_ref[...] = jnp.where(jnp.logical_or(sel, jb == own), 0.0, NEG)

    qbd_ref[...] = jnp.zeros(qbd_ref.shape, bf16)
    qbd_ref[pl.ds(pl.multiple_of(g * HEAD_DIM, HEAD_DIM), HEAD_DIM), :] = (
        qn * (HEAD_DIM ** -0.5 * LOG2E)).astype(bf16)

    slope = sl_ref[0]
    kk = lax.broadcasted_iota(jnp.int32, (MOBA_BLOCK, cols), 0)
    ali_ref[...] = kk.astype(f32) * slope
    span = MOBA_GROUP * MOBA_BLOCK
    ones = jnp.ones((2 * SUBLANES, span), bf16)
    qin = (qt % per_blk) * tq + lax.broadcasted_iota(jnp.int32, (MOBA_BLOCK, cols), 1) % tq
    causal = kk <= qin

    def group(grp, m_old, has_own):
        base = pl.multiple_of(grp * span, span)
        s_all = _dot(kb_ref[pl.ds(base, span), :], qbd_ref[...])
        ss, shifts = [], []
        for k in range(MOBA_GROUP):
            j = grp * MOBA_GROUP + k
            s = s_all[k * MOBA_BLOCK:(k + 1) * MOBA_BLOCK, :] + ali_ref[...]
            if has_own:
                s = jnp.where(jnp.logical_or(causal, j < own), s, NEG)
            ss.append(s)
            shifts.append(slope * ((j - own) * MOBA_BLOCK).astype(f32) + pen_ref[pl.ds(j, 1), :])
        m_new = m_old
        for s, sh in zip(ss, shifts):
            mj = jnp.max(s, axis=0, keepdims=True) + sh
            m_new = mj if m_new is None else jnp.maximum(m_new, mj)
        p = jnp.concatenate([jnp.exp2(s - (m_new - sh)).astype(bf16) for s, sh in zip(ss, shifts)], axis=0)
        v_aug = jnp.concatenate([vt_ref[0, :, pl.ds(base, span)], ones], axis=0)
        return m_new, _dot(v_aug, p)

    n_full = own // MOBA_GROUP
    m0, pv0 = group(n_full, None, True)
    acc_ref[...] = pv0
    m_ref[...] = m0

    def body(grp, c):
        m_old = m_ref[...]
        m_new, pv = group(grp, m_old, False)
        acc_ref[...] = acc_ref[...] * jnp.exp2(m_old - m_new) + pv
        m_ref[...] = m_new
        return c

    lax.fori_loop(0, n_full, body, 0)

    acc = acc_ref[...]
    o = acc[0:HEAD_DIM, :] / acc[HEAD_DIM:HEAD_DIM + 1, :]
    oT = jnp.concatenate([o[:, r * tq:(r + 1) * tq] for r in range(Q_PER_KV)], axis=0)
    o_ref[...] = oT.T


def moba_prompt(proj, kb, vt, kmean_g, q_gain_col, slopes2, batch, seq, tq):
    nq = seq // tq
    qw = Q_PER_KV * HEAD_DIM
    kvw = N_KV_HEADS * HEAD_DIM
    nblk = seq // MOBA_BLOCK
    cols = Q_PER_KV * tq
    return pl.pallas_call(
        functools.partial(_moba_prompt_kernel, tq=tq),
        grid=(batch, N_KV_HEADS, nq),
        in_specs=[
            pl.BlockSpec((tq, qw), lambda b, g, i: (b * nq + i, OFF_Q // qw + g)),
            pl.BlockSpec((seq, kvw), lambda b, g, i: (b, 0)),
            pl.BlockSpec((1, HEAD_DIM, seq), lambda b, g, i: (b, g, 0)),
            pl.BlockSpec((1, 1, nblk, HEAD_DIM), lambda b, g, i: (b, g, 0, 0)),
            pl.BlockSpec((HEAD_DIM, tq), lambda b, g, i: (0, 0)),
            pl.BlockSpec((1, 1, cols), lambda b, g, i: (g, 0, 0)),
        ],
        out_specs=pl.BlockSpec((tq, qw), lambda b, g, i: (b * nq + i, g)),
        out_shape=jax.ShapeDtypeStruct((batch * seq, N_HEADS * HEAD_DIM), f32),
        scratch_shapes=[
            pltpu.VMEM((kvw, cols), bf16),
            pltpu.VMEM((nblk, cols), f32),
            pltpu.VMEM((MOBA_BLOCK, cols), f32),
            pltpu.VMEM((HEAD_DIM + 2 * SUBLANES, cols), f32),
            pltpu.VMEM((1, cols), f32),
        ],
        compiler_params=_cparams(("arbitrary", "arbitrary", "arbitrary")),
        name="moba_prompt",
    )(proj, kb, vt, kmean_g, q_gain_col, slopes2)


def _head_expand():
    r = lax.broadcasted_iota(jnp.int32, (LANES, D_INNER), 0)
    c = lax.broadcasted_iota(jnp.int32, (LANES, D_INNER), 1) // SSM_HEAD_DIM
    return jnp.where(r == c, 1.0, 0.0).astype(bf16)


def _ssd_conv_act(ext_ref, rows, cw_ref, cb_ref, base):
    conv = cb_ref[...] + cw_ref[0:1, :] * ext_ref[pl.ds(base, rows), :]
    for i in range(1, CONV_WIDTH):
        conv = conv + cw_ref[i:i + 1, :] * ext_ref[pl.ds(base + i, rows), :]
    return _silu(conv)


def _ssd_chunk(act, dt_raw, allowed, dtb_ref, alog_ref):
    rows = act.shape[0]
    xs = act[:, :D_INNER]
    gn = SSM_GROUPS * D_STATE
    bm = act[:, D_INNER:D_INNER + gn]
    cm = act[:, D_INNER + gn:]
    dt = _softplus(dt_raw + dtb_ref[...])
    a = -jnp.exp(alog_ref[...])
    mask_b = jnp.where(allowed, 1.0, 0.0).astype(bf16)
    cs = _dot_exact_lhs(mask_b, dt * a)
    expand = _head_expand()
    dt_e = _dot_exact_rhs(dt, expand)
    cs_e = _dot_exact_rhs(cs, expand)
    xdt = xs * dt_e
    csT = cs.T
    xdt_b = xdt.astype(bf16)
    ypairs = []
    lane = lax.broadcasted_iota(jnp.int32, (rows, LANES), 1)
    for g in range(SSM_GROUPS):
        cb = _dot_nt(cm[:, g * D_STATE:(g + 1) * D_STATE].astype(bf16), bm[:, g * D_STATE:(g + 1) * D_STATE].astype(bf16))
        hpg = SSM_HEADS // SSM_GROUPS
        for pair in range(hpg // 2):
            res = []
            for k in range(2):
                h = g * hpg + pair * 2 + k
                seg = cs[:, h:h + 1] - csT[h:h + 1, :]
                m = (cb * jnp.exp(jnp.where(allowed, seg, NEG))).astype(bf16)
                lo = (g * hpg + pair * 2) * SSM_HEAD_DIM
                res.append(_dot(m, xdt_b[:, lo:lo + LANES]))
            ypairs.append(jnp.where(lane < SSM_HEAD_DIM, res[0], res[1]))
    y_diag = jnp.concatenate(ypairs, axis=1)
    return xs, bm, cm, dt_e, cs_e, xdt, y_diag


def _ssd_finish(y, xs, z, dexp_ref, norm_ref):
    y = (y + dexp_ref[...] * xs) * _silu(z)
    gw = D_INNER // SSM_GROUPS
    outs = []
    for g in range(SSM_GROUPS):
        yg = y[:, g * gw:(g + 1) * gw]
        ms = jnp.mean(yg * yg, axis=-1, keepdims=True)
        outs.append(yg * lax.rsqrt(ms + EPS) * norm_ref[:, g * gw:(g + 1) * gw])
    return jnp.concatenate(outs, axis=1)


def _ssd_prompt_kernel(xbc_ref, z_ref, dt_ref, cw_ref, cb_ref, dtb_ref, alog_ref, dexp_ref, norm_ref,
                       y_ref, h_ref, ext_ref, ht_ref, *, rows):
    c = pl.program_id(1)

    @pl.when(c == 0)
    def _():
        ext_ref[pl.ds(0, SUBLANES), :] = jnp.zeros((SUBLANES, CONV_DIM), f32)
        ht_ref[...] = jnp.zeros(ht_ref.shape, f32)

    xt = xbc_ref[...]
    ext_ref[pl.ds(SUBLANES, rows), :] = xt
    act = _ssd_conv_act(ext_ref, rows, cw_ref, cb_ref, base=SUBLANES - (CONV_WIDTH - 1))
    ext_ref[pl.ds(0, SUBLANES), :] = xt[rows - SUBLANES:rows, :]

    ri = lax.broadcasted_iota(jnp.int32, (rows, rows), 0)
    ci = lax.broadcasted_iota(jnp.int32, (rows, rows), 1)
    xs, bm, cm, dt_e, cs_e, xdt, y = _ssd_chunk(act, dt_ref[...], ci <= ri, dtb_ref, alog_ref)

    cs_last = cs_e[rows - 1:rows, :]
    ecs = jnp.exp(cs_e)
    xdte = (xdt * jnp.exp(cs_last - cs_e)).astype(bf16)
    dec = jnp.exp(cs_last)
    hw = D_INNER // SSM_GROUPS
    yoff = []
    for g in range(SSM_GROUPS):
        ht_g = ht_ref[:, g * hw:(g + 1) * hw]
        yoff.append(_dot(cm[:, g * D_STATE:(g + 1) * D_STATE].astype(bf16), ht_g.astype(bf16)))
        upd = _dot_tn(bm[:, g * D_STATE:(g + 1) * D_STATE].astype(bf16), xdte[:, g * hw:(g + 1) * hw])
        ht_ref[:, g * hw:(g + 1) * hw] = ht_g * dec[:, g * hw:(g + 1) * hw] + upd
    y = y + jnp.concatenate(yoff, axis=1) * ecs
    y_ref[...] = _ssd_finish(y, xs, z_ref[...], dexp_ref, norm_ref)

    @pl.when(c == pl.num_programs(1) - 1)
    def _():
        h_ref[0] = ht_ref[...].T.reshape(SSM_HEADS, SSM_HEAD_DIM, D_STATE)


def ssd_prompt(proj, conv_w, conv_b_row, dtb_row, alog_row, dexp_row, norm_row, batch, seq, rows):
    nc = seq // rows
    return pl.pallas_call(
        functools.partial(_ssd_prompt_kernel, rows=rows),
        grid=(batch, nc),
        in_specs=[
            pl.BlockSpec((rows, CONV_DIM), lambda b, c: (b * nc + c, OFF_XBC // CONV_DIM)),
            pl.BlockSpec((rows, D_INNER), lambda b, c: (b * nc + c, OFF_Z // D_INNER)),
            pl.BlockSpec((rows, LANES), lambda b, c: (b * nc + c, OFF_DT // LANES)),
            pl.BlockSpec((CONV_WIDTH, CONV_DIM), lambda b, c: (0, 0)),
            pl.BlockSpec((1, CONV_DIM), lambda b, c: (0, 0)),
            pl.BlockSpec((1, LANES), lambda b, c: (0, 0)),
            pl.BlockSpec((1, LANES), lambda b, c: (0, 0)),
            pl.BlockSpec((1, D_INNER), lambda b, c: (0, 0)),
            pl.BlockSpec((1, D_INNER), lambda b, c: (0, 0)),
        ],
        out_specs=[
            pl.BlockSpec((rows, D_INNER), lambda b, c: (b * nc + c, 0)),
            pl.BlockSpec((1, SSM_HEADS, SSM_HEAD_DIM, D_STATE), lambda b, c: (b, 0, 0, 0)),
        ],
        out_shape=[
            jax.ShapeDtypeStruct((batch * seq, D_INNER), f32),
            jax.ShapeDtypeStruct((batch, SSM_HEADS, SSM_HEAD_DIM, D_STATE), f32),
        ],
        scratch_shapes=[
            pltpu.VMEM((rows + SUBLANES, CONV_DIM), f32),
            pltpu.VMEM((D_STATE, D_INNER), f32),
        ],
        compiler_params=_cparams(("arbitrary", "arbitrary")),
        name="ssd_prompt",
    )(proj, proj, proj, conv_w, conv_b_row, dtb_row, alog_row, dexp_row, norm_row)


def _mem_attend_kernel(q_ref, mk_ref, mv_ref, g_ref, o_ref):
    outs = []
    for h in range(MEM_HEADS):
        lo = h * MEM_HEAD_DIM
        q = q_ref[:, lo:lo + MEM_HEAD_DIM]
        ms = jnp.mean(q * q, axis=-1, keepdims=True)
        qn = (q * lax.rsqrt(ms + EPS) * g_ref[...] * (MEM_HEAD_DIM ** -0.5)).astype(bf16)
        s = _dot_nt(qn, mk_ref[0, :, lo:lo + MEM_HEAD_DIM].astype(bf16))
        m = jnp.max(s, axis=-1, keepdims=True)
        p = jnp.exp(s - m)
        l = jnp.sum(p, axis=-1, keepdims=True)
        outs.append(_dot(p.astype(bf16), mv_ref[0, :, lo:lo + MEM_HEAD_DIM].astype(bf16)) / l)
    o_ref[...] = jnp.concatenate(outs, axis=1)


def mem_attend(proj, mk, mv, mq_gain_row, nseq, rows_per_seq, tq):
    w = MEM_HEADS * MEM_HEAD_DIM
    nq = rows_per_seq // tq
    return pl.pallas_call(
        _mem_attend_kernel,
        grid=(nseq, nq),
        in_specs=[
            pl.BlockSpec((tq, w), lambda b, i: (b * nq + i, OFF_MQ // w)),
            pl.BlockSpec((1, MEM_TOKENS, w), lambda b, i: (b, 0, 0)),
            pl.BlockSpec((1, MEM_TOKENS, w), lambda b, i: (b, 0, 0)),
            pl.BlockSpec((1, MEM_HEAD_DIM), lambda b, i: (0, 0)),
        ],
        out_specs=pl.BlockSpec((tq, w), lambda b, i: (b * nq + i, 0)),
        out_shape=jax.ShapeDtypeStruct((nseq * rows_per_seq, w), f32),
        compiler_params=_cparams(("arbitrary", "arbitrary")),
        name="mem_attend",
    )(proj, mk, mv, mq_gain_row)


def _merge_kernel(x_ref, a_ref, s_ref, m_ref, ga_ref, gs_ref, gm_ref, wa_ref, ws_ref, wm_ref, wo_ref, o_ref):
    mixed = _sigmoid(ga_ref[...]) * _dot(a_ref[...].astype(bf16), wa_ref[...])
    mixed = mixed + _sigmoid(gs_ref[...]) * _dot(s_ref[...].astype(bf16), ws_ref[...])
    mixed = mixed + _sigmoid(gm_ref[...]) * _dot(m_ref[...].astype(bf16), wm_ref[...])
    o_ref[...] = x_ref[...] + _dot(mixed.astype(bf16), wo_ref[...])


def merge(x, attn_o, ssm_o, mem_o, proj, wa, ws, wm, wo, tm):
    t, d = x.shape
    row = lambda i: (i, 0)
    const = lambda i: (0, 0)
    gate_spec = lambda k: pl.BlockSpec((tm, d), lambda i: (i, OFF_GATES // d + k))
    w_spec = pl.BlockSpec((d, d), const)
    return pl.pallas_call(
        _merge_kernel,
        grid=(t // tm,),
        in_specs=[pl.BlockSpec((tm, d), row)] * 4 + [gate_spec(0), gate_spec(1), gate_spec(2)] + [w_spec] * 4,
        out_specs=pl.BlockSpec((tm, d), row),
        out_shape=jax.ShapeDtypeStruct((t, d), f32),
        compiler_params=_cparams(("arbitrary",)),
        name="merge",
    )(x, attn_o, ssm_o, mem_o, proj, proj, proj, wa, ws, wm, wo)


def _ffn_kernel(x_ref, g_ref, wg_ref, wu_ref, wd_ref, o_ref):
    x = x_ref[...]
    ms = jnp.mean(x * x, axis=-1, keepdims=True)
    h = (x * lax.rsqrt(ms + EPS) * g_ref[...]).astype(bf16)
    act = _silu(_dot(h, wg_ref[...])) * _dot(h, wu_ref[...])
    o_ref[...] = x + _dot(act.astype(bf16), wd_ref[...])


def ffn(x, gain_row, wg, wu, wd, tm):
    t, d = x.shape
    dff = wg.shape[1]
    const = lambda i: (0, 0)
    single = pl.Buffered(1)
    return pl.pallas_call(
        _ffn_kernel,
        grid=(t // tm,),
        in_specs=[
            pl.BlockSpec((tm, d), lambda i: (i, 0)),
            pl.BlockSpec((1, d), const),
            pl.BlockSpec((d, dff), const, pipeline_mode=single),
            pl.BlockSpec((d, dff), const, pipeline_mode=single),
            pl.BlockSpec((dff, d), const, pipeline_mode=single),
        ],
        out_specs=pl.BlockSpec((tm, d), lambda i: (i, 0)),
        out_shape=jax.ShapeDtypeStruct((t, d), f32),
        compiler_params=_cparams(("arbitrary",)),
        name="ffn",
    )(x, gain_row, wg, wu, wd)


def _top3_cols(gate, jb):
    sel = jnp.zeros(gate.shape, jnp.bool_)
    for _ in range(MOBA_TOPK):
        m = jnp.max(gate, axis=-1, keepdims=True)
        idx = jnp.min(jnp.where(gate == m, jb, gate.shape[1]), axis=-1, keepdims=True)
        hit = jb == idx
        sel = jnp.logical_or(sel, jnp.logical_and(hit, m > -jnp.inf))
        gate = jnp.where(hit, -jnp.inf, gate)
    return sel


def _moba_sample_kernel(pt_ref, q_ref, kn_ref, vn_ref, qg_ref, sl_ref, ck_hbm, cv_hbm, o_ref,
                        kbuf, vbuf, s_ref, sem, *, n_pages):
    b = pl.program_id(0)
    nseq = pl.num_programs(0)
    kvw = N_KV_HEADS * HEAD_DIM
    nrow = Q_PER_KV * N_KV_HEADS * SAMPLE_ROWS
    nblk = n_pages * PAGE_SIZE // MOBA_BLOCK
    ppb = MOBA_BLOCK // PAGE_SIZE
    past = n_pages * PAGE_SIZE
    slot = b % 2

    def k_copy(seq, sl, p):
        return pltpu.make_async_copy(ck_hbm.at[pt_ref[seq, p]], kbuf.at[sl, p], sem.at[0, sl])

    def v_copy(seq, sl, p):
        return pltpu.make_async_copy(cv_hbm.at[pt_ref[seq, p]], vbuf.at[sl, p], sem.at[1, sl])

    def start_all(seq, sl):
        for p in range(n_pages):
            k_copy(seq, sl, p).start()
        for p in range(n_pages):
            v_copy(seq, sl, p).start()

    @pl.when(b == 0)
    def _():
        start_all(0, 0)

    @pl.when(b + 1 < nseq)
    def _():
        start_all(b + 1, 1 - slot)

    qb = q_ref[...]
    lane_g = lax.broadcasted_iota(jnp.int32, (SAMPLE_ROWS, kvw), 1) // HEAD_DIM
    pieces = []
    for r in range(Q_PER_KV):
        slab = qb[:, r * kvw:(r + 1) * kvw]
        for g in range(N_KV_HEADS):
            pieces.append(jnp.where(lane_g == g, slab, 0.0))
    qbd = jnp.concatenate(pieces, axis=0)
    ms = jnp.sum(qbd * qbd, axis=-1, keepdims=True) * (1.0 / HEAD_DIM)
    qn = qbd * lax.rsqrt(ms + EPS) * qg_ref[...]
    qs = (qn * (HEAD_DIM ** -0.5 * LOG2E)).astype(bf16)

    for p in range(n_pages):
        k_copy(b, slot, p).wait()

    def k_page(p):
        return kbuf[slot, p].reshape(kvw, PAGE_SIZE)

    lane = lax.broadcasted_iota(jnp.int32, (kvw, LANES), 1)
    kmean_t = jnp.zeros((kvw, LANES), f32)
    for j in range(nblk):
        blk = k_page(j * ppb)
        for pp in range(1, ppb):
            blk = blk + k_page(j * ppb + pp)
        kmean_t = kmean_t + jnp.where(lane == j, jnp.sum(blk, axis=-1, keepdims=True) * (1.0 / MOBA_BLOCK), 0.0)
    kh, kmid, kl = _split3(kmean_t)
    qh, qmid, ql = _split3(qn)
    gate = _dot(qh, kh) + _dot(qh, kmid) + _dot(qmid, kh) + _dot(qh, kl) + _dot(ql, kh) + _dot(qmid, kmid)
    jb = lax.broadcasted_iota(jnp.int32, (nrow, LANES), 1)
    gate = jnp.where(jb < nblk, gate, -jnp.inf)
    pen = jnp.where(_top3_cols(gate, jb), 0.0, NEG)

    slope = sl_ref[...]
    kk = lax.broadcasted_iota(jnp.int32, (nrow, PAGE_SIZE), 1)
    for p in range(n_pages):
        j = p // ppb
        rel = (kk + (p * PAGE_SIZE - past)).astype(f32)
        s_ref[:, p * PAGE_SIZE:(p + 1) * PAGE_SIZE] = _dot(qs, k_page(p).astype(bf16)) + slope * rel + pen[:, j:j + 1]

    zpad = jnp.zeros((LANES - SAMPLE_ROWS, kvw), f32)
    knew = jnp.concatenate([kn_ref[...], zpad], axis=0).astype(bf16)
    vnew = jnp.concatenate([vn_ref[...], zpad], axis=0).astype(bf16)
    kn_lane = lax.broadcasted_iota(jnp.int32, (nrow, LANES), 1)
    tok = lax.broadcasted_iota(jnp.int32, (nrow, LANES), 0) % SAMPLE_ROWS
    s_own = _dot_nt(qs, knew) + slope * kn_lane.astype(f32)
    s_ref[:, past:past + LANES] = jnp.where(kn_lane <= tok, s_own, NEG)

    m = jnp.max(s_ref[...], axis=-1, keepdims=True)
    p_own = jnp.exp2(s_ref[:, past:past + LANES] - m)
    l = jnp.sum(p_own, axis=-1, keepdims=True)
    acc = _dot(p_own.astype(bf16), vnew)

    for p in range(n_pages):
        v_copy(b, slot, p).wait()

    for p in range(n_pages):
        pp = jnp.exp2(s_ref[:, p * PAGE_SIZE:(p + 1) * PAGE_SIZE] - m)
        l = l + jnp.sum(pp, axis=-1, keepdims=True)
        vp = vbuf[slot, p].reshape(kvw, PAGE_SIZE).astype(bf16)
        acc = acc + _dot_nt(pp.astype(bf16), vp)
    o = acc / l

    slabs = []
    for r in range(Q_PER_KV):
        slab = jnp.zeros((SAMPLE_ROWS, kvw), f32)
        for g in range(N_KV_HEADS):
            lo = (r * N_KV_HEADS + g) * SAMPLE_ROWS
            slab = slab + jnp.where(lane_g == g, o[lo:lo + SAMPLE_ROWS, :], 0.0)
        slabs.append(slab)
    o_ref[...] = jnp.concatenate(slabs, axis=1)


def moba_sample(page_table, proj, kn_new, ck, cv, q_gain_row, slope_rows, nseq):
    kvw = N_KV_HEADS * HEAD_DIM
    qw = N_HEADS * HEAD_DIM
    n_pages = page_table.shape[1]
    nrow = Q_PER_KV * N_KV_HEADS * SAMPLE_ROWS
    past = n_pages * PAGE_SIZE
    grid_spec = pltpu.PrefetchScalarGridSpec(
        num_scalar_prefetch=1,
        grid=(nseq,),
        in_specs=[
            pl.BlockSpec((SAMPLE_ROWS, qw), lambda b, pt: (b, OFF_Q // qw)),
            pl.BlockSpec((SAMPLE_ROWS, kvw), lambda b, pt: (b, 0)),
            pl.BlockSpec((SAMPLE_ROWS, kvw), lambda b, pt: (b, OFF_V // kvw)),
            pl.BlockSpec((1, kvw), lambda b, pt: (0, 0)),
            pl.BlockSpec((nrow, LANES), lambda b, pt: (0, 0)),
            pl.BlockSpec(memory_space=pl.ANY),
            pl.BlockSpec(memory_space=pl.ANY),
        ],
        out_specs=pl.BlockSpec((SAMPLE_ROWS, qw), lambda b, pt: (b, 0)),
        scratch_shapes=[
            pltpu.VMEM((2, n_pages, N_KV_HEADS, HEAD_DIM, PAGE_SIZE), f32),
            pltpu.VMEM((2, n_pages, N_KV_HEADS, HEAD_DIM, PAGE_SIZE), f32),
            pltpu.VMEM((nrow, past + LANES), f32),
            pltpu.SemaphoreType.DMA((2, 2)),
        ],
    )
    return pl.pallas_call(
        functools.partial(_moba_sample_kernel, n_pages=n_pages),
        grid_spec=grid_spec,
        out_shape=jax.ShapeDtypeStruct((nseq * SAMPLE_ROWS, qw), f32),
        compiler_params=_cparams(("arbitrary",)),
        name="moba_sample",
    )(page_table, proj, kn_new, proj, q_gain_row, slope_rows, ck, cv)


def _ssd_sample_kernel(ext_ref, z_ref, dt_ref, h0_ref, cw_ref, cb_ref, dtb_ref, alog_ref, dexp_ref, norm_ref,
                       y_ref, h_ref, scr_ref, *, nseq_step, n_new):
    rows = nseq_step * SAMPLE_ROWS
    scr_ref[pl.ds(0, rows), :] = ext_ref[...]
    scr_ref[pl.ds(rows, SUBLANES), :] = jnp.zeros((SUBLANES, CONV_DIM), f32)
    act = _ssd_conv_act(scr_ref, rows, cw_ref, cb_ref, base=0)

    ri = lax.broadcasted_iota(jnp.int32, (rows, rows), 0)
    ci = lax.broadcasted_iota(jnp.int32, (rows, rows), 1)
    allowed = jnp.logical_and(ri // SAMPLE_ROWS == ci // SAMPLE_ROWS,
                              jnp.logical_and(ci % SAMPLE_ROWS <= ri % SAMPLE_ROWS, ci % SAMPLE_ROWS < n_new))
    xs, bm, cm, dt_e, cs_e, xdt, y = _ssd_chunk(act, dt_ref[...], allowed, dtb_ref, alog_ref)

    valid = lax.broadcasted_iota(jnp.int32, (SAMPLE_ROWS, 1), 0) < n_new
    ecs = jnp.exp(cs_e)
    hw = D_INNER // SSM_GROUPS
    hpg = SSM_HEADS // SSM_GROUPS
    yoff_rows = []
    for i in range(nseq_step):
        lo = i * SAMPLE_ROWS
        cs_i = cs_e[lo:lo + SAMPLE_ROWS, :]
        cs_last = cs_e[lo + n_new - 1:lo + n_new, :]
        xdte = jnp.where(valid, xdt[lo:lo + SAMPLE_ROWS, :] * jnp.exp(cs_last - cs_i), 0.0)
        yg = []
        upd = []
        for g in range(SSM_GROUPS):
            h0g = h0_ref[i, g * hpg:(g + 1) * hpg].reshape(hw, D_STATE)
            yg.append(_dot_nt(cm[lo:lo + SAMPLE_ROWS, g * D_STATE:(g + 1) * D_STATE], h0g))
            upd.append(_dot_tn(xdte[:, g * hw:(g + 1) * hw], bm[lo:lo + SAMPLE_ROWS, g * D_STATE:(g + 1) * D_STATE]))
        yoff_rows.append(jnp.concatenate(yg, axis=1))
        dec_col = jnp.broadcast_to(jnp.exp(cs_last), (SUBLANES, D_INNER)).T[:, 0:1]
        h_new = h0_ref[i].reshape(D_INNER, D_STATE) * dec_col + jnp.concatenate(upd, axis=0)
        h_ref[i] = h_new.reshape(SSM_HEADS, SSM_HEAD_DIM, D_STATE)
    y = y + jnp.concatenate(yoff_rows, axis=0) * ecs
    y_ref[...] = _ssd_finish(y, xs, z_ref[...], dexp_ref, norm_ref)


def ssd_sample(ext, proj, h0, conv_w, conv_b_row, dtb_row, alog_row, dexp_row, norm_row, nseq, nseq_step, n_new):
    rows = nseq_step * SAMPLE_ROWS
    const = lambda i: (0, 0)
    state_spec = pl.BlockSpec((nseq_step, SSM_HEADS, SSM_HEAD_DIM, D_STATE), lambda i: (i, 0, 0, 0))
    return pl.pallas_call(
        functools.partial(_ssd_sample_kernel, nseq_step=nseq_step, n_new=n_new),
        grid=(nseq // nseq_step,),
        in_specs=[
            pl.BlockSpec((rows, CONV_DIM), lambda i: (i, 0)),
            pl.BlockSpec((rows, D_INNER), lambda i: (i, OFF_Z // D_INNER)),
            pl.BlockSpec((rows, LANES), lambda i: (i, OFF_DT // LANES)),
            state_spec,
            pl.BlockSpec((CONV_WIDTH, CONV_DIM), const),
            pl.BlockSpec((1, CONV_DIM), const),
            pl.BlockSpec((1, LANES), const),
            pl.BlockSpec((1, LANES), const),
            pl.BlockSpec((1, D_INNER), const),
            pl.BlockSpec((1, D_INNER), const),
        ],
        out_specs=[pl.BlockSpec((rows, D_INNER), lambda i: (i, 0)), state_spec],
        out_shape=[
            jax.ShapeDtypeStruct((nseq * SAMPLE_ROWS, D_INNER), f32),
            jax.ShapeDtypeStruct(h0.shape, f32),
        ],
        scratch_shapes=[pltpu.VMEM((rows + SUBLANES, CONV_DIM), f32)],
        compiler_params=_cparams(("arbitrary",)),
        name="ssd_sample",
    )(ext, proj, proj, h0, conv_w, conv_b_row, dtb_row, alog_row, dexp_row, norm_row)


IN_SIZES = (N_HEADS * HEAD_DIM, N_KV_HEADS * HEAD_DIM, N_KV_HEADS * HEAD_DIM, D_INNER, CONV_DIM, SSM_HEADS,
            MEM_HEADS * MEM_HEAD_DIM, 3 * D_MODEL)
TM = 512
TN_PROJ = 2944
TQ = 128
SSD_ROWS = 256
SSD_SAMPLE_SEQS = 8


def _pad_lanes(v):
    return jnp.pad(v, (0, LANES - v.shape[0])).reshape(1, LANES)


def kernel(x_prompt, x_sample, mem_prompt, cache_k, cache_v, page_table, state_conv, state_ssm, cache_mem_k, cache_mem_v, norm_mix, w_in, q_norm, k_norm, conv_w, conv_b, dt_bias, a_log, d_skip, ssm_norm, mem_norm, w_mem_kv, mq_norm, mk_norm, w_attn_br, w_ssm_br, w_mem_br, w_out, norm_ffn, w_gate, w_up, w_down):
    assert w_in.shape[0] == 1, "single layer"
    batch, seq, d = x_prompt.shape
    nseq, n_new, _ = x_sample.shape
    kvw = N_KV_HEADS * HEAD_DIM
    qw = N_HEADS * HEAD_DIM

    offs = np.cumsum(IN_SIZES)[:-1].tolist()
    wq, wk, wv, wz, wxbc, wdt, wmq, wgates = jnp.split(w_in[0], offs, axis=1)
    wdt = jnp.pad(wdt, ((0, 0), (0, LANES - SSM_HEADS)))
    tail = [wz, wxbc, wmq, wgates, wk, wv, wdt]
    w_prompt = jnp.concatenate([wq] + tail, axis=1).astype(bf16)
    wq_rgd = wq.reshape(d, N_KV_HEADS, Q_PER_KV, HEAD_DIM).transpose(0, 2, 1, 3).reshape(d, qw)
    w_sample = jnp.concatenate([wq_rgd] + tail, axis=1).astype(bf16)
    wa = w_attn_br[0].astype(bf16)
    wa_rgd = w_attn_br[0].reshape(N_KV_HEADS, Q_PER_KV, HEAD_DIM, d).transpose(1, 0, 2, 3).reshape(qw, d).astype(bf16)
    ws, wm, wo = w_ssm_br[0].astype(bf16), w_mem_br[0].astype(bf16), w_out[0].astype(bf16)
    wg, wu, wd = w_gate[0].astype(bf16), w_up[0].astype(bf16), w_down[0].astype(bf16)

    k_gain = jnp.tile(k_norm[0], N_KV_HEADS).reshape(1, kvw)
    conv_b_row = conv_b[0].reshape(1, CONV_DIM)
    dtb_row, alog_row = _pad_lanes(dt_bias[0]), _pad_lanes(a_log[0])
    dexp_row = jnp.repeat(d_skip[0], SSM_HEAD_DIM).reshape(1, D_INNER)
    ssm_norm_row = ssm_norm[0].reshape(1, D_INNER)
    mq_gain = mq_norm[0].reshape(1, MEM_HEAD_DIM)
    ffn_gain = norm_ffn[0].reshape(1, d)
    slopes2 = jnp.exp2(-8.0 * jnp.arange(1, N_HEADS + 1, dtype=f32) / N_HEADS) * LOG2E

    xp = x_prompt.reshape(batch * seq, d)
    proj = inproj(xp, norm_mix[0], w_prompt, TM, TN_PROJ)
    kn, kb, vt, kmean = kprep(proj, k_gain, batch, seq, 2048)
    kmean_g = kmean.reshape(batch, seq // MOBA_BLOCK, N_KV_HEADS, HEAD_DIM).transpose(0, 2, 1, 3)
    slope_cols = jnp.repeat(slopes2.reshape(N_KV_HEADS, Q_PER_KV), TQ, axis=1).reshape(N_KV_HEADS, 1, Q_PER_KV * TQ)
    q_gain_col = jnp.broadcast_to(q_norm[0][:, None], (HEAD_DIM, TQ))
    attn_o = moba_prompt(proj, kb, vt, kmean_g, q_gain_col, slope_cols, batch, seq, TQ)
    ssm_o, h_prompt = ssd_prompt(proj, conv_w[0], conv_b_row, dtb_row, alog_row, dexp_row, ssm_norm_row,
                                 batch, seq, SSD_ROWS)
    mem_kv = inproj(mem_prompt.reshape(batch * MEM_TOKENS, d), mem_norm[0], w_mem_kv[0].astype(bf16), TM, qw)
    mk = headnorm(mem_kv, 0, qw, jnp.tile(mk_norm[0], MEM_HEADS).reshape(1, qw), MEM_HEAD_DIM, TM)
    mv = mem_kv[:, qw:]
    mem_o = mem_attend(proj, mk.reshape(batch, MEM_TOKENS, qw), mv.reshape(batch, MEM_TOKENS, qw), mq_gain,
                       batch, seq, TM)
    x1 = merge(xp, attn_o, ssm_o, mem_o, proj, wa, ws, wm, wo, TM)
    y_prompt = ffn(x1, ffn_gain, wg, wu, wd, TM).reshape(batch, seq, d)

    k_prompt = kn.reshape(1, batch, seq, N_KV_HEADS, HEAD_DIM)
    v_prompt = proj[:, OFF_V:OFF_V + kvw].reshape(1, batch, seq, N_KV_HEADS, HEAD_DIM)
    conv_prompt = proj[:, OFF_XBC:OFF_XBC + CONV_DIM].reshape(batch, seq, CONV_DIM)[:, seq - (CONV_WIDTH - 1):][None]
    mem_k_prompt = mk.reshape(1, batch, MEM_TOKENS, MEM_HEADS, MEM_HEAD_DIM)
    mem_v_prompt = mv.reshape(1, batch, MEM_TOKENS, MEM_HEADS, MEM_HEAD_DIM)

    xs = jnp.pad(x_sample, ((0, 0), (0, SAMPLE_ROWS - n_new), (0, 0))).reshape(nseq * SAMPLE_ROWS, d)
    proj_s = inproj(xs, norm_mix[0], w_sample, TM, TN_PROJ)
    kn_s = headnorm(proj_s, OFF_K // kvw, kvw, k_gain, HEAD_DIM, TM)
    slope_rows = jnp.broadcast_to(
        jnp.repeat(slopes2.reshape(N_KV_HEADS, Q_PER_KV).T.reshape(-1), SAMPLE_ROWS)[:, None], (N_HEADS * SAMPLE_ROWS, LANES))
    ck_t = jnp.transpose(cache_k[0], (0, 2, 3, 1))
    cv_t = jnp.transpose(cache_v[0], (0, 2, 3, 1))
    attn_s = moba_sample(page_table, proj_s, kn_s, ck_t, cv_t, jnp.tile(q_norm[0], N_KV_HEADS).reshape(1, kvw),
                         slope_rows, nseq)
    xbc_new = proj_s[:, OFF_XBC:OFF_XBC + CONV_DIM].reshape(nseq, SAMPLE_ROWS, CONV_DIM)[:, :n_new]
    ext = jnp.concatenate(
        [state_conv[0], xbc_new, jnp.zeros((nseq, SAMPLE_ROWS - (CONV_WIDTH - 1) - n_new, CONV_DIM), f32)], axis=1)
    ssm_s, h_sample = ssd_sample(ext.reshape(nseq * SAMPLE_ROWS, CONV_DIM), proj_s, state_ssm[0], conv_w[0], conv_b_row,
                                 dtb_row, alog_row, dexp_row, ssm_norm_row, nseq, SSD_SAMPLE_SEQS, n_new)
    mem_s = mem_attend(proj_s, cache_mem_k[0].reshape(nseq, MEM_TOKENS, qw), cache_mem_v[0].reshape(nseq, MEM_TOKENS, qw),
                       mq_gain, nseq, SAMPLE_ROWS, SAMPLE_ROWS)
    x1s = merge(xs, attn_s, ssm_s, mem_s, proj_s, wa_rgd, ws, wm, wo, TM)
    y_sample = ffn(x1s, ffn_gain, wg, wu, wd, TM).reshape(nseq, SAMPLE_ROWS, d)[:, :n_new]

    k_sample = kn_s.reshape(nseq, SAMPLE_ROWS, N_KV_HEADS, HEAD_DIM)[:, :n_new][None]
    v_sample = proj_s[:, OFF_V:OFF_V + kvw].reshape(nseq, SAMPLE_ROWS, N_KV_HEADS, HEAD_DIM)[:, :n_new][None]
    conv_sample = ext[:, n_new:n_new + CONV_WIDTH - 1][None]

    return (y_prompt, y_sample, k_prompt, v_prompt, conv_prompt, h_prompt[None], mem_k_prompt, mem_v_prompt,
            k_sample, v_sample, conv_sample, h_sample[None])
```

```python
import functools
import math

import jax
import jax.numpy as jnp
import numpy as np
from jax import lax
from jax.experimental import pallas as pl
from jax.experimental.pallas import tpu as pltpu

f32 = jnp.float32
bf16 = jnp.bfloat16

D_MODEL = 1024
N_HEADS = 16
N_KV_HEADS = 4
HEAD_DIM = 64
Q_PER_KV = N_HEADS // N_KV_HEADS
MOBA_BLOCK = 256
MOBA_TOPK = 3
SSM_HEADS = 16
SSM_HEAD_DIM = 64
D_INNER = SSM_HEADS * SSM_HEAD_DIM
SSM_GROUPS = 4
D_STATE = 128
CONV_WIDTH = 4
CONV_DIM = D_INNER + 2 * SSM_GROUPS * D_STATE
MEM_TOKENS = 256
MEM_HEADS = 4
MEM_HEAD_DIM = 256
D_FF = 2816
EPS = 1e-6
PAGE_SIZE = 128

LANES = 128
SUBLANES = 8
LOG2E = 1.4426950408889634
NEG = -1e30
VMEM_LIMIT = 56 * 1024 * 1024

OFF_Q, OFF_Z, OFF_XBC, OFF_MQ, OFF_GATES, OFF_K, OFF_V, OFF_DT = 0, 1024, 2048, 4096, 5120, 8192, 8448, 8704
N_PROJ = 8832
SAMPLE_ROWS = 8
MOBA_GROUP = 4


def _cparams(sem):
    return pltpu.CompilerParams(dimension_semantics=sem, vmem_limit_bytes=VMEM_LIMIT)


def _split3(x):
    h = x.astype(bf16)
    r = x - h.astype(f32)
    m = r.astype(bf16)
    l = (r - m.astype(f32)).astype(bf16)
    return h, m, l


def _dot(a, b):
    return jnp.dot(a, b, preferred_element_type=f32)


def _dot_nt(a, b):
    return lax.dot_general(a, b, (((1,), (1,)), ((), ())), preferred_element_type=f32)


def _dot_tn(a, b):
    return lax.dot_general(a, b, (((0,), (0,)), ((), ())), preferred_element_type=f32)


def _dot_exact_rhs(x, m_bf16):
    h, m, l = _split3(x)
    return _dot(h, m_bf16) + _dot(m, m_bf16) + _dot(l, m_bf16)


def _dot_exact_lhs(m_bf16, x):
    h, m, l = _split3(x)
    return _dot(m_bf16, h) + _dot(m_bf16, m) + _dot(m_bf16, l)


def _silu(x):
    return x * (1.0 / (1.0 + jnp.exp(-x)))


def _sigmoid(x):
    return 1.0 / (1.0 + jnp.exp(-x))


def _softplus(x):
    return jnp.maximum(x, 0.0) + jnp.log(1.0 + jnp.exp(-jnp.abs(x)))


def _inproj_kernel(x_ref, g_ref, w_ref, o_ref):
    x = x_ref[...]
    ms = jnp.mean(x * x, axis=-1, keepdims=True)
    xn = (x * lax.rsqrt(ms + EPS) * g_ref[...]).astype(bf16)
    o_ref[...] = _dot(xn, w_ref[...])


def inproj(x, gain, w, tm, tn):
    t, d = x.shape
    n = w.shape[1]
    return pl.pallas_call(
        _inproj_kernel,
        grid=(n // tn, t // tm),
        in_specs=[
            pl.BlockSpec((tm, d), lambda j, i: (i, 0)),
            pl.BlockSpec((1, d), lambda j, i: (0, 0)),
            pl.BlockSpec((d, tn), lambda j, i: (0, j)),
        ],
        out_specs=pl.BlockSpec((tm, tn), lambda j, i: (i, j)),
        out_shape=jax.ShapeDtypeStruct((t, n), f32),
        compiler_params=_cparams(("arbitrary", "arbitrary")),
        name="inproj",
    )(x, gain.reshape(1, d), w)


def _seg_ones(width, seg):
    r = lax.broadcasted_iota(jnp.int32, (width, width), 0) // seg
    c = lax.broadcasted_iota(jnp.int32, (width, width), 1) // seg
    return jnp.where(r == c, 1.0, 0.0).astype(bf16)


def _seg_rmsnorm(x, gain_row, seg):
    ss = _dot_exact_rhs(x * x, _seg_ones(x.shape[1], seg))
    return x * lax.rsqrt(ss * (1.0 / seg) + EPS) * gain_row


def _headnorm_kernel(x_ref, g_ref, o_ref, *, seg):
    o_ref[...] = _seg_rmsnorm(x_ref[...], g_ref[...], seg)


def headnorm(x, col_block, width, gain_row, seg, tm):
    t = x.shape[0]
    return pl.pallas_call(
        functools.partial(_headnorm_kernel, seg=seg),
        grid=(t // tm,),
        in_specs=[
            pl.BlockSpec((tm, width), lambda i: (i, col_block)),
            pl.BlockSpec((1, width), lambda i: (0, 0)),
        ],
        out_specs=pl.BlockSpec((tm, width), lambda i: (i, 0)),
        out_shape=jax.ShapeDtypeStruct((t, width), f32),
        compiler_params=_cparams(("arbitrary",)),
        name="headnorm",
    )(x, gain_row)


def _kprep_kernel(k_ref, v_ref, g_ref, kn_ref, ka_ref, vt_ref, km_ref):
    kn = _seg_rmsnorm(k_ref[...], g_ref[...], HEAD_DIM)
    kn_ref[...] = kn
    kn_b = kn.astype(bf16)
    rows = kn.shape[0]
    kvw = kn.shape[1]
    r = lax.broadcasted_iota(jnp.int32, (kvw, LANES), 0)
    c = lax.broadcasted_iota(jnp.int32, (kvw, LANES), 1)
    lane = lax.broadcasted_iota(jnp.int32, (rows, LANES), 1)
    pos = (lax.broadcasted_iota(jnp.int32, (rows, LANES), 0) % MOBA_BLOCK).astype(f32)
    pos_lanes = jnp.where(jnp.logical_and(lane >= HEAD_DIM, lane < HEAD_DIM + 3), pos, 0.0)
    for g in range(N_KV_HEADS):
        pick = jnp.where(jnp.logical_and(r == c + g * HEAD_DIM, c < HEAD_DIM), 1.0, 0.0).astype(bf16)
        ka_ref[0, g] = (_dot(kn_b, pick) + pos_lanes).astype(bf16)
    nb = kn.shape[0] // MOBA_BLOCK
    km_ref[0] = jnp.sum(kn.reshape(nb, MOBA_BLOCK, kn.shape[1]), axis=1) * (1.0 / MOBA_BLOCK)
    vt_ref[0] = v_ref[...].astype(bf16).T


def kprep(proj, k_gain_row, batch, seq, tm):
    kvw = N_KV_HEADS * HEAD_DIM
    nt = seq // tm
    nb = tm // MOBA_BLOCK
    return pl.pallas_call(
        _kprep_kernel,
        grid=(batch, nt),
        in_specs=[
            pl.BlockSpec((tm, kvw), lambda b, i: (b * nt + i, OFF_K // kvw)),
            pl.BlockSpec((tm, kvw), lambda b, i: (b * nt + i, OFF_V // kvw)),
            pl.BlockSpec((1, kvw), lambda b, i: (0, 0)),
        ],
        out_specs=[
            pl.BlockSpec((tm, kvw), lambda b, i: (b * nt + i, 0)),
            pl.BlockSpec((1, N_KV_HEADS, tm, LANES), lambda b, i: (b, 0, i, 0)),
            pl.BlockSpec((1, kvw, tm), lambda b, i: (b, 0, i)),
            pl.BlockSpec((1, nb, kvw), lambda b, i: (b, i, 0)),
        ],
        out_shape=[
            jax.ShapeDtypeStruct((batch * seq, kvw), f32),
            jax.ShapeDtypeStruct((batch, N_KV_HEADS, seq, LANES), bf16),
            jax.ShapeDtypeStruct((batch, kvw, seq), bf16),
            jax.ShapeDtypeStruct((batch, seq // MOBA_BLOCK, kvw), f32),
        ],
        compiler_params=_cparams(("arbitrary", "arbitrary")),
        name="kprep",
    )(proj, proj, k_gain_row)


def _top3_rows(gate, jb):
    sel = jnp.zeros(gate.shape, jnp.bool_)
    for _ in range(MOBA_TOPK):
        m = jnp.max(gate, axis=0, keepdims=True)
        idx = jnp.min(jnp.where(gate == m, jb, gate.shape[0]), axis=0, keepdims=True)
        hit = jb == idx
        sel = jnp.logical_or(sel, jnp.logical_and(hit, m > -jnp.inf))
        gate = jnp.where(hit, -jnp.inf, gate)
    return sel


def _moba_prompt_kernel(q_ref, kb_ref, vt_ref, km_ref, qg_ref, sl_ref, o_ref,
                        qbd_ref, pen_ref, acc_ref, m_ref, s_ref, cm_ref, *, tq):
    qt = pl.program_id(2)
    per_blk = MOBA_BLOCK // tq
    own = qt // per_blk
    cols = Q_PER_KV * tq
    nblk = km_ref.shape[2]

    qT = q_ref[...].T
    parts = []
    for r in range(Q_PER_KV):
        xr = qT[r * HEAD_DIM:(r + 1) * HEAD_DIM, :]
        ms = jnp.mean(xr * xr, axis=0, keepdims=True)
        parts.append(xr * lax.rsqrt(ms + EPS) * qg_ref[...])
    qn = jnp.concatenate(parts, axis=1)

    km = km_ref[0, 0]
    kh, kmid, kl = _split3(km)
    qh, qmid, ql = _split3(qn)
    gate = _dot(kh, qh) + _dot(kh, qmid) + _dot(kmid, qh) + _dot(kh, ql) + _dot(kl, qh) + _dot(kmid, qmid)
    jb = lax.broadcasted_iota(jnp.int32, (nblk, cols), 0)
    gate = jnp.where(jb < own, gate, -jnp.inf)
    sel = _top3_rows(gate, jb)
    pen_ref[...] = jnp.where(jnp.logical_or(sel, jb == own), 0.0, NEG)

    slope = sl_ref[0]
    s_hi, s_mid, s_lo = _split3(slope)
    rowi = lax.broadcasted_iota(jnp.int32, (HEAD_DIM, cols), 0)
    slope_rows = jnp.where(rowi == 0, s_hi.astype(f32), jnp.where(rowi == 1, s_mid.astype(f32),
                           jnp.where(rowi == 2, s_lo.astype(f32), 0.0)))
    qbd_ref[...] = jnp.concatenate([qn * (HEAD_DIM ** -0.5 * LOG2E), slope_rows], axis=0).astype(bf16)

    kk = lax.broadcasted_iota(jnp.int32, (MOBA_BLOCK, cols), 0)
    ones = jnp.ones((2 * SUBLANES, MOBA_BLOCK), bf16)
    qin = (qt % per_blk) * tq + lax.broadcasted_iota(jnp.int32, (MOBA_BLOCK, cols), 1) % tq

    def blk(j):
        return pl.ds(pl.multiple_of(j * MOBA_BLOCK, MOBA_BLOCK), MOBA_BLOCK)

    def stage_scores(j, buf, mask=None):
        s = _dot(kb_ref[0, 0, blk(j), :], qbd_ref[...])
        if mask is not None:
            s = jnp.where(mask, s, NEG)
        s_ref[buf] = s
        cm_ref[buf] = jnp.max(s, axis=0, keepdims=True)

    def stage_accum(j, buf, valid=None, first=False):
        shift = slope * ((j - own) * MOBA_BLOCK).astype(f32) + pen_ref[pl.ds(j, 1), :]
        if valid is not None:
            shift = shift + jnp.where(valid, 0.0, NEG)
        mj = cm_ref[buf] + shift
        m_old = None if first else m_ref[...]
        m_new = mj if first else jnp.maximum(m_old, mj)
        p = jnp.exp2(s_ref[buf] - (m_new - shift)).astype(bf16)
        v_aug = jnp.concatenate([vt_ref[0, :, blk(j)], ones], axis=0)
        pv = _dot(v_aug, p)
        acc_ref[...] = pv if first else acc_ref[...] * jnp.exp2(m_old - m_new) + pv
        m_ref[...] = m_new

    last = jnp.maximum(own - 1, 0)
    stage_scores(own, 0, kk <= qin)
    stage_scores(0, 1)
    stage_accum(own, 0, first=True)

    def body(i, c):
        j0, j1, j2 = 2 * i, 2 * i + 1, 2 * i + 2
        stage_scores(jnp.minimum(j1, last), 0)
        stage_accum(j0, 1)
        stage_scores(jnp.minimum(j2, last), 1)
        stage_accum(jnp.minimum(j1, last), 0, valid=j1 < own)
        return c

    lax.fori_loop(0, (own + 1) // 2, body, 0)

    acc = acc_ref[...]
    o = acc[0:HEAD_DIM, :] / acc[HEAD_DIM:HEAD_DIM + 1, :]
    oT = jnp.concatenate([o[:, r * tq:(r + 1) * tq] for r in range(Q_PER_KV)], axis=0)
    o_ref[...] = oT.T


def moba_prompt(proj, kb, vt, kmean_g, q_gain_col, slopes2, batch, seq, tq):
    nq = seq // tq
    qw = Q_PER_KV * HEAD_DIM
    kvw = N_KV_HEADS * HEAD_DIM
    nblk = seq // MOBA_BLOCK
    cols = Q_PER_KV * tq
    return pl.pallas_call(
        functools.partial(_moba_prompt_kernel, tq=tq),
        grid=(batch, N_KV_HEADS, nq),
        in_specs=[
            pl.BlockSpec((tq, qw), lambda b, g, i: (b * nq + i, OFF_Q // qw + g)),
            pl.BlockSpec((1, 1, seq, LANES), lambda b, g, i: (b, g, 0, 0)),
            pl.BlockSpec((1, HEAD_DIM, seq), lambda b, g, i: (b, g, 0)),
            pl.BlockSpec((1, 1, nblk, HEAD_DIM), lambda b, g, i: (b, g, 0, 0)),
            pl.BlockSpec((HEAD_DIM, tq), lambda b, g, i: (0, 0)),
            pl.BlockSpec((1, 1, cols), lambda b, g, i: (g, 0, 0)),
        ],
        out_specs=pl.BlockSpec((tq, qw), lambda b, g, i: (b * nq + i, g)),
        out_shape=jax.ShapeDtypeStruct((batch * seq, N_HEADS * HEAD_DIM), f32),
        scratch_shapes=[
            pltpu.VMEM((LANES, cols), bf16),
            pltpu.VMEM((nblk, cols), f32),
            pltpu.VMEM((HEAD_DIM + 2 * SUBLANES, cols), f32),
            pltpu.VMEM((1, cols), f32),
            pltpu.VMEM((2, MOBA_BLOCK, cols), f32),
            pltpu.VMEM((2, 1, cols), f32),
        ],
        compiler_params=_cparams(("arbitrary", "arbitrary", "arbitrary")),
        name="moba_prompt",
    )(proj, kb, vt, kmean_g, q_gain_col, slopes2)


def _head_expand():
    r = lax.broadcasted_iota(jnp.int32, (LANES, D_INNER), 0)
    c = lax.broadcasted_iota(jnp.int32, (LANES, D_INNER), 1) // SSM_HEAD_DIM
    return jnp.where(r == c, 1.0, 0.0).astype(bf16)


def _ssd_conv_act(ext_ref, rows, cw_ref, cb_ref, base):
    conv = cb_ref[...] + cw_ref[0:1, :] * ext_ref[pl.ds(base, rows), :]
    for i in range(1, CONV_WIDTH):
        conv = conv + cw_ref[i:i + 1, :] * ext_ref[pl.ds(base + i, rows), :]
    return _silu(conv)


def _ssd_chunk(act, dt_raw, allowed, dtb_ref, alog_ref):
    rows = act.shape[0]
    xs = act[:, :D_INNER]
    gn = SSM_GROUPS * D_STATE
    bm = act[:, D_INNER:D_INNER + gn]
    cm = act[:, D_INNER + gn:]
    dt = _softplus(dt_raw + dtb_ref[...])
    a = -jnp.exp(alog_ref[...])
    mask_b = jnp.where(allowed, 1.0, 0.0).astype(bf16)
    cs = _dot_exact_lhs(mask_b, dt * a)
    expand = _head_expand()
    dt_e = _dot_exact_rhs(dt, expand)
    cs_e = _dot_exact_rhs(cs, expand)
    xdt = xs * dt_e
    csT = cs.T
    xdt_b = xdt.astype(bf16)
    ypairs = []
    lane = lax.broadcasted_iota(jnp.int32, (rows, LANES), 1)
    for g in range(SSM_GROUPS):
        cb = _dot_nt(cm[:, g * D_STATE:(g + 1) * D_STATE].astype(bf16), bm[:, g * D_STATE:(g + 1) * D_STATE].astype(bf16))
        hpg = SSM_HEADS // SSM_GROUPS
        for pair in range(hpg // 2):
            res = []
            for k in range(2):
                h = g * hpg + pair * 2 + k
                seg = cs[:, h:h + 1] - csT[h:h + 1, :]
                m = (cb * jnp.exp(jnp.where(allowed, seg, NEG))).astype(bf16)
                lo = (g * hpg + pair * 2) * SSM_HEAD_DIM
                res.append(_dot(m, xdt_b[:, lo:lo + LANES]))
            ypairs.append(jnp.where(lane < SSM_HEAD_DIM, res[0], res[1]))
    y_diag = jnp.concatenate(ypairs, axis=1)
    return xs, bm, cm, dt_e, cs_e, xdt, y_diag


def _ssd_finish(y, xs, z, dexp_ref, norm_ref):
    y = (y + dexp_ref[...] * xs) * _silu(z)
    gw = D_INNER // SSM_GROUPS
    outs = []
    for g in range(SSM_GROUPS):
        yg = y[:, g * gw:(g + 1) * gw]
        ms = jnp.mean(yg * yg, axis=-1, keepdims=True)
        outs.append(yg * lax.rsqrt(ms + EPS) * norm_ref[:, g * gw:(g + 1) * gw])
    return jnp.concatenate(outs, axis=1)


def _ssd_prompt_kernel(xbc_ref, z_ref, dt_ref, cw_ref, cb_ref, dtb_ref, alog_ref, dexp_ref, norm_ref,
                       y_ref, h_ref, ext_ref, ht_ref, *, rows):
    c = pl.program_id(1)

    @pl.when(c == 0)
    def _():
        ext_ref[pl.ds(0, SUBLANES), :] = jnp.zeros((SUBLANES, CONV_DIM), f32)
        ht_ref[...] = jnp.zeros(ht_ref.shape, f32)

    xt = xbc_ref[...]
    ext_ref[pl.ds(SUBLANES, rows), :] = xt
    act = _ssd_conv_act(ext_ref, rows, cw_ref, cb_ref, base=SUBLANES - (CONV_WIDTH - 1))
    ext_ref[pl.ds(0, SUBLANES), :] = xt[rows - SUBLANES:rows, :]

    ri = lax.broadcasted_iota(jnp.int32, (rows, rows), 0)
    ci = lax.broadcasted_iota(jnp.int32, (rows, rows), 1)
    xs, bm, cm, dt_e, cs_e, xdt, y = _ssd_chunk(act, dt_ref[...], ci <= ri, dtb_ref, alog_ref)

    cs_last = cs_e[rows - 1:rows, :]
    ecs = jnp.exp(cs_e)
    xdte = (xdt * jnp.exp(cs_last - cs_e)).astype(bf16)
    dec = jnp.exp(cs_last)
    hw = D_INNER // SSM_GROUPS
    yoff = []
    for g in range(SSM_GROUPS):
        ht_g = ht_ref[:, g * hw:(g + 1) * hw]
        yoff.append(_dot(cm[:, g * D_STATE:(g + 1) * D_STATE].astype(bf16), ht_g.astype(bf16)))
        upd = _dot_tn(bm[:, g * D_STATE:(g + 1) * D_STATE].astype(bf16), xdte[:, g * hw:(g + 1) * hw])
        ht_ref[:, g * hw:(g + 1) * hw] = ht_g * dec[:, g * hw:(g + 1) * hw] + upd
    y = y + jnp.concatenate(yoff, axis=1) * ecs
    y_ref[...] = _ssd_finish(y, xs, z_ref[...], dexp_ref, norm_ref)

    @pl.when(c == pl.num_programs(1) - 1)
    def _():
        h_ref[0] = ht_ref[...].T.reshape(SSM_HEADS, SSM_HEAD_DIM, D_STATE)


def ssd_prompt(proj, conv_w, conv_b_row, dtb_row, alog_row, dexp_row, norm_row, batch, seq, rows):
    nc = seq // rows
    return pl.pallas_call(
        functools.partial(_ssd_prompt_kernel, rows=rows),
        grid=(batch, nc),
        in_specs=[
            pl.BlockSpec((rows, CONV_DIM), lambda b, c: (b * nc + c, OFF_XBC // CONV_DIM)),
            pl.BlockSpec((rows, D_INNER), lambda b, c: (b * nc + c, OFF_Z // D_INNER)),
            pl.BlockSpec((rows, LANES), lambda b, c: (b * nc + c, OFF_DT // LANES)),
            pl.BlockSpec((CONV_WIDTH, CONV_DIM), lambda b, c: (0, 0)),
            pl.BlockSpec((1, CONV_DIM), lambda b, c: (0, 0)),
            pl.BlockSpec((1, LANES), lambda b, c: (0, 0)),
            pl.BlockSpec((1, LANES), lambda b, c: (0, 0)),
            pl.BlockSpec((1, D_INNER), lambda b, c: (0, 0)),
            pl.BlockSpec((1, D_INNER), lambda b, c: (0, 0)),
        ],
        out_specs=[
            pl.BlockSpec((rows, D_INNER), lambda b, c: (b * nc + c, 0)),
            pl.BlockSpec((1, SSM_HEADS, SSM_HEAD_DIM, D_STATE), lambda b, c: (b, 0, 0, 0)),
        ],
        out_shape=[
            jax.ShapeDtypeStruct((batch * seq, D_INNER), f32),
            jax.ShapeDtypeStruct((batch, SSM_HEADS, SSM_HEAD_DIM, D_STATE), f32),
        ],
        scratch_shapes=[
            pltpu.VMEM((rows + SUBLANES, CONV_DIM), f32),
            pltpu.VMEM((D_STATE, D_INNER), f32),
        ],
        compiler_params=_cparams(("arbitrary", "arbitrary")),
        name="ssd_prompt",
    )(proj, proj, proj, conv_w, conv_b_row, dtb_row, alog_row, dexp_row, norm_row)


def _mem_attend_kernel(q_ref, mk_ref, mv_ref, g_ref, o_ref, *, head_axis):
    outs = []
    for h in range(MEM_HEADS):
        lo = h * MEM_HEAD_DIM
        q = q_ref[:, lo:lo + MEM_HEAD_DIM]
        ms = jnp.mean(q * q, axis=-1, keepdims=True)
        qn = (q * lax.rsqrt(ms + EPS) * g_ref[...] * (MEM_HEAD_DIM ** -0.5)).astype(bf16)
        k = mk_ref[0, :, h, :] if head_axis else mk_ref[0, :, lo:lo + MEM_HEAD_DIM]
        v = mv_ref[0, :, h, :] if head_axis else mv_ref[0, :, lo:lo + MEM_HEAD_DIM]
        s = _dot_nt(qn, k.astype(bf16))
        m = jnp.max(s, axis=-1, keepdims=True)
        p = jnp.exp(s - m)
        l = jnp.sum(p, axis=-1, keepdims=True)
        outs.append(_dot(p.astype(bf16), v.astype(bf16)) / l)
    o_ref[...] = jnp.concatenate(outs, axis=1)


def mem_attend(proj, mk, mv, mq_gain_row, nseq, rows_per_seq, tq):
    w = MEM_HEADS * MEM_HEAD_DIM
    nq = rows_per_seq // tq
    head_axis = mk.ndim == 4
    if head_axis:
        kv_spec = pl.BlockSpec((1, MEM_TOKENS, MEM_HEADS, MEM_HEAD_DIM), lambda b, i: (b, 0, 0, 0))
    else:
        kv_spec = pl.BlockSpec((1, MEM_TOKENS, w), lambda b, i: (b, 0, 0))
    return pl.pallas_call(
        functools.partial(_mem_attend_kernel, head_axis=head_axis),
        grid=(nseq, nq),
        in_specs=[
            pl.BlockSpec((tq, w), lambda b, i: (b * nq + i, OFF_MQ // w)),
            kv_spec,
            kv_spec,
            pl.BlockSpec((1, MEM_HEAD_DIM), lambda b, i: (0, 0)),
        ],
        out_specs=pl.BlockSpec((tq, w), lambda b, i: (b * nq + i, 0)),
        out_shape=jax.ShapeDtypeStruct((nseq * rows_per_seq, w), f32),
        compiler_params=_cparams(("arbitrary", "arbitrary")),
        name="mem_attend",
    )(proj, mk, mv, mq_gain_row)


def _merge_kernel(x_ref, a_ref, s_ref, m_ref, ga_ref, gs_ref, gm_ref, wa_ref, ws_ref, wm_ref, wo_ref, o_ref):
    mixed = _sigmoid(ga_ref[...]) * _dot(a_ref[...].astype(bf16), wa_ref[...])
    mixed = mixed + _sigmoid(gs_ref[...]) * _dot(s_ref[...].astype(bf16), ws_ref[...])
    mixed = mixed + _sigmoid(gm_ref[...]) * _dot(m_ref[...].astype(bf16), wm_ref[...])
    o_ref[...] = x_ref[...] + _dot(mixed.astype(bf16), wo_ref[...])


def merge(x, attn_o, ssm_o, mem_o, proj, wa, ws, wm, wo, tm):
    t, d = x.shape
    row = lambda i: (i, 0)
    const = lambda i: (0, 0)
    gate_spec = lambda k: pl.BlockSpec((tm, d), lambda i: (i, OFF_GATES // d + k))
    w_spec = pl.BlockSpec((d, d), const)
    return pl.pallas_call(
        _merge_kernel,
        grid=(t // tm,),
        in_specs=[pl.BlockSpec((tm, d), row)] * 4 + [gate_spec(0), gate_spec(1), gate_spec(2)] + [w_spec] * 4,
        out_specs=pl.BlockSpec((tm, d), row),
        out_shape=jax.ShapeDtypeStruct((t, d), f32),
        compiler_params=_cparams(("arbitrary",)),
        name="merge",
    )(x, attn_o, ssm_o, mem_o, proj, proj, proj, wa, ws, wm, wo)


def _ffn_kernel(x_ref, g_ref, wg_ref, wu_ref, wd_ref, o_ref):
    x = x_ref[...]
    ms = jnp.mean(x * x, axis=-1, keepdims=True)
    h = (x * lax.rsqrt(ms + EPS) * g_ref[...]).astype(bf16)
    act = _silu(_dot(h, wg_ref[...])) * _dot(h, wu_ref[...])
    o_ref[...] = x + _dot(act.astype(bf16), wd_ref[...])


def ffn(x, gain_row, wg, wu, wd, tm):
    t, d = x.shape
    dff = wg.shape[1]
    const = lambda i: (0, 0)
    single = pl.Buffered(1)
    return pl.pallas_call(
        _ffn_kernel,
        grid=(t // tm,),
        in_specs=[
            pl.BlockSpec((tm, d), lambda i: (i, 0)),
            pl.BlockSpec((1, d), const),
            pl.BlockSpec((d, dff), const, pipeline_mode=single),
            pl.BlockSpec((d, dff), const, pipeline_mode=single),
            pl.BlockSpec((dff, d), const, pipeline_mode=single),
        ],
        out_specs=pl.BlockSpec((tm, d), lambda i: (i, 0)),
        out_shape=jax.ShapeDtypeStruct((t, d), f32),
        compiler_params=_cparams(("arbitrary",)),
        name="ffn",
    )(x, gain_row, wg, wu, wd)


def _top3_cols(gate, jb):
    sel = jnp.zeros(gate.shape, jnp.bool_)
    for _ in range(MOBA_TOPK):
        m = jnp.max(gate, axis=-1, keepdims=True)
        idx = jnp.min(jnp.where(gate == m, jb, gate.shape[1]), axis=-1, keepdims=True)
        hit = jb == idx
        sel = jnp.logical_or(sel, jnp.logical_and(hit, m > -jnp.inf))
        gate = jnp.where(hit, -jnp.inf, gate)
    return sel


def _moba_sample_kernel(pt_ref, q_ref, kn_ref, vn_ref, qg_ref, sl_ref, ck_hbm, cv_hbm, o_ref,
                        kbuf, vbuf, s_ref, sem, *, n_pages):
    b = pl.program_id(0)
    nseq = pl.num_programs(0)
    kvw = N_KV_HEADS * HEAD_DIM
    nrow = Q_PER_KV * N_KV_HEADS * SAMPLE_ROWS
    nblk = n_pages * PAGE_SIZE // MOBA_BLOCK
    ppb = MOBA_BLOCK // PAGE_SIZE
    past = n_pages * PAGE_SIZE
    slot = b % 2

    def k_copy(seq, sl, p):
        return pltpu.make_async_copy(ck_hbm.at[pt_ref[seq, p]], kbuf.at[sl, p], sem.at[0, sl])

    def v_copy(seq, sl, p):
        return pltpu.make_async_copy(cv_hbm.at[pt_ref[seq, p]], vbuf.at[sl, p], sem.at[1, sl])

    def start_all(seq, sl):
        for p in range(n_pages):
            k_copy(seq, sl, p).start()
        for p in range(n_pages):
            v_copy(seq, sl, p).start()

    @pl.when(b == 0)
    def _():
        start_all(0, 0)

    @pl.when(b + 1 < nseq)
    def _():
        start_all(b + 1, 1 - slot)

    qb = q_ref[...]
    lane_g = lax.broadcasted_iota(jnp.int32, (SAMPLE_ROWS, kvw), 1) // HEAD_DIM
    pieces = []
    for r in range(Q_PER_KV):
        slab = qb[:, r * kvw:(r + 1) * kvw]
        for g in range(N_KV_HEADS):
            pieces.append(jnp.where(lane_g == g, slab, 0.0))
    qbd = jnp.concatenate(pieces, axis=0)
    ms = jnp.sum(qbd * qbd, axis=-1, keepdims=True) * (1.0 / HEAD_DIM)
    qn = qbd * lax.rsqrt(ms + EPS) * qg_ref[...]
    qs = (qn * (HEAD_DIM ** -0.5 * LOG2E)).astype(bf16)

    for p in range(n_pages):
        k_copy(b, slot, p).wait()

    def k_page(p):
        return kbuf[slot, p].reshape(kvw, PAGE_SIZE)

    lane = lax.broadcasted_iota(jnp.int32, (kvw, LANES), 1)
    kmean_t = jnp.zeros((kvw, LANES), f32)
    for j in range(nblk):
        blk = k_page(j * ppb)
        for pp in range(1, ppb):
            blk = blk + k_page(j * ppb + pp)
        kmean_t = kmean_t + jnp.where(lane == j, jnp.sum(blk, axis=-1, keepdims=True) * (1.0 / MOBA_BLOCK), 0.0)
    kh, kmid, kl = _split3(kmean_t)
    qh, qmid, ql = _split3(qn)
    gate = _dot(qh, kh) + _dot(qh, kmid) + _dot(qmid, kh) + _dot(qh, kl) + _dot(ql, kh) + _dot(qmid, kmid)
    jb = lax.broadcasted_iota(jnp.int32, (nrow, LANES), 1)
    gate = jnp.where(jb < nblk, gate, -jnp.inf)
    pen = jnp.where(_top3_cols(gate, jb), 0.0, NEG)

    slope = sl_ref[...]
    kk = lax.broadcasted_iota(jnp.int32, (nrow, PAGE_SIZE), 1)
    for p in range(n_pages):
        j = p // ppb
        rel = (kk + (p * PAGE_SIZE - past)).astype(f32)
        s_ref[:, p * PAGE_SIZE:(p + 1) * PAGE_SIZE] = _dot(qs, k_page(p).astype(bf16)) + slope * rel + pen[:, j:j + 1]

    zpad = jnp.zeros((LANES - SAMPLE_ROWS, kvw), f32)
    knew = jnp.concatenate([kn_ref[...], zpad], axis=0).astype(bf16)
    vnew = jnp.concatenate([vn_ref[...], zpad], axis=0).astype(bf16)
    kn_lane = lax.broadcasted_iota(jnp.int32, (nrow, LANES), 1)
    tok = lax.broadcasted_iota(jnp.int32, (nrow, LANES), 0) % SAMPLE_ROWS
    s_own = _dot_nt(qs, knew) + slope * kn_lane.astype(f32)
    s_ref[:, past:past + LANES] = jnp.where(kn_lane <= tok, s_own, NEG)

    m = jnp.max(s_ref[...], axis=-1, keepdims=True)
    p_own = jnp.exp2(s_ref[:, past:past + LANES] - m)
    psum = p_own
    acc = _dot(p_own.astype(bf16), vnew)

    for p in range(n_pages):
        v_copy(b, slot, p).wait()

    for p in range(n_pages):
        pp = jnp.exp2(s_ref[:, p * PAGE_SIZE:(p + 1) * PAGE_SIZE] - m)
        psum = psum + pp
        vp = vbuf[slot, p].reshape(kvw, PAGE_SIZE).astype(bf16)
        acc = acc + _dot_nt(pp.astype(bf16), vp)
    o = acc / jnp.sum(psum, axis=-1, keepdims=True)

    slabs = []
    for r in range(Q_PER_KV):
        slab = jnp.zeros((SAMPLE_ROWS, kvw), f32)
        for g in range(N_KV_HEADS):
            lo = (r * N_KV_HEADS + g) * SAMPLE_ROWS
            slab = slab + jnp.where(lane_g == g, o[lo:lo + SAMPLE_ROWS, :], 0.0)
        slabs.append(slab)
    o_ref[...] = jnp.concatenate(slabs, axis=1)


def moba_sample(page_table, proj, kn_new, ck, cv, q_gain_row, slope_rows, nseq):
    kvw = N_KV_HEADS * HEAD_DIM
    qw = N_HEADS * HEAD_DIM
    n_pages = page_table.shape[1]
    nrow = Q_PER_KV * N_KV_HEADS * SAMPLE_ROWS
    past = n_pages * PAGE_SIZE
    grid_spec = pltpu.PrefetchScalarGridSpec(
        num_scalar_prefetch=1,
        grid=(nseq,),
        in_specs=[
            pl.BlockSpec((SAMPLE_ROWS, qw), lambda b, pt: (b, OFF_Q // qw)),
            pl.BlockSpec((SAMPLE_ROWS, kvw), lambda b, pt: (b, 0)),
            pl.BlockSpec((SAMPLE_ROWS, kvw), lambda b, pt: (b, OFF_V // kvw)),
            pl.BlockSpec((1, kvw), lambda b, pt: (0, 0)),
            pl.BlockSpec((nrow, LANES), lambda b, pt: (0, 0)),
            pl.BlockSpec(memory_space=pl.ANY),
            pl.BlockSpec(memory_space=pl.ANY),
        ],
        out_specs=pl.BlockSpec((SAMPLE_ROWS, qw), lambda b, pt: (b, 0)),
        scratch_shapes=[
            pltpu.VMEM((2, n_pages, N_KV_HEADS, HEAD_DIM, PAGE_SIZE), f32),
            pltpu.VMEM((2, n_pages, N_KV_HEADS, HEAD_DIM, PAGE_SIZE), f32),
            pltpu.VMEM((nrow, past + LANES), f32),
            pltpu.SemaphoreType.DMA((2, 2)),
        ],
    )
    return pl.pallas_call(
        functools.partial(_moba_sample_kernel, n_pages=n_pages),
        grid_spec=grid_spec,
        out_shape=jax.ShapeDtypeStruct((nseq * SAMPLE_ROWS, qw), f32),
        compiler_params=_cparams(("arbitrary",)),
        name="moba_sample",
    )(page_table, proj, kn_new, proj, q_gain_row, slope_rows, ck, cv)


def _ssd_sample_kernel(ext_ref, z_ref, dt_ref, h0_ref, cw_ref, cb_ref, dtb_ref, alog_ref, dexp_ref, norm_ref,
                       y_ref, h_ref, scr_ref, *, nseq_step, n_new):
    rows = nseq_step * SAMPLE_ROWS
    scr_ref[pl.ds(0, rows), :] = ext_ref[...]
    scr_ref[pl.ds(rows, SUBLANES), :] = jnp.zeros((SUBLANES, CONV_DIM), f32)
    act = _ssd_conv_act(scr_ref, rows, cw_ref, cb_ref, base=0)

    ri = lax.broadcasted_iota(jnp.int32, (rows, rows), 0)
    ci = lax.broadcasted_iota(jnp.int32, (rows, rows), 1)
    allowed = jnp.logical_and(ri // SAMPLE_ROWS == ci // SAMPLE_ROWS,
                              jnp.logical_and(ci % SAMPLE_ROWS <= ri % SAMPLE_ROWS, ci % SAMPLE_ROWS < n_new))
    xs, bm, cm, dt_e, cs_e, xdt, y = _ssd_chunk(act, dt_ref[...], allowed, dtb_ref, alog_ref)

    valid = lax.broadcasted_iota(jnp.int32, (SAMPLE_ROWS, 1), 0) < n_new
    ecs = jnp.exp(cs_e)
    hw = D_INNER // SSM_GROUPS
    hpg = SSM_HEADS // SSM_GROUPS
    yoff_rows = []
    for i in range(nseq_step):
        lo = i * SAMPLE_ROWS
        cs_i = cs_e[lo:lo + SAMPLE_ROWS, :]
        cs_last = cs_e[lo + n_new - 1:lo + n_new, :]
        xdte = jnp.where(valid, xdt[lo:lo + SAMPLE_ROWS, :] * jnp.exp(cs_last - cs_i), 0.0)
        yg = []
        upd = []
        for g in range(SSM_GROUPS):
            h0g = h0_ref[i, g * hpg:(g + 1) * hpg].reshape(hw, D_STATE)
            yg.append(_dot_nt(cm[lo:lo + SAMPLE_ROWS, g * D_STATE:(g + 1) * D_STATE], h0g))
            upd.append(_dot_tn(xdte[:, g * hw:(g + 1) * hw], bm[lo:lo + SAMPLE_ROWS, g * D_STATE:(g + 1) * D_STATE]))
        yoff_rows.append(jnp.concatenate(yg, axis=1))
        dec_col = jnp.broadcast_to(jnp.exp(cs_last), (SUBLANES, D_INNER)).T[:, 0:1]
        h_new = h0_ref[i].reshape(D_INNER, D_STATE) * dec_col + jnp.concatenate(upd, axis=0)
        h_ref[i] = h_new.reshape(SSM_HEADS, SSM_HEAD_DIM, D_STATE)
    y = y + jnp.concatenate(yoff_rows, axis=0) * ecs
    y_ref[...] = _ssd_finish(y, xs, z_ref[...], dexp_ref, norm_ref)


def ssd_sample(ext, proj, h0, conv_w, conv_b_row, dtb_row, alog_row, dexp_row, norm_row, nseq, nseq_step, n_new):
    rows = nseq_step * SAMPLE_ROWS
    const = lambda i: (0, 0)
    state_spec = pl.BlockSpec((nseq_step, SSM_HEADS, SSM_HEAD_DIM, D_STATE), lambda i: (i, 0, 0, 0))
    return pl.pallas_call(
        functools.partial(_ssd_sample_kernel, nseq_step=nseq_step, n_new=n_new),
        grid=(nseq // nseq_step,),
        in_specs=[
            pl.BlockSpec((rows, CONV_DIM), lambda i: (i, 0)),
            pl.BlockSpec((rows, D_INNER), lambda i: (i, OFF_Z // D_INNER)),
            pl.BlockSpec((rows, LANES), lambda i: (i, OFF_DT // LANES)),
            state_spec,
            pl.BlockSpec((CONV_WIDTH, CONV_DIM), const),
            pl.BlockSpec((1, CONV_DIM), const),
            pl.BlockSpec((1, LANES), const),
            pl.BlockSpec((1, LANES), const),
            pl.BlockSpec((1, D_INNER), const),
            pl.BlockSpec((1, D_INNER), const),
        ],
        out_specs=[pl.BlockSpec((rows, D_INNER), lambda i: (i, 0)), state_spec],
        out_shape=[
            jax.ShapeDtypeStruct((nseq * SAMPLE_ROWS, D_INNER), f32),
            jax.ShapeDtypeStruct(h0.shape, f32),
        ],
        scratch_shapes=[pltpu.VMEM((rows + SUBLANES, CONV_DIM), f32)],
        compiler_params=_cparams(("arbitrary",)),
        name="ssd_sample",
    )(ext, proj, proj, h0, conv_w, conv_b_row, dtb_row, alog_row, dexp_row, norm_row)


IN_SIZES = (N_HEADS * HEAD_DIM, N_KV_HEADS * HEAD_DIM, N_KV_HEADS * HEAD_DIM, D_INNER, CONV_DIM, SSM_HEADS,
            MEM_HEADS * MEM_HEAD_DIM, 3 * D_MODEL)
TM = 512
TN_PROJ = 2944
TQ = 256
SSD_ROWS = 256
SSD_SAMPLE_SEQS = 8


def _pad_lanes(v):
    return jnp.pad(v, (0, LANES - v.shape[0])).reshape(1, LANES)


def kernel(x_prompt, x_sample, mem_prompt, cache_k, cache_v, page_table, state_conv, state_ssm, cache_mem_k, cache_mem_v, norm_mix, w_in, q_norm, k_norm, conv_w, conv_b, dt_bias, a_log, d_skip, ssm_norm, mem_norm, w_mem_kv, mq_norm, mk_norm, w_attn_br, w_ssm_br, w_mem_br, w_out, norm_ffn, w_gate, w_up, w_down):
    assert w_in.shape[0] == 1, "single layer"
    batch, seq, d = x_prompt.shape
    nseq, n_new, _ = x_sample.shape
    kvw = N_KV_HEADS * HEAD_DIM
    qw = N_HEADS * HEAD_DIM

    offs = np.cumsum(IN_SIZES)[:-1].tolist()
    wq, wk, wv, wz, wxbc, wdt, wmq, wgates = jnp.split(w_in[0], offs, axis=1)
    wdt = jnp.pad(wdt, ((0, 0), (0, LANES - SSM_HEADS)))
    tail = [wz, wxbc, wmq, wgates, wk, wv, wdt]
    w_prompt = jnp.concatenate([wq] + tail, axis=1).astype(bf16)
    wq_rgd = wq.reshape(d, N_KV_HEADS, Q_PER_KV, HEAD_DIM).transpose(0, 2, 1, 3).reshape(d, qw)
    w_sample = jnp.concatenate([wq_rgd] + tail, axis=1).astype(bf16)
    wa = w_attn_br[0].astype(bf16)
    wa_rgd = w_attn_br[0].reshape(N_KV_HEADS, Q_PER_KV, HEAD_DIM, d).transpose(1, 0, 2, 3).reshape(qw, d).astype(bf16)
    ws, wm, wo = w_ssm_br[0].astype(bf16), w_mem_br[0].astype(bf16), w_out[0].astype(bf16)
    wg, wu, wd = w_gate[0].astype(bf16), w_up[0].astype(bf16), w_down[0].astype(bf16)

    k_gain = jnp.tile(k_norm[0], N_KV_HEADS).reshape(1, kvw)
    conv_b_row = conv_b[0].reshape(1, CONV_DIM)
    dtb_row, alog_row = _pad_lanes(dt_bias[0]), _pad_lanes(a_log[0])
    dexp_row = jnp.repeat(d_skip[0], SSM_HEAD_DIM).reshape(1, D_INNER)
    ssm_norm_row = ssm_norm[0].reshape(1, D_INNER)
    mq_gain = mq_norm[0].reshape(1, MEM_HEAD_DIM)
    ffn_gain = norm_ffn[0].reshape(1, d)
    slopes2 = jnp.exp2(-8.0 * jnp.arange(1, N_HEADS + 1, dtype=f32) / N_HEADS) * LOG2E

    xp = x_prompt.reshape(batch * seq, d)
    proj = inproj(xp, norm_mix[0], w_prompt, TM, TN_PROJ)
    kn, kb, vt, kmean = kprep(proj, k_gain, batch, seq, 2048)
    kmean_g = kmean.reshape(batch, seq // MOBA_BLOCK, N_KV_HEADS, HEAD_DIM).transpose(0, 2, 1, 3)
    slope_cols = jnp.repeat(slopes2.reshape(N_KV_HEADS, Q_PER_KV), TQ, axis=1).reshape(N_KV_HEADS, 1, Q_PER_KV * TQ)
    q_gain_col = jnp.broadcast_to(q_norm[0][:, None], (HEAD_DIM, TQ))
    attn_o = moba_prompt(proj, kb, vt, kmean_g, q_gain_col, slope_cols, batch, seq, TQ)
    ssm_o, h_prompt = ssd_prompt(proj, conv_w[0], conv_b_row, dtb_row, alog_row, dexp_row, ssm_norm_row,
                                 batch, seq, SSD_ROWS)
    mem_kv = inproj(mem_prompt.reshape(batch * MEM_TOKENS, d), mem_norm[0], w_mem_kv[0].astype(bf16), TM, qw)
    mk = headnorm(mem_kv, 0, qw, jnp.tile(mk_norm[0], MEM_HEADS).reshape(1, qw), MEM_HEAD_DIM, TM)
    mv = mem_kv[:, qw:]
    mem_o = mem_attend(proj, mk.reshape(batch, MEM_TOKENS, qw), mv.reshape(batch, MEM_TOKENS, qw), mq_gain,
                       batch, seq, TM)
    x1 = merge(xp, attn_o, ssm_o, mem_o, proj, wa, ws, wm, wo, TM)
    y_prompt = ffn(x1, ffn_gain, wg, wu, wd, TM).reshape(batch, seq, d)

    k_prompt = kn.reshape(1, batch, seq, N_KV_HEADS, HEAD_DIM)
    v_prompt = proj[:, OFF_V:OFF_V + kvw].reshape(1, batch, seq, N_KV_HEADS, HEAD_DIM)
    conv_prompt = proj.reshape(batch, seq, N_PROJ)[:, seq - (CONV_WIDTH - 1):, OFF_XBC:OFF_XBC + CONV_DIM][None]
    mem_k_prompt = mk.reshape(1, batch, MEM_TOKENS, MEM_HEADS, MEM_HEAD_DIM)
    mem_v_prompt = mv.reshape(1, batch, MEM_TOKENS, MEM_HEADS, MEM_HEAD_DIM)

    xs = jnp.pad(x_sample, ((0, 0), (0, SAMPLE_ROWS - n_new), (0, 0))).reshape(nseq * SAMPLE_ROWS, d)
    proj_s = inproj(xs, norm_mix[0], w_sample, TM, TN_PROJ)
    kn_s = headnorm(proj_s, OFF_K // kvw, kvw, k_gain, HEAD_DIM, TM)
    slope_rows = jnp.broadcast_to(
        jnp.repeat(slopes2.reshape(N_KV_HEADS, Q_PER_KV).T.reshape(-1), SAMPLE_ROWS)[:, None], (N_HEADS * SAMPLE_ROWS, LANES))
    ck_t = jnp.transpose(cache_k[0], (0, 2, 3, 1))
    cv_t = jnp.transpose(cache_v[0], (0, 2, 3, 1))
    attn_s = moba_sample(page_table, proj_s, kn_s, ck_t, cv_t, jnp.tile(q_norm[0], N_KV_HEADS).reshape(1, kvw),
                         slope_rows, nseq)
    xbc_new = proj_s[:, OFF_XBC:OFF_XBC + CONV_DIM].reshape(nseq, SAMPLE_ROWS, CONV_DIM)[:, :n_new]
    ext = jnp.concatenate(
        [state_conv[0], xbc_new, jnp.zeros((nseq, SAMPLE_ROWS - (CONV_WIDTH - 1) - n_new, CONV_DIM), f32)], axis=1)
    ssm_s, h_sample = ssd_sample(ext.reshape(nseq * SAMPLE_ROWS, CONV_DIM), proj_s, state_ssm[0], conv_w[0], conv_b_row,
                                 dtb_row, alog_row, dexp_row, ssm_norm_row, nseq, SSD_SAMPLE_SEQS, n_new)
    mem_s = mem_attend(proj_s, cache_mem_k[0], cache_mem_v[0], mq_gain, nseq, SAMPLE_ROWS, SAMPLE_ROWS)
    x1s = merge(xs, attn_s, ssm_s, mem_s, proj_s, wa_rgd, ws, wm, wo, TM)
    y_sample = ffn(x1s, ffn_gain, wg, wu, wd, TM).reshape(nseq, SAMPLE_ROWS, d)[:, :n_new]

    k_sample = kn_s.reshape(nseq, SAMPLE_ROWS, N_KV_HEADS, HEAD_DIM)[:, :n_new][None]
    v_sample = proj_s[:, OFF_V:OFF_V + kvw].reshape(nseq, SAMPLE_ROWS, N_KV_HEADS, HEAD_DIM)[:, :n_new][None]
    conv_sample = ext[:, n_new:n_new + CONV_WIDTH - 1][None]

    return (y_prompt, y_sample, k_prompt, v_prompt, conv_prompt, h_prompt[None], mem_k_prompt, mem_v_prompt,
            k_sample, v_sample, conv_sample, h_sample[None])
```

```python
import functools
import math

import jax
import jax.numpy as jnp
import numpy as np
from jax import lax
from jax.experimental import pallas as pl
from jax.experimental.pallas import tpu as pltpu

f32 = jnp.float32
bf16 = jnp.bfloat16

D_MODEL = 1024
N_HEADS = 16
N_KV_HEADS = 4
HEAD_DIM = 64
Q_PER_KV = N_HEADS // N_KV_HEADS
MOBA_BLOCK = 256
MOBA_TOPK = 3
SSM_HEADS = 16
SSM_HEAD_DIM = 64
D_INNER = SSM_HEADS * SSM_HEAD_DIM
SSM_GROUPS = 4
D_STATE = 128
CONV_WIDTH = 4
CONV_DIM = D_INNER + 2 * SSM_GROUPS * D_STATE
MEM_TOKENS = 256
MEM_HEADS = 4
MEM_HEAD_DIM = 256
D_FF = 2816
EPS = 1e-6
PAGE_SIZE = 128

LANES = 128
SUBLANES = 8
LOG2E = 1.4426950408889634
NEG = -1e30
VMEM_LIMIT = 56 * 1024 * 1024

OFF_Q, OFF_Z, OFF_XBC, OFF_MQ, OFF_GATES, OFF_K, OFF_V, OFF_DT = 0, 1024, 2048, 4096, 5120, 8192, 8448, 8704
N_PROJ = 8832
SAMPLE_ROWS = 8
MOBA_GROUP = 4


def _cparams(sem):
    return pltpu.CompilerParams(dimension_semantics=sem, vmem_limit_bytes=VMEM_LIMIT)


def _split3(x):
    h = x.astype(bf16)
    r = x - h.astype(f32)
    m = r.astype(bf16)
    l = (r - m.astype(f32)).astype(bf16)
    return h, m, l


def _dot(a, b):
    return jnp.dot(a, b, preferred_element_type=f32)


def _dot_nt(a, b):
    return lax.dot_general(a, b, (((1,), (1,)), ((), ())), preferred_element_type=f32)


def _dot_tn(a, b):
    return lax.dot_general(a, b, (((0,), (0,)), ((), ())), preferred_element_type=f32)


def _dot_exact_rhs(x, m_bf16):
    h, m, l = _split3(x)
    return _dot(h, m_bf16) + _dot(m, m_bf16) + _dot(l, m_bf16)


def _dot_exact_lhs(m_bf16, x):
    h, m, l = _split3(x)
    return _dot(m_bf16, h) + _dot(m_bf16, m) + _dot(m_bf16, l)


def _silu(x):
    return x * (1.0 / (1.0 + jnp.exp(-x)))


def _sigmoid(x):
    return 1.0 / (1.0 + jnp.exp(-x))


def _softplus(x):
    return jnp.maximum(x, 0.0) + jnp.log(1.0 + jnp.exp(-jnp.abs(x)))


def _inproj_kernel(x_ref, g_ref, w_ref, o_ref):
    x = x_ref[...]
    ms = jnp.mean(x * x, axis=-1, keepdims=True)
    xn = (x * lax.rsqrt(ms + EPS) * g_ref[...]).astype(bf16)
    o_ref[...] = _dot(xn, w_ref[...])


def inproj(x, gain, w, tm, tn):
    t, d = x.shape
    n = w.shape[1]
    return pl.pallas_call(
        _inproj_kernel,
        grid=(n // tn, t // tm),
        in_specs=[
            pl.BlockSpec((tm, d), lambda j, i: (i, 0)),
            pl.BlockSpec((1, d), lambda j, i: (0, 0)),
            pl.BlockSpec((d, tn), lambda j, i: (0, j)),
        ],
        out_specs=pl.BlockSpec((tm, tn), lambda j, i: (i, j)),
        out_shape=jax.ShapeDtypeStruct((t, n), f32),
        compiler_params=_cparams(("arbitrary", "arbitrary")),
        name="inproj",
    )(x, gain.reshape(1, d), w)


def _seg_ones(width, seg):
    r = lax.broadcasted_iota(jnp.int32, (width, width), 0) // seg
    c = lax.broadcasted_iota(jnp.int32, (width, width), 1) // seg
    return jnp.where(r == c, 1.0, 0.0).astype(bf16)


def _seg_rmsnorm(x, gain_row, seg):
    ss = _dot_exact_rhs(x * x, _seg_ones(x.shape[1], seg))
    return x * lax.rsqrt(ss * (1.0 / seg) + EPS) * gain_row


def _headnorm_kernel(x_ref, g_ref, o_ref, *, seg):
    o_ref[...] = _seg_rmsnorm(x_ref[...], g_ref[...], seg)


def headnorm(x, col_block, width, gain_row, seg, tm):
    t = x.shape[0]
    return pl.pallas_call(
        functools.partial(_headnorm_kernel, seg=seg),
        grid=(t // tm,),
        in_specs=[
            pl.BlockSpec((tm, width), lambda i: (i, col_block)),
            pl.BlockSpec((1, width), lambda i: (0, 0)),
        ],
        out_specs=pl.BlockSpec((tm, width), lambda i: (i, 0)),
        out_shape=jax.ShapeDtypeStruct((t, width), f32),
        compiler_params=_cparams(("arbitrary",)),
        name="headnorm",
    )(x, gain_row)


def _kprep_kernel(k_ref, v_ref, g_ref, kn_ref, ka_ref, vt_ref, km_ref):
    kn = _seg_rmsnorm(k_ref[...], g_ref[...], HEAD_DIM)
    kn_ref[...] = kn
    kn_b = kn.astype(bf16)
    rows = kn.shape[0]
    kvw = kn.shape[1]
    r = lax.broadcasted_iota(jnp.int32, (kvw, LANES), 0)
    c = lax.broadcasted_iota(jnp.int32, (kvw, LANES), 1)
    lane = lax.broadcasted_iota(jnp.int32, (rows, LANES), 1)
    pos = (lax.broadcasted_iota(jnp.int32, (rows, LANES), 0) % MOBA_BLOCK).astype(f32)
    pos_lanes = jnp.where(jnp.logical_and(lane >= HEAD_DIM, lane < HEAD_DIM + 3), pos, 0.0)
    for g in range(N_KV_HEADS):
        pick = jnp.where(jnp.logical_and(r == c + g * HEAD_DIM, c < HEAD_DIM), 1.0, 0.0).astype(bf16)
        ka_ref[0, g] = (_dot(kn_b, pick) + pos_lanes).astype(bf16)
    nb = kn.shape[0] // MOBA_BLOCK
    km_ref[0] = jnp.sum(kn.reshape(nb, MOBA_BLOCK, kn.shape[1]), axis=1) * (1.0 / MOBA_BLOCK)
    vt_ref[0] = v_ref[...].astype(bf16).T


def kprep(proj, k_gain_row, batch, seq, tm):
    kvw = N_KV_HEADS * HEAD_DIM
    nt = seq // tm
    nb = tm // MOBA_BLOCK
    return pl.pallas_call(
        _kprep_kernel,
        grid=(batch, nt),
        in_specs=[
            pl.BlockSpec((tm, kvw), lambda b, i: (b * nt + i, OFF_K // kvw)),
            pl.BlockSpec((tm, kvw), lambda b, i: (b * nt + i, OFF_V // kvw)),
            pl.BlockSpec((1, kvw), lambda b, i: (0, 0)),
        ],
        out_specs=[
            pl.BlockSpec((tm, kvw), lambda b, i: (b * nt + i, 0)),
            pl.BlockSpec((1, N_KV_HEADS, tm, LANES), lambda b, i: (b, 0, i, 0)),
            pl.BlockSpec((1, kvw, tm), lambda b, i: (b, 0, i)),
            pl.BlockSpec((1, nb, kvw), lambda b, i: (b, i, 0)),
        ],
        out_shape=[
            jax.ShapeDtypeStruct((batch * seq, kvw), f32),
            jax.ShapeDtypeStruct((batch, N_KV_HEADS, seq, LANES), bf16),
            jax.ShapeDtypeStruct((batch, kvw, seq), bf16),
            jax.ShapeDtypeStruct((batch, seq // MOBA_BLOCK, kvw), f32),
        ],
        compiler_params=_cparams(("arbitrary", "arbitrary")),
        name="kprep",
    )(proj, proj, k_gain_row)


def _top3_rows(gate, jb):
    sel = jnp.zeros(gate.shape, jnp.bool_)
    for _ in range(MOBA_TOPK):
        m = jnp.max(gate, axis=0, keepdims=True)
        idx = jnp.min(jnp.where(gate == m, jb, gate.shape[0]), axis=0, keepdims=True)
        hit = jb == idx
        sel = jnp.logical_or(sel, jnp.logical_and(hit, m > -jnp.inf))
        gate = jnp.where(hit, -jnp.inf, gate)
    return sel


def _moba_prompt_kernel(q_ref, kb_ref, vt_ref, km_ref, qg_ref, sl_ref, o_ref,
                        qbd_ref, pen_ref, acc_ref, m_ref, s_ref, cm_ref, *, tq):
    qt = pl.program_id(2)
    per_blk = MOBA_BLOCK // tq
    own = qt // per_blk
    cols = Q_PER_KV * tq
    nblk = km_ref.shape[2]

    qT = q_ref[...].T
    parts = []
    for r in range(Q_PER_KV):
        xr = qT[r * HEAD_DIM:(r + 1) * HEAD_DIM, :]
        ms = jnp.mean(xr * xr, axis=0, keepdims=True)
        parts.append(xr * lax.rsqrt(ms + EPS) * qg_ref[...])
    qn = jnp.concatenate(parts, axis=1)

    km = km_ref[0, 0]
    kh, kmid, kl = _split3(km)
    qh, qmid, ql = _split3(qn)
    gate = _dot(kh, qh) + _dot(kh, qmid) + _dot(kmid, qh) + _dot(kh, ql) + _dot(kl, qh) + _dot(kmid, qmid)
    jb = lax.broadcasted_iota(jnp.int32, (nblk, cols), 0)
    gate = jnp.where(jb < own, gate, -jnp.inf)
    sel = _top3_rows(gate, jb)
    pen_ref[...] = jnp.where(jnp.logical_or(sel, jb == own), 0.0, NEG)

    slope = sl_ref[0]
    s_hi, s_mid, s_lo = _split3(slope)
    rowi = lax.broadcasted_iota(jnp.int32, (HEAD_DIM, cols), 0)
    slope_rows = jnp.where(rowi == 0, s_hi.astype(f32), jnp.where(rowi == 1, s_mid.astype(f32),
                           jnp.where(rowi == 2, s_lo.astype(f32), 0.0)))
    qbd_ref[...] = jnp.concatenate([qn * (HEAD_DIM ** -0.5 * LOG2E), slope_rows], axis=0).astype(bf16)

    kk = lax.broadcasted_iota(jnp.int32, (MOBA_BLOCK, cols), 0)
    ones = jnp.ones((2 * SUBLANES, MOBA_BLOCK), bf16)
    qin = (qt % per_blk) * tq + lax.broadcasted_iota(jnp.int32, (MOBA_BLOCK, cols), 1) % tq

    def blk(j):
        return pl.ds(pl.multiple_of(j * MOBA_BLOCK, MOBA_BLOCK), MOBA_BLOCK)

    def stage_scores(j, buf, mask=None):
        s = _dot(kb_ref[0, 0, blk(j), :], qbd_ref[...])
        if mask is not None:
            s = jnp.where(mask, s, NEG)
        s_ref[buf] = s
        cm_ref[buf] = jnp.max(s, axis=0, keepdims=True)

    def stage_accum(j, buf, valid=None, first=False):
        shift = slope * ((j - own) * MOBA_BLOCK).astype(f32) + pen_ref[pl.ds(j, 1), :]
        if valid is not None:
            shift = shift + jnp.where(valid, 0.0, NEG)
        mj = cm_ref[buf] + shift
        m_old = None if first else m_ref[...]
        m_new = mj if first else jnp.maximum(m_old, mj)
        p = jnp.exp2(s_ref[buf] - (m_new - shift)).astype(bf16)
        v_aug = jnp.concatenate([vt_ref[0, :, blk(j)], ones], axis=0)
        pv = _dot(v_aug, p)
        acc_ref[...] = pv if first else acc_ref[...] * jnp.exp2(m_old - m_new) + pv
        m_ref[...] = m_new

    last = jnp.maximum(own - 1, 0)
    stage_scores(own, 0, kk <= qin)
    stage_scores(0, 1)
    stage_accum(own, 0, first=True)

    def body(i, c):
        j0, j1, j2 = 2 * i, 2 * i + 1, 2 * i + 2
        stage_scores(jnp.minimum(j1, last), 0)
        stage_accum(j0, 1)
        stage_scores(jnp.minimum(j2, last), 1)
        stage_accum(jnp.minimum(j1, last), 0, valid=j1 < own)
        return c

    lax.fori_loop(0, (own + 1) // 2, body, 0)

    acc = acc_ref[...]
    o = acc[0:HEAD_DIM, :] / acc[HEAD_DIM:HEAD_DIM + 1, :]
    oT = jnp.concatenate([o[:, r * tq:(r + 1) * tq] for r in range(Q_PER_KV)], axis=0)
    o_ref[...] = oT.T


def moba_prompt(proj, kb, vt, kmean_g, q_gain_col, slopes2, batch, seq, tq):
    nq = seq // tq
    qw = Q_PER_KV * HEAD_DIM
    kvw = N_KV_HEADS * HEAD_DIM
    nblk = seq // MOBA_BLOCK
    cols = Q_PER_KV * tq
    return pl.pallas_call(
        functools.partial(_moba_prompt_kernel, tq=tq),
        grid=(batch, N_KV_HEADS, nq),
        in_specs=[
            pl.BlockSpec((tq, qw), lambda b, g, i: (b * nq + i, OFF_Q // qw + g)),
            pl.BlockSpec((1, 1, seq, LANES), lambda b, g, i: (b, g, 0, 0)),
            pl.BlockSpec((1, HEAD_DIM, seq), lambda b, g, i: (b, g, 0)),
            pl.BlockSpec((1, 1, nblk, HEAD_DIM), lambda b, g, i: (b, g, 0, 0)),
            pl.BlockSpec((HEAD_DIM, tq), lambda b, g, i: (0, 0)),
            pl.BlockSpec((1, 1, cols), lambda b, g, i: (g, 0, 0)),
        ],
        out_specs=pl.BlockSpec((tq, qw), lambda b, g, i: (b * nq + i, g)),
        out_shape=jax.ShapeDtypeStruct((batch * seq, N_HEADS * HEAD_DIM), f32),
        scratch_shapes=[
            pltpu.VMEM((LANES, cols), bf16),
            pltpu.VMEM((nblk, cols), f32),
            pltpu.VMEM((HEAD_DIM + 2 * SUBLANES, cols), f32),
            pltpu.VMEM((1, cols), f32),
            pltpu.VMEM((2, MOBA_BLOCK, cols), f32),
            pltpu.VMEM((2, 1, cols), f32),
        ],
        compiler_params=_cparams(("arbitrary", "arbitrary", "arbitrary")),
        name="moba_prompt",
    )(proj, kb, vt, kmean_g, q_gain_col, slopes2)


def _head_expand():
    r = lax.broadcasted_iota(jnp.int32, (LANES, D_INNER), 0)
    c = lax.broadcasted_iota(jnp.int32, (LANES, D_INNER), 1) // SSM_HEAD_DIM
    return jnp.where(r == c, 1.0, 0.0).astype(bf16)


def _ssd_conv_act(ext_ref, rows, cw_ref, cb_ref, base):
    conv = cb_ref[...] + cw_ref[0:1, :] * ext_ref[pl.ds(base, rows), :]
    for i in range(1, CONV_WIDTH):
        conv = conv + cw_ref[i:i + 1, :] * ext_ref[pl.ds(base + i, rows), :]
    return _silu(conv)


def _ssd_chunk(act, dt_raw, allowed, dtb_ref, alog_ref):
    rows = act.shape[0]
    xs = act[:, :D_INNER]
    gn = SSM_GROUPS * D_STATE
    bm = act[:, D_INNER:D_INNER + gn]
    cm = act[:, D_INNER + gn:]
    dt = _softplus(dt_raw + dtb_ref[...])
    a = -jnp.exp(alog_ref[...])
    mask_b = jnp.where(allowed, 1.0, 0.0).astype(bf16)
    cs = _dot_exact_lhs(mask_b, dt * a)
    expand = _head_expand()
    dt_e = _dot_exact_rhs(dt, expand)
    cs_e = _dot_exact_rhs(cs, expand)
    xdt = xs * dt_e
    csT = cs.T
    xdt_b = xdt.astype(bf16)
    ypairs = []
    lane = lax.broadcasted_iota(jnp.int32, (rows, LANES), 1)
    for g in range(SSM_GROUPS):
        cb = _dot_nt(cm[:, g * D_STATE:(g + 1) * D_STATE].astype(bf16), bm[:, g * D_STATE:(g + 1) * D_STATE].astype(bf16))
        hpg = SSM_HEADS // SSM_GROUPS
        for pair in range(hpg // 2):
            res = []
            for k in range(2):
                h = g * hpg + pair * 2 + k
                seg = cs[:, h:h + 1] - csT[h:h + 1, :]
                m = (cb * jnp.exp(jnp.where(allowed, seg, NEG))).astype(bf16)
                lo = (g * hpg + pair * 2) * SSM_HEAD_DIM
                res.append(_dot(m, xdt_b[:, lo:lo + LANES]))
            ypairs.append(jnp.where(lane < SSM_HEAD_DIM, res[0], res[1]))
    y_diag = jnp.concatenate(ypairs, axis=1)
    return xs, bm, cm, dt_e, cs_e, xdt, y_diag


def _ssd_finish(y, xs, z, dexp_ref, norm_ref):
    y = (y + dexp_ref[...] * xs) * _silu(z)
    gw = D_INNER // SSM_GROUPS
    outs = []
    for g in range(SSM_GROUPS):
        yg = y[:, g * gw:(g + 1) * gw]
        ms = jnp.mean(yg * yg, axis=-1, keepdims=True)
        outs.append(yg * lax.rsqrt(ms + EPS) * norm_ref[:, g * gw:(g + 1) * gw])
    return jnp.concatenate(outs, axis=1)


def _ssd_prompt_kernel(xbc_ref, z_ref, dt_ref, cw_ref, cb_ref, dtb_ref, alog_ref, dexp_ref, norm_ref,
                       y_ref, h_ref, ext_ref, ht_ref, *, rows):
    c = pl.program_id(1)

    @pl.when(c == 0)
    def _():
        ext_ref[pl.ds(0, SUBLANES), :] = jnp.zeros((SUBLANES, CONV_DIM), f32)
        ht_ref[...] = jnp.zeros(ht_ref.shape, f32)

    xt = xbc_ref[...]
    ext_ref[pl.ds(SUBLANES, rows), :] = xt
    act = _ssd_conv_act(ext_ref, rows, cw_ref, cb_ref, base=SUBLANES - (CONV_WIDTH - 1))
    ext_ref[pl.ds(0, SUBLANES), :] = xt[rows - SUBLANES:rows, :]

    ri = lax.broadcasted_iota(jnp.int32, (rows, rows), 0)
    ci = lax.broadcasted_iota(jnp.int32, (rows, rows), 1)
    xs, bm, cm, dt_e, cs_e, xdt, y = _ssd_chunk(act, dt_ref[...], ci <= ri, dtb_ref, alog_ref)

    cs_last = cs_e[rows - 1:rows, :]
    ecs = jnp.exp(cs_e)
    xdte = (xdt * jnp.exp(cs_last - cs_e)).astype(bf16)
    dec = jnp.exp(cs_last)
    hw = D_INNER // SSM_GROUPS
    yoff = []
    for g in range(SSM_GROUPS):
        ht_g = ht_ref[:, g * hw:(g + 1) * hw]
        yoff.append(_dot(cm[:, g * D_STATE:(g + 1) * D_STATE].astype(bf16), ht_g.astype(bf16)))
        upd = _dot_tn(bm[:, g * D_STATE:(g + 1) * D_STATE].astype(bf16), xdte[:, g * hw:(g + 1) * hw])
        ht_ref[:, g * hw:(g + 1) * hw] = ht_g * dec[:, g * hw:(g + 1) * hw] + upd
    y = y + jnp.concatenate(yoff, axis=1) * ecs
    y_ref[...] = _ssd_finish(y, xs, z_ref[...], dexp_ref, norm_ref)

    @pl.when(c == pl.num_programs(1) - 1)
    def _():
        h_ref[0] = ht_ref[...].T.reshape(SSM_HEADS, SSM_HEAD_DIM, D_STATE)


def ssd_prompt(proj, conv_w, conv_b_row, dtb_row, alog_row, dexp_row, norm_row, batch, seq, rows):
    nc = seq // rows
    return pl.pallas_call(
        functools.partial(_ssd_prompt_kernel, rows=rows),
        grid=(batch, nc),
        in_specs=[
            pl.BlockSpec((rows, CONV_DIM), lambda b, c: (b * nc + c, OFF_XBC // CONV_DIM)),
            pl.BlockSpec((rows, D_INNER), lambda b, c: (b * nc + c, OFF_Z // D_INNER)),
            pl.BlockSpec((rows, LANES), lambda b, c: (b * nc + c, OFF_DT // LANES)),
            pl.BlockSpec((CONV_WIDTH, CONV_DIM), lambda b, c: (0, 0)),
            pl.BlockSpec((1, CONV_DIM), lambda b, c: (0, 0)),
            pl.BlockSpec((1, LANES), lambda b, c: (0, 0)),
            pl.BlockSpec((1, LANES), lambda b, c: (0, 0)),
            pl.BlockSpec((1, D_INNER), lambda b, c: (0, 0)),
            pl.BlockSpec((1, D_INNER), lambda b, c: (0, 0)),
        ],
        out_specs=[
            pl.BlockSpec((rows, D_INNER), lambda b, c: (b * nc + c, 0)),
            pl.BlockSpec((1, SSM_HEADS, SSM_HEAD_DIM, D_STATE), lambda b, c: (b, 0, 0, 0)),
        ],
        out_shape=[
            jax.ShapeDtypeStruct((batch * seq, D_INNER), f32),
            jax.ShapeDtypeStruct((batch, SSM_HEADS, SSM_HEAD_DIM, D_STATE), f32),
        ],
        scratch_shapes=[
            pltpu.VMEM((rows + SUBLANES, CONV_DIM), f32),
            pltpu.VMEM((D_STATE, D_INNER), f32),
        ],
        compiler_params=_cparams(("arbitrary", "arbitrary")),
        name="ssd_prompt",
    )(proj, proj, proj, conv_w, conv_b_row, dtb_row, alog_row, dexp_row, norm_row)


def _mem_q(q_ref, g_ref, h):
    q = q_ref[:, h * MEM_HEAD_DIM:(h + 1) * MEM_HEAD_DIM]
    ms = jnp.mean(q * q, axis=-1, keepdims=True)
    return (q * lax.rsqrt(ms + EPS) * g_ref[...] * (MEM_HEAD_DIM ** -0.5)).astype(bf16)


def _mem_attend_kernel(q_ref, mk_ref, mv_ref, g_ref, o_ref, *, head_axis):
    if head_axis:
        tq = q_ref.shape[0]
        q4 = jnp.concatenate([_mem_q(q_ref, g_ref, h) for h in range(MEM_HEADS)], axis=0)
        nrow = MEM_TOKENS * MEM_HEADS
        s = _dot_nt(q4, mk_ref[0].reshape(nrow, MEM_HEAD_DIM).astype(bf16))
        rh = lax.broadcasted_iota(jnp.int32, s.shape, 0) // tq
        ch = lax.broadcasted_iota(jnp.int32, s.shape, 1) % MEM_HEADS
        s = jnp.where(rh == ch, s, NEG)
        m = jnp.max(s, axis=-1, keepdims=True)
        p = jnp.exp(s - m)
        l = jnp.sum(p, axis=-1, keepdims=True)
        o = _dot(p.astype(bf16), mv_ref[0].reshape(nrow, MEM_HEAD_DIM).astype(bf16)) / l
        o_ref[...] = jnp.concatenate([o[h * tq:(h + 1) * tq, :] for h in range(MEM_HEADS)], axis=1)
        return
    outs = []
    for h in range(MEM_HEADS):
        lo = h * MEM_HEAD_DIM
        s = _dot_nt(_mem_q(q_ref, g_ref, h), mk_ref[0, :, lo:lo + MEM_HEAD_DIM].astype(bf16))
        m = jnp.max(s, axis=-1, keepdims=True)
        p = jnp.exp(s - m)
        l = jnp.sum(p, axis=-1, keepdims=True)
        outs.append(_dot(p.astype(bf16), mv_ref[0, :, lo:lo + MEM_HEAD_DIM].astype(bf16)) / l)
    o_ref[...] = jnp.concatenate(outs, axis=1)


def mem_attend(proj, mk, mv, mq_gain_row, nseq, rows_per_seq, tq):
    w = MEM_HEADS * MEM_HEAD_DIM
    nq = rows_per_seq // tq
    head_axis = mk.ndim == 4
    if head_axis:
        kv_spec = pl.BlockSpec((1, MEM_TOKENS, MEM_HEADS, MEM_HEAD_DIM), lambda b, i: (b, 0, 0, 0))
    else:
        kv_spec = pl.BlockSpec((1, MEM_TOKENS, w), lambda b, i: (b, 0, 0))
    return pl.pallas_call(
        functools.partial(_mem_attend_kernel, head_axis=head_axis),
        grid=(nseq, nq),
        in_specs=[
            pl.BlockSpec((tq, w), lambda b, i: (b * nq + i, OFF_MQ // w)),
            kv_spec,
            kv_spec,
            pl.BlockSpec((1, MEM_HEAD_DIM), lambda b, i: (0, 0)),
        ],
        out_specs=pl.BlockSpec((tq, w), lambda b, i: (b * nq + i, 0)),
        out_shape=jax.ShapeDtypeStruct((nseq * rows_per_seq, w), f32),
        compiler_params=_cparams(("arbitrary", "arbitrary")),
        name="mem_attend",
    )(proj, mk, mv, mq_gain_row)


def _merge_kernel(x_ref, a_ref, s_ref, m_ref, ga_ref, gs_ref, gm_ref, wa_ref, ws_ref, wm_ref, wo_ref, o_ref):
    mixed = _sigmoid(ga_ref[...]) * _dot(a_ref[...].astype(bf16), wa_ref[...])
    mixed = mixed + _sigmoid(gs_ref[...]) * _dot(s_ref[...].astype(bf16), ws_ref[...])
    mixed = mixed + _sigmoid(gm_ref[...]) * _dot(m_ref[...].astype(bf16), wm_ref[...])
    o_ref[...] = x_ref[...] + _dot(mixed.astype(bf16), wo_ref[...])


def merge(x, attn_o, ssm_o, mem_o, proj, wa, ws, wm, wo, tm):
    t, d = x.shape
    row = lambda i: (i, 0)
    const = lambda i: (0, 0)
    gate_spec = lambda k: pl.BlockSpec((tm, d), lambda i: (i, OFF_GATES // d + k))
    w_spec = pl.BlockSpec((d, d), const)
    return pl.pallas_call(
        _merge_kernel,
        grid=(t // tm,),
        in_specs=[pl.BlockSpec((tm, d), row)] * 4 + [gate_spec(0), gate_spec(1), gate_spec(2)] + [w_spec] * 4,
        out_specs=pl.BlockSpec((tm, d), row),
        out_shape=jax.ShapeDtypeStruct((t, d), f32),
        compiler_params=_cparams(("arbitrary",)),
        name="merge",
    )(x, attn_o, ssm_o, mem_o, proj, proj, proj, wa, ws, wm, wo)


def _ffn_kernel(x_ref, g_ref, wg_ref, wu_ref, wd_ref, o_ref):
    x = x_ref[...]
    ms = jnp.mean(x * x, axis=-1, keepdims=True)
    h = (x * lax.rsqrt(ms + EPS) * g_ref[...]).astype(bf16)
    act = _silu(_dot(h, wg_ref[...])) * _dot(h, wu_ref[...])
    o_ref[...] = x + _dot(act.astype(bf16), wd_ref[...])


def ffn(x, gain_row, wg, wu, wd, tm):
    t, d = x.shape
    dff = wg.shape[1]
    const = lambda i: (0, 0)
    single = pl.Buffered(1)
    return pl.pallas_call(
        _ffn_kernel,
        grid=(t // tm,),
        in_specs=[
            pl.BlockSpec((tm, d), lambda i: (i, 0)),
            pl.BlockSpec((1, d), const),
            pl.BlockSpec((d, dff), const, pipeline_mode=single),
            pl.BlockSpec((d, dff), const, pipeline_mode=single),
            pl.BlockSpec((dff, d), const, pipeline_mode=single),
        ],
        out_specs=pl.BlockSpec((tm, d), lambda i: (i, 0)),
        out_shape=jax.ShapeDtypeStruct((t, d), f32),
        compiler_params=_cparams(("arbitrary",)),
        name="ffn",
    )(x, gain_row, wg, wu, wd)


def _top3_cols(gate, jb):
    sel = jnp.zeros(gate.shape, jnp.bool_)
    for _ in range(MOBA_TOPK):
        m = jnp.max(gate, axis=-1, keepdims=True)
        idx = jnp.min(jnp.where(gate == m, jb, gate.shape[1]), axis=-1, keepdims=True)
        hit = jb == idx
        sel = jnp.logical_or(sel, jnp.logical_and(hit, m > -jnp.inf))
        gate = jnp.where(hit, -jnp.inf, gate)
    return sel


def _moba_sample_kernel(pt_ref, q_ref, kn_ref, vn_ref, qg_ref, sl_ref, ck_hbm, cv_hbm, o_ref,
                        kbuf, vbuf, s_ref, sem, *, n_pages):
    b = pl.program_id(0)
    nseq = pl.num_programs(0)
    kvw = N_KV_HEADS * HEAD_DIM
    half = SAMPLE_ROWS // 2
    nrow = Q_PER_KV * N_KV_HEADS * half
    nblk = n_pages * PAGE_SIZE // MOBA_BLOCK
    ppb = MOBA_BLOCK // PAGE_SIZE
    past = n_pages * PAGE_SIZE
    slot = b % 2

    def k_copy(seq, sl, p):
        return pltpu.make_async_copy(ck_hbm.at[pt_ref[seq, p]], kbuf.at[sl, p], sem.at[0, sl])

    def v_copy(seq, sl, p):
        return pltpu.make_async_copy(cv_hbm.at[pt_ref[seq, p]], vbuf.at[sl, p], sem.at[1, sl])

    def start_all(seq, sl):
        for p in range(n_pages):
            k_copy(seq, sl, p).start()
        for p in range(n_pages):
            v_copy(seq, sl, p).start()

    @pl.when(b == 0)
    def _():
        start_all(0, 0)

    nxt = jnp.minimum(b + 1, nseq - 1)

    qb = q_ref[...]
    lane_g = lax.broadcasted_iota(jnp.int32, (SAMPLE_ROWS, kvw), 1) // HEAD_DIM
    low_half = lax.broadcasted_iota(jnp.int32, (SAMPLE_ROWS, kvw), 0) < half
    pieces = []
    for r in range(Q_PER_KV):
        slab = qb[:, r * kvw:(r + 1) * kvw]
        for g in range(N_KV_HEADS):
            pieces.append(jnp.where(lane_g == g, slab, 0.0))
    tiles = [jnp.where(low_half, pieces[2 * k], pltpu.roll(pieces[2 * k + 1], half, axis=0))
             for k in range(len(pieces) // 2)]
    qbd = jnp.concatenate(tiles, axis=0)
    ms = jnp.sum(qbd * qbd, axis=-1, keepdims=True) * (1.0 / HEAD_DIM)
    qn = qbd * lax.rsqrt(ms + EPS) * qg_ref[...]
    qs = (qn * (HEAD_DIM ** -0.5 * LOG2E)).astype(bf16)

    for p in range(n_pages):
        k_copy(b, slot, p).wait()

    def k_page(p):
        return kbuf[slot, p].reshape(kvw, PAGE_SIZE)

    lane = lax.broadcasted_iota(jnp.int32, (kvw, LANES), 1)
    kmean_t = jnp.zeros((kvw, LANES), f32)
    for j in range(nblk):
        blk = k_page(j * ppb)
        for pp in range(1, ppb):
            blk = blk + k_page(j * ppb + pp)
        kmean_t = kmean_t + jnp.where(lane == j, jnp.sum(blk, axis=-1, keepdims=True) * (1.0 / MOBA_BLOCK), 0.0)
    kh, kmid, kl = _split3(kmean_t)
    qh, qmid, ql = _split3(qn)
    gate = _dot(qh, kh) + _dot(qh, kmid) + _dot(qmid, kh) + _dot(qh, kl) + _dot(ql, kh) + _dot(qmid, kmid)
    jb = lax.broadcasted_iota(jnp.int32, (nrow, LANES), 1)
    gate = jnp.where(jb < nblk, gate, -jnp.inf)
    pen = jnp.where(_top3_cols(gate, jb), 0.0, NEG)

    slope = sl_ref[...]
    kk = lax.broadcasted_iota(jnp.int32, (nrow, PAGE_SIZE), 1)
    pages_per_dot = 4
    for p0 in range(0, n_pages, pages_per_dot):
        for p in range(p0, p0 + pages_per_dot):
            k_copy(nxt, 1 - slot, p).start()
        k4 = jnp.concatenate([k_page(p).astype(bf16) for p in range(p0, p0 + pages_per_dot)], axis=1)
        s4 = _dot(qs, k4)
        for i in range(pages_per_dot):
            p = p0 + i
            j = p // ppb
            rel = (kk + (p * PAGE_SIZE - past)).astype(f32)
            s_ref[:, p * PAGE_SIZE:(p + 1) * PAGE_SIZE] = (
                s4[:, i * PAGE_SIZE:(i + 1) * PAGE_SIZE] + slope * rel + pen[:, j:j + 1])

    zpad = jnp.zeros((LANES - SAMPLE_ROWS, kvw), f32)
    knew = jnp.concatenate([kn_ref[...], zpad], axis=0).astype(bf16)
    vnew = jnp.concatenate([vn_ref[...], zpad], axis=0).astype(bf16)
    kn_lane = lax.broadcasted_iota(jnp.int32, (nrow, LANES), 1)
    tok = lax.broadcasted_iota(jnp.int32, (nrow, LANES), 0) % half
    s_own = _dot_nt(qs, knew) + slope * kn_lane.astype(f32)
    s_ref[:, past:past + LANES] = jnp.where(kn_lane <= tok, s_own, NEG)

    m = jnp.max(s_ref[...], axis=-1, keepdims=True)
    p_own = jnp.exp2(s_ref[:, past:past + LANES] - m)
    n_acc = 4
    psums = [p_own] + [None] * (n_acc - 1)
    accs = [_dot(p_own.astype(bf16), vnew)] + [None] * (n_acc - 1)

    for p in range(n_pages):
        v_copy(b, slot, p).wait()

    for p0 in range(0, n_pages, ppb):
        for p in range(p0, p0 + ppb):
            v_copy(nxt, 1 - slot, p).start()
        pp = jnp.exp2(s_ref[:, p0 * PAGE_SIZE:(p0 + ppb) * PAGE_SIZE] - m)
        vb = jnp.concatenate([vbuf[slot, p].reshape(kvw, PAGE_SIZE).astype(bf16) for p in range(p0, p0 + ppb)], axis=1)
        pv = _dot_nt(pp.astype(bf16), vb)
        a = (p0 // ppb) % n_acc
        pl_sum = pp[:, :PAGE_SIZE]
        for i in range(1, ppb):
            pl_sum = pl_sum + pp[:, i * PAGE_SIZE:(i + 1) * PAGE_SIZE]
        psums[a] = pl_sum if psums[a] is None else psums[a] + pl_sum
        accs[a] = pv if accs[a] is None else accs[a] + pv
    acc = (accs[0] + accs[1]) + (accs[2] + accs[3])
    psum = (psums[0] + psums[1]) + (psums[2] + psums[3])
    o = acc / jnp.sum(psum, axis=-1, keepdims=True)

    @pl.when(b == nseq - 1)
    def _():
        for p in range(n_pages):
            k_copy(nxt, 1 - slot, p).wait()
        for p in range(n_pages):
            v_copy(nxt, 1 - slot, p).wait()

    slabs = []
    for r in range(Q_PER_KV):
        slab = jnp.zeros((SAMPLE_ROWS, kvw), f32)
        for g in range(N_KV_HEADS):
            piece = r * N_KV_HEADS + g
            tile = o[(piece // 2) * SAMPLE_ROWS:(piece // 2 + 1) * SAMPLE_ROWS, :]
            if piece % 2:
                tile = pltpu.roll(tile, half, axis=0)
            slab = slab + jnp.where(lane_g == g, tile, 0.0)
        slabs.append(slab)
    o_ref[...] = jnp.concatenate(slabs, axis=1)


def moba_sample(page_table, proj, kn_new, ck, cv, q_gain_row, slope_rows, nseq):
    kvw = N_KV_HEADS * HEAD_DIM
    qw = N_HEADS * HEAD_DIM
    n_pages = page_table.shape[1]
    nrow = Q_PER_KV * N_KV_HEADS * (SAMPLE_ROWS // 2)
    past = n_pages * PAGE_SIZE
    grid_spec = pltpu.PrefetchScalarGridSpec(
        num_scalar_prefetch=1,
        grid=(nseq,),
        in_specs=[
            pl.BlockSpec((SAMPLE_ROWS, qw), lambda b, pt: (b, OFF_Q // qw)),
            pl.BlockSpec((SAMPLE_ROWS, kvw), lambda b, pt: (b, 0)),
            pl.BlockSpec((SAMPLE_ROWS, kvw), lambda b, pt: (b, OFF_V // kvw)),
            pl.BlockSpec((1, kvw), lambda b, pt: (0, 0)),
            pl.BlockSpec((nrow, LANES), lambda b, pt: (0, 0)),
            pl.BlockSpec(memory_space=pl.ANY),
            pl.BlockSpec(memory_space=pl.ANY),
        ],
        out_specs=pl.BlockSpec((SAMPLE_ROWS, qw), lambda b, pt: (b, 0)),
        scratch_shapes=[
            pltpu.VMEM((2, n_pages, N_KV_HEADS, HEAD_DIM, PAGE_SIZE), f32),
            pltpu.VMEM((2, n_pages, N_KV_HEADS, HEAD_DIM, PAGE_SIZE), f32),
            pltpu.VMEM((nrow, past + LANES), f32),
            pltpu.SemaphoreType.DMA((2, 2)),
        ],
    )
    return pl.pallas_call(
        functools.partial(_moba_sample_kernel, n_pages=n_pages),
        grid_spec=grid_spec,
        out_shape=jax.ShapeDtypeStruct((nseq * SAMPLE_ROWS, qw), f32),
        compiler_params=_cparams(("arbitrary",)),
        name="moba_sample",
    )(page_table, proj, kn_new, proj, q_gain_row, slope_rows, ck, cv)


def _ssd_sample_kernel(ext_ref, z_ref, dt_ref, h0_ref, cw_ref, cb_ref, dtb_ref, alog_ref, dexp_ref, norm_ref,
                       y_ref, h_ref, scr_ref, *, nseq_step, n_new):
    rows = nseq_step * SAMPLE_ROWS
    scr_ref[pl.ds(0, rows), :] = ext_ref[...]
    scr_ref[pl.ds(rows, SUBLANES), :] = jnp.zeros((SUBLANES, CONV_DIM), f32)
    act = _ssd_conv_act(scr_ref, rows, cw_ref, cb_ref, base=0)

    ri = lax.broadcasted_iota(jnp.int32, (rows, rows), 0)
    ci = lax.broadcasted_iota(jnp.int32, (rows, rows), 1)
    allowed = jnp.logical_and(ri // SAMPLE_ROWS == ci // SAMPLE_ROWS,
                              jnp.logical_and(ci % SAMPLE_ROWS <= ri % SAMPLE_ROWS, ci % SAMPLE_ROWS < n_new))
    xs, bm, cm, dt_e, cs_e, xdt, y = _ssd_chunk(act, dt_ref[...], allowed, dtb_ref, alog_ref)

    valid = lax.broadcasted_iota(jnp.int32, (SAMPLE_ROWS, 1), 0) < n_new
    ecs = jnp.exp(cs_e)
    hw = D_INNER // SSM_GROUPS
    hpg = SSM_HEADS // SSM_GROUPS
    yoff_rows = []
    for i in range(nseq_step):
        lo = i * SAMPLE_ROWS
        cs_i = cs_e[lo:lo + SAMPLE_ROWS, :]
        cs_last = cs_e[lo + n_new - 1:lo + n_new, :]
        xdte = jnp.where(valid, xdt[lo:lo + SAMPLE_ROWS, :] * jnp.exp(cs_last - cs_i), 0.0)
        yg = []
        upd = []
        for g in range(SSM_GROUPS):
            h0g = h0_ref[i, g * hpg:(g + 1) * hpg].reshape(hw, D_STATE)
            yg.append(_dot_nt(cm[lo:lo + SAMPLE_ROWS, g * D_STATE:(g + 1) * D_STATE], h0g))
            upd.append(_dot_tn(xdte[:, g * hw:(g + 1) * hw], bm[lo:lo + SAMPLE_ROWS, g * D_STATE:(g + 1) * D_STATE]))
        yoff_rows.append(jnp.concatenate(yg, axis=1))
        dec_col = jnp.broadcast_to(jnp.exp(cs_last), (SUBLANES, D_INNER)).T[:, 0:1]
        h_new = h0_ref[i].reshape(D_INNER, D_STATE) * dec_col + jnp.concatenate(upd, axis=0)
        h_ref[i] = h_new.reshape(SSM_HEADS, SSM_HEAD_DIM, D_STATE)
    y = y + jnp.concatenate(yoff_rows, axis=0) * ecs
    y_ref[...] = _ssd_finish(y, xs, z_ref[...], dexp_ref, norm_ref)


def ssd_sample(ext, proj, h0, conv_w, conv_b_row, dtb_row, alog_row, dexp_row, norm_row, nseq, nseq_step, n_new):
    rows = nseq_step * SAMPLE_ROWS
    const = lambda i: (0, 0)
    state_spec = pl.BlockSpec((nseq_step, SSM_HEADS, SSM_HEAD_DIM, D_STATE), lambda i: (i, 0, 0, 0))
    return pl.pallas_call(
        functools.partial(_ssd_sample_kernel, nseq_step=nseq_step, n_new=n_new),
        grid=(nseq // nseq_step,),
        in_specs=[
            pl.BlockSpec((rows, CONV_DIM), lambda i: (i, 0)),
            pl.BlockSpec((rows, D_INNER), lambda i: (i, OFF_Z // D_INNER)),
            pl.BlockSpec((rows, LANES), lambda i: (i, OFF_DT // LANES)),
            state_spec,
            pl.BlockSpec((CONV_WIDTH, CONV_DIM), const),
            pl.BlockSpec((1, CONV_DIM), const),
            pl.BlockSpec((1, LANES), const),
            pl.BlockSpec((1, LANES), const),
            pl.BlockSpec((1, D_INNER), const),
            pl.BlockSpec((1, D_INNER), const),
        ],
        out_specs=[pl.BlockSpec((rows, D_INNER), lambda i: (i, 0)), state_spec],
        out_shape=[
            jax.ShapeDtypeStruct((nseq * SAMPLE_ROWS, D_INNER), f32),
            jax.ShapeDtypeStruct(h0.shape, f32),
        ],
        scratch_shapes=[pltpu.VMEM((rows + SUBLANES, CONV_DIM), f32)],
        compiler_params=_cparams(("arbitrary",)),
        name="ssd_sample",
    )(ext, proj, proj, h0, conv_w, conv_b_row, dtb_row, alog_row, dexp_row, norm_row)


IN_SIZES = (N_HEADS * HEAD_DIM, N_KV_HEADS * HEAD_DIM, N_KV_HEADS * HEAD_DIM, D_INNER, CONV_DIM, SSM_HEADS,
            MEM_HEADS * MEM_HEAD_DIM, 3 * D_MODEL)
TM = 512
TN_PROJ = 2944
TQ = 256
SSD_ROWS = 256
SSD_SAMPLE_SEQS = 8


def _pad_lanes(v):
    return jnp.pad(v, (0, LANES - v.shape[0])).reshape(1, LANES)


def kernel(x_prompt, x_sample, mem_prompt, cache_k, cache_v, page_table, state_conv, state_ssm, cache_mem_k, cache_mem_v, norm_mix, w_in, q_norm, k_norm, conv_w, conv_b, dt_bias, a_log, d_skip, ssm_norm, mem_norm, w_mem_kv, mq_norm, mk_norm, w_attn_br, w_ssm_br, w_mem_br, w_out, norm_ffn, w_gate, w_up, w_down):
    assert w_in.shape[0] == 1, "single layer"
    assert x_sample.shape[1] <= SAMPLE_ROWS // 2, "sample MoBA packs two heads' tokens into one row tile"
    batch, seq, d = x_prompt.shape
    nseq, n_new, _ = x_sample.shape
    kvw = N_KV_HEADS * HEAD_DIM
    qw = N_HEADS * HEAD_DIM

    offs = np.cumsum(IN_SIZES)[:-1].tolist()
    wq, wk, wv, wz, wxbc, wdt, wmq, wgates = jnp.split(w_in[0], offs, axis=1)
    wdt = jnp.pad(wdt, ((0, 0), (0, LANES - SSM_HEADS)))
    tail = [wz, wxbc, wmq, wgates, wk, wv, wdt]
    w_prompt = jnp.concatenate([wq] + tail, axis=1).astype(bf16)
    wq_rgd = wq.reshape(d, N_KV_HEADS, Q_PER_KV, HEAD_DIM).transpose(0, 2, 1, 3).reshape(d, qw)
    w_sample = jnp.concatenate([wq_rgd] + tail, axis=1).astype(bf16)
    wa = w_attn_br[0].astype(bf16)
    wa_rgd = w_attn_br[0].reshape(N_KV_HEADS, Q_PER_KV, HEAD_DIM, d).transpose(1, 0, 2, 3).reshape(qw, d).astype(bf16)
    ws, wm, wo = w_ssm_br[0].astype(bf16), w_mem_br[0].astype(bf16), w_out[0].astype(bf16)
    wg, wu, wd = w_gate[0].astype(bf16), w_up[0].astype(bf16), w_down[0].astype(bf16)

    k_gain = jnp.tile(k_norm[0], N_KV_HEADS).reshape(1, kvw)
    conv_b_row = conv_b[0].reshape(1, CONV_DIM)
    dtb_row, alog_row = _pad_lanes(dt_bias[0]), _pad_lanes(a_log[0])
    dexp_row = jnp.repeat(d_skip[0], SSM_HEAD_DIM).reshape(1, D_INNER)
    ssm_norm_row = ssm_norm[0].reshape(1, D_INNER)
    mq_gain = mq_norm[0].reshape(1, MEM_HEAD_DIM)
    ffn_gain = norm_ffn[0].reshape(1, d)
    slopes2 = jnp.exp2(-8.0 * jnp.arange(1, N_HEADS + 1, dtype=f32) / N_HEADS) * LOG2E

    xp = x_prompt.reshape(batch * seq, d)
    proj = inproj(xp, norm_mix[0], w_prompt, TM, TN_PROJ)
    kn, kb, vt, kmean = kprep(proj, k_gain, batch, seq, 2048)
    kmean_g = kmean.reshape(batch, seq // MOBA_BLOCK, N_KV_HEADS, HEAD_DIM).transpose(0, 2, 1, 3)
    slope_cols = jnp.repeat(slopes2.reshape(N_KV_HEADS, Q_PER_KV), TQ, axis=1).reshape(N_KV_HEADS, 1, Q_PER_KV * TQ)
    q_gain_col = jnp.broadcast_to(q_norm[0][:, None], (HEAD_DIM, TQ))
    attn_o = moba_prompt(proj, kb, vt, kmean_g, q_gain_col, slope_cols, batch, seq, TQ)
    ssm_o, h_prompt = ssd_prompt(proj, conv_w[0], conv_b_row, dtb_row, alog_row, dexp_row, ssm_norm_row,
                                 batch, seq, SSD_ROWS)
    mem_kv = inproj(mem_prompt.reshape(batch * MEM_TOKENS, d), mem_norm[0], w_mem_kv[0].astype(bf16), TM, qw)
    mk = headnorm(mem_kv, 0, qw, jnp.tile(mk_norm[0], MEM_HEADS).reshape(1, qw), MEM_HEAD_DIM, TM)
    mv = mem_kv[:, qw:]
    mem_o = mem_attend(proj, mk.reshape(batch, MEM_TOKENS, qw), mv.reshape(batch, MEM_TOKENS, qw), mq_gain,
                       batch, seq, TM)
    x1 = merge(xp, attn_o, ssm_o, mem_o, proj, wa, ws, wm, wo, TM)
    y_prompt = ffn(x1, ffn_gain, wg, wu, wd, TM).reshape(batch, seq, d)

    k_prompt = kn.reshape(1, batch, seq, N_KV_HEADS, HEAD_DIM)
    v_prompt = proj[:, OFF_V:OFF_V + kvw].reshape(1, batch, seq, N_KV_HEADS, HEAD_DIM)
    conv_prompt = proj.reshape(batch, seq, N_PROJ)[:, seq - (CONV_WIDTH - 1):, OFF_XBC:OFF_XBC + CONV_DIM][None]
    mem_k_prompt = mk.reshape(1, batch, MEM_TOKENS, MEM_HEADS, MEM_HEAD_DIM)
    mem_v_prompt = mv.reshape(1, batch, MEM_TOKENS, MEM_HEADS, MEM_HEAD_DIM)

    xs = jnp.pad(x_sample, ((0, 0), (0, SAMPLE_ROWS - n_new), (0, 0))).reshape(nseq * SAMPLE_ROWS, d)
    proj_s = inproj(xs, norm_mix[0], w_sample, TM, TN_PROJ)
    kn_s = headnorm(proj_s, OFF_K // kvw, kvw, k_gain, HEAD_DIM, TM)
    slope_rows = jnp.broadcast_to(
        jnp.repeat(slopes2.reshape(N_KV_HEADS, Q_PER_KV).T.reshape(-1), SAMPLE_ROWS // 2)[:, None],
        (N_HEADS * (SAMPLE_ROWS // 2), LANES))
    ck_t = jnp.transpose(cache_k[0], (0, 2, 3, 1))
    cv_t = jnp.transpose(cache_v[0], (0, 2, 3, 1))
    attn_s = moba_sample(page_table, proj_s, kn_s, ck_t, cv_t, jnp.tile(q_norm[0], N_KV_HEADS).reshape(1, kvw),
                         slope_rows, nseq)
    xbc_new = proj_s[:, OFF_XBC:OFF_XBC + CONV_DIM].reshape(nseq, SAMPLE_ROWS, CONV_DIM)[:, :n_new]
    ext = jnp.concatenate(
        [state_conv[0], xbc_new, jnp.zeros((nseq, SAMPLE_ROWS - (CONV_WIDTH - 1) - n_new, CONV_DIM), f32)], axis=1)
    ssm_s, h_sample = ssd_sample(ext.reshape(nseq * SAMPLE_ROWS, CONV_DIM), proj_s, state_ssm[0], conv_w[0], conv_b_row,
                                 dtb_row, alog_row, dexp_row, ssm_norm_row, nseq, SSD_SAMPLE_SEQS, n_new)
    mem_s = mem_attend(proj_s, cache_mem_k[0], cache_mem_v[0], mq_gain, nseq, SAMPLE_ROWS, SAMPLE_ROWS)
    x1s = merge(xs, attn_s, ssm_s, mem_s, proj_s, wa_rgd, ws, wm, wo, TM)
    y_sample = ffn(x1s, ffn_gain, wg, wu, wd, TM).reshape(nseq, SAMPLE_ROWS, d)[:, :n_new]

    k_sample = kn_s.reshape(nseq, SAMPLE_ROWS, N_KV_HEADS, HEAD_DIM)[:, :n_new][None]
    v_sample = proj_s[:, OFF_V:OFF_V + kvw].reshape(nseq, SAMPLE_ROWS, N_KV_HEADS, HEAD_DIM)[:, :n_new][None]
    conv_sample = ext[:, n_new:n_new + CONV_WIDTH - 1][None]

    return (y_prompt, y_sample, k_prompt, v_prompt, conv_prompt, h_prompt[None], mem_k_prompt, mem_v_prompt,
            k_sample, v_sample, conv_sample, h_sample[None])
```

```python
import functools
import math

import jax
import jax.numpy as jnp
import numpy as np
from jax import lax
from jax.experimental import pallas as pl
from jax.experimental.pallas import tpu as pltpu

f32 = jnp.float32
bf16 = jnp.bfloat16

D_MODEL = 1024
N_HEADS = 16
N_KV_HEADS = 4
HEAD_DIM = 64
Q_PER_KV = N_HEADS // N_KV_HEADS
MOBA_BLOCK = 256
MOBA_TOPK = 3
SSM_HEADS = 16
SSM_HEAD_DIM = 64
D_INNER = SSM_HEADS * SSM_HEAD_DIM
SSM_GROUPS = 4
D_STATE = 128
CONV_WIDTH = 4
CONV_DIM = D_INNER + 2 * SSM_GROUPS * D_STATE
MEM_TOKENS = 256
MEM_HEADS = 4
MEM_HEAD_DIM = 256
D_FF = 2816
EPS = 1e-6
PAGE_SIZE = 128

LANES = 128
SUBLANES = 8
LOG2E = 1.4426950408889634
NEG = -1e30
VMEM_LIMIT = 56 * 1024 * 1024

OFF_Q, OFF_Z, OFF_XBC, OFF_MQ, OFF_GATES, OFF_K, OFF_V, OFF_DT = 0, 1024, 2048, 4096, 5120, 8192, 8448, 8704
N_PROJ = 8832
SAMPLE_ROWS = 8
MOBA_GROUP = 4


def _cparams(sem):
    return pltpu.CompilerParams(dimension_semantics=sem, vmem_limit_bytes=VMEM_LIMIT)


def _split3(x):
    h = x.astype(bf16)
    r = x - h.astype(f32)
    m = r.astype(bf16)
    l = (r - m.astype(f32)).astype(bf16)
    return h, m, l


def _dot(a, b):
    return jnp.dot(a, b, preferred_element_type=f32)


def _dot_nt(a, b):
    return lax.dot_general(a, b, (((1,), (1,)), ((), ())), preferred_element_type=f32)


def _dot_tn(a, b):
    return lax.dot_general(a, b, (((0,), (0,)), ((), ())), preferred_element_type=f32)


def _dot_exact_rhs(x, m_bf16):
    h, m, l = _split3(x)
    return _dot(h, m_bf16) + _dot(m, m_bf16) + _dot(l, m_bf16)


def _dot_exact_lhs(m_bf16, x):
    h, m, l = _split3(x)
    return _dot(m_bf16, h) + _dot(m_bf16, m) + _dot(m_bf16, l)


def _silu(x):
    return x * (1.0 / (1.0 + jnp.exp(-x)))


def _sigmoid(x):
    return 1.0 / (1.0 + jnp.exp(-x))


def _softplus(x):
    return jnp.maximum(x, 0.0) + jnp.log(1.0 + jnp.exp(-jnp.abs(x)))


def _inproj_kernel(x_ref, g_ref, w_ref, o_ref):
    x = x_ref[...]
    ms = jnp.mean(x * x, axis=-1, keepdims=True)
    xn = (x * lax.rsqrt(ms + EPS) * g_ref[...]).astype(bf16)
    o_ref[...] = _dot(xn, w_ref[...])


def inproj(x, gain, w, tm, tn):
    t, d = x.shape
    n = w.shape[1]
    return pl.pallas_call(
        _inproj_kernel,
        grid=(n // tn, t // tm),
        in_specs=[
            pl.BlockSpec((tm, d), lambda j, i: (i, 0)),
            pl.BlockSpec((1, d), lambda j, i: (0, 0)),
            pl.BlockSpec((d, tn), lambda j, i: (0, j)),
        ],
        out_specs=pl.BlockSpec((tm, tn), lambda j, i: (i, j)),
        out_shape=jax.ShapeDtypeStruct((t, n), f32),
        compiler_params=_cparams(("arbitrary", "arbitrary")),
        name="inproj",
    )(x, gain.reshape(1, d), w)


def _seg_ones(width, seg):
    r = lax.broadcasted_iota(jnp.int32, (width, width), 0) // seg
    c = lax.broadcasted_iota(jnp.int32, (width, width), 1) // seg
    return jnp.where(r == c, 1.0, 0.0).astype(bf16)


def _seg_rmsnorm(x, gain_row, seg):
    ss = _dot_exact_rhs(x * x, _seg_ones(x.shape[1], seg))
    return x * lax.rsqrt(ss * (1.0 / seg) + EPS) * gain_row


def _headnorm_kernel(x_ref, g_ref, o_ref, *, seg):
    o_ref[...] = _seg_rmsnorm(x_ref[...], g_ref[...], seg)


def headnorm(x, col_block, width, gain_row, seg, tm):
    t = x.shape[0]
    return pl.pallas_call(
        functools.partial(_headnorm_kernel, seg=seg),
        grid=(t // tm,),
        in_specs=[
            pl.BlockSpec((tm, width), lambda i: (i, col_block)),
            pl.BlockSpec((1, width), lambda i: (0, 0)),
        ],
        out_specs=pl.BlockSpec((tm, width), lambda i: (i, 0)),
        out_shape=jax.ShapeDtypeStruct((t, width), f32),
        compiler_params=_cparams(("arbitrary",)),
        name="headnorm",
    )(x, gain_row)


def _kprep_kernel(k_ref, v_ref, g_ref, kn_ref, ka_ref, vt_ref, km_ref):
    kn = _seg_rmsnorm(k_ref[...], g_ref[...], HEAD_DIM)
    kn_ref[...] = kn
    kn_b = kn.astype(bf16)
    rows = kn.shape[0]
    kvw = kn.shape[1]
    r = lax.broadcasted_iota(jnp.int32, (kvw, LANES), 0)
    c = lax.broadcasted_iota(jnp.int32, (kvw, LANES), 1)
    lane = lax.broadcasted_iota(jnp.int32, (rows, LANES), 1)
    pos = (lax.broadcasted_iota(jnp.int32, (rows, LANES), 0) % MOBA_BLOCK).astype(f32)
    pos_lanes = jnp.where(jnp.logical_and(lane >= HEAD_DIM, lane < HEAD_DIM + 3), pos, 0.0)
    for g in range(N_KV_HEADS):
        pick = jnp.where(jnp.logical_and(r == c + g * HEAD_DIM, c < HEAD_DIM), 1.0, 0.0).astype(bf16)
        ka_ref[0, g] = (_dot(kn_b, pick) + pos_lanes).astype(bf16)
    nb = kn.shape[0] // MOBA_BLOCK
    km_ref[0] = jnp.sum(kn.reshape(nb, MOBA_BLOCK, kn.shape[1]), axis=1) * (1.0 / MOBA_BLOCK)
    vt_ref[0] = v_ref[...].astype(bf16).T


def kprep(proj, k_gain_row, batch, seq, tm):
    kvw = N_KV_HEADS * HEAD_DIM
    nt = seq // tm
    nb = tm // MOBA_BLOCK
    return pl.pallas_call(
        _kprep_kernel,
        grid=(batch, nt),
        in_specs=[
            pl.BlockSpec((tm, kvw), lambda b, i: (b * nt + i, OFF_K // kvw)),
            pl.BlockSpec((tm, kvw), lambda b, i: (b * nt + i, OFF_V // kvw)),
            pl.BlockSpec((1, kvw), lambda b, i: (0, 0)),
        ],
        out_specs=[
            pl.BlockSpec((tm, kvw), lambda b, i: (b * nt + i, 0)),
            pl.BlockSpec((1, N_KV_HEADS, tm, LANES), lambda b, i: (b, 0, i, 0)),
            pl.BlockSpec((1, kvw, tm), lambda b, i: (b, 0, i)),
            pl.BlockSpec((1, nb, kvw), lambda b, i: (b, i, 0)),
        ],
        out_shape=[
            jax.ShapeDtypeStruct((batch * seq, kvw), f32),
            jax.ShapeDtypeStruct((batch, N_KV_HEADS, seq, LANES), bf16),
            jax.ShapeDtypeStruct((batch, kvw, seq), bf16),
            jax.ShapeDtypeStruct((batch, seq // MOBA_BLOCK, kvw), f32),
        ],
        compiler_params=_cparams(("arbitrary", "arbitrary")),
        name="kprep",
    )(proj, proj, k_gain_row)


def _top3_rows(gate, jb):
    sel = jnp.zeros(gate.shape, jnp.bool_)
    for _ in range(MOBA_TOPK):
        m = jnp.max(gate, axis=0, keepdims=True)
        idx = jnp.min(jnp.where(gate == m, jb, gate.shape[0]), axis=0, keepdims=True)
        hit = jb == idx
        sel = jnp.logical_or(sel, jnp.logical_and(hit, m > -jnp.inf))
        gate = jnp.where(hit, -jnp.inf, gate)
    return sel


def _moba_prompt_kernel(q_ref, kb_ref, vt_ref, km_ref, qg_ref, sl_ref, o_ref,
                        qbd_ref, pen_ref, acc_ref, m_ref, s_ref, cm_ref, *, tq):
    qt = pl.program_id(2)
    nb_tile = tq // MOBA_BLOCK
    own = qt * nb_tile
    cols = Q_PER_KV * tq
    nblk = km_ref.shape[2]
    q_in_tile = lax.broadcasted_iota(jnp.int32, (1, cols), 1) % tq
    own_col = own + q_in_tile // MOBA_BLOCK

    qT = q_ref[...].T
    parts = []
    for r in range(Q_PER_KV):
        xr = qT[r * HEAD_DIM:(r + 1) * HEAD_DIM, :]
        ms = jnp.mean(xr * xr, axis=0, keepdims=True)
        parts.append(xr * lax.rsqrt(ms + EPS) * qg_ref[...])
    qn = jnp.concatenate(parts, axis=1)

    km = km_ref[0, 0]
    kh, kmid, kl = _split3(km)
    qh, qmid, ql = _split3(qn)
    gate = _dot(kh, qh) + _dot(kh, qmid) + _dot(kmid, qh) + _dot(kh, ql) + _dot(kl, qh) + _dot(kmid, qmid)
    jb = lax.broadcasted_iota(jnp.int32, (nblk, cols), 0)
    gate = jnp.where(jb < own_col, gate, -jnp.inf)
    sel = _top3_rows(gate, jb)
    pen_ref[...] = jnp.where(jnp.logical_or(sel, jb == own_col), 0.0, NEG)

    slope = sl_ref[0]
    s_hi, s_mid, s_lo = _split3(slope)
    rowi = lax.broadcasted_iota(jnp.int32, (HEAD_DIM, cols), 0)
    slope_rows = jnp.where(rowi == 0, s_hi.astype(f32), jnp.where(rowi == 1, s_mid.astype(f32),
                           jnp.where(rowi == 2, s_lo.astype(f32), 0.0)))
    qbd_ref[...] = jnp.concatenate([qn * (HEAD_DIM ** -0.5 * LOG2E), slope_rows], axis=0).astype(bf16)

    kk = lax.broadcasted_iota(jnp.int32, (MOBA_BLOCK, cols), 0)
    ones = jnp.ones((2 * SUBLANES, MOBA_BLOCK), bf16)

    def blk(j):
        return pl.ds(pl.multiple_of(j * MOBA_BLOCK, MOBA_BLOCK), MOBA_BLOCK)

    def v_aug(j):
        return jnp.concatenate([vt_ref[0, :, blk(j)], ones], axis=0)

    def shift_of(j):
        return slope * ((j - own) * MOBA_BLOCK).astype(f32) + pen_ref[pl.ds(j, 1), :]

    def stage_scores(j, buf):
        s = _dot(kb_ref[0, 0, blk(j), :], qbd_ref[...])
        s_ref[buf] = s
        cm_ref[buf] = jnp.max(s, axis=0, keepdims=True)

    def stage_accum(j, buf, valid=None):
        shift = shift_of(j)
        if valid is not None:
            shift = shift + jnp.where(valid, 0.0, NEG)
        m_old = m_ref[...]
        m_new = jnp.maximum(m_old, cm_ref[buf] + shift)
        p = jnp.exp2(s_ref[buf] - (m_new - shift)).astype(bf16)
        acc_ref[...] = acc_ref[...] * jnp.exp2(m_old - m_new) + _dot(v_aug(j), p)
        m_ref[...] = m_new

    last = jnp.maximum(own - 1, 0)
    stage_scores(0, 1)
    col_blk = q_in_tile // MOBA_BLOCK
    causal = kk <= q_in_tile % MOBA_BLOCK
    ss, shifts = [], []
    for t in range(nb_tile):
        s = _dot(kb_ref[0, 0, blk(own + t), :], qbd_ref[...])
        visible = jnp.logical_or(col_blk > t, jnp.logical_and(col_blk == t, causal))
        ss.append(jnp.where(visible, s, NEG))
        shifts.append(shift_of(own + t))
    m0 = None
    for s, sh in zip(ss, shifts):
        mj = jnp.max(s, axis=0, keepdims=True) + sh
        m0 = mj if m0 is None else jnp.maximum(m0, mj)
    pv0 = None
    for t, (s, sh) in enumerate(zip(ss, shifts)):
        pv = _dot(v_aug(own + t), jnp.exp2(s - (m0 - sh)).astype(bf16))
        pv0 = pv if pv0 is None else pv0 + pv
    acc_ref[...] = pv0
    m_ref[...] = m0


    def pair(j0):
        j1, j2 = j0 + 1, j0 + 2
        stage_scores(jnp.minimum(j1, last), 0)
        stage_accum(j0, 1)
        stage_scores(jnp.minimum(j2, last), 1)
        stage_accum(jnp.minimum(j1, last), 0, valid=j1 < own)

    def body8(i, c):
        for k in range(4):
            pair(8 * i + 2 * k)
        return c

    def body4(i, c):
        pair(8 * n8)
        pair(8 * n8 + 2)
        return c

    def body2(i, c):
        pair(8 * n8 + 4 * n4 + 2 * i)
        return c

    n8 = own // 8
    n4 = (own - 8 * n8) // 4
    lax.fori_loop(0, n8, body8, 0)
    lax.fori_loop(0, n4, body4, 0)
    lax.fori_loop(0, (own - 8 * n8 - 4 * n4 + 1) // 2, body2, 0)

    acc = acc_ref[...]
    o = acc[0:HEAD_DIM, :] / acc[HEAD_DIM:HEAD_DIM + 1, :]
    oT = jnp.concatenate([o[:, r * tq:(r + 1) * tq] for r in range(Q_PER_KV)], axis=0)
    o_ref[...] = oT.T


def moba_prompt(proj, kb, vt, kmean_g, q_gain_col, slopes2, batch, seq, tq):
    nq = seq // tq
    qw = Q_PER_KV * HEAD_DIM
    kvw = N_KV_HEADS * HEAD_DIM
    nblk = seq // MOBA_BLOCK
    cols = Q_PER_KV * tq
    return pl.pallas_call(
        functools.partial(_moba_prompt_kernel, tq=tq),
        grid=(batch, N_KV_HEADS, nq),
        in_specs=[
            pl.BlockSpec((tq, qw), lambda b, g, i: (b * nq + i, OFF_Q // qw + g)),
            pl.BlockSpec((1, 1, seq, LANES), lambda b, g, i: (b, g, 0, 0)),
            pl.BlockSpec((1, HEAD_DIM, seq), lambda b, g, i: (b, g, 0)),
            pl.BlockSpec((1, 1, nblk, HEAD_DIM), lambda b, g, i: (b, g, 0, 0)),
            pl.BlockSpec((HEAD_DIM, tq), lambda b, g, i: (0, 0)),
            pl.BlockSpec((1, 1, cols), lambda b, g, i: (g, 0, 0)),
        ],
        out_specs=pl.BlockSpec((tq, qw), lambda b, g, i: (b * nq + i, g)),
        out_shape=jax.ShapeDtypeStruct((batch * seq, N_HEADS * HEAD_DIM), f32),
        scratch_shapes=[
            pltpu.VMEM((LANES, cols), bf16),
            pltpu.VMEM((nblk, cols), f32),
            pltpu.VMEM((HEAD_DIM + 2 * SUBLANES, cols), f32),
            pltpu.VMEM((1, cols), f32),
            pltpu.VMEM((2, MOBA_BLOCK, cols), f32),
            pltpu.VMEM((2, 1, cols), f32),
        ],
        compiler_params=_cparams(("arbitrary", "arbitrary", "arbitrary")),
        name="moba_prompt",
    )(proj, kb, vt, kmean_g, q_gain_col, slopes2)


def _head_expand():
    r = lax.broadcasted_iota(jnp.int32, (LANES, D_INNER), 0)
    c = lax.broadcasted_iota(jnp.int32, (LANES, D_INNER), 1) // SSM_HEAD_DIM
    return jnp.where(r == c, 1.0, 0.0).astype(bf16)


def _ssd_conv_act(ext_ref, rows, cw_ref, cb_ref, base):
    conv = cb_ref[...] + cw_ref[0:1, :] * ext_ref[pl.ds(base, rows), :]
    for i in range(1, CONV_WIDTH):
        conv = conv + cw_ref[i:i + 1, :] * ext_ref[pl.ds(base + i, rows), :]
    return _silu(conv)


def _ssd_chunk(act, dt_raw, allowed, dtb_ref, alog_ref):
    rows = act.shape[0]
    xs = act[:, :D_INNER]
    gn = SSM_GROUPS * D_STATE
    bm = act[:, D_INNER:D_INNER + gn]
    cm = act[:, D_INNER + gn:]
    dt = _softplus(dt_raw + dtb_ref[...])
    a = -jnp.exp(alog_ref[...])
    mask_b = jnp.where(allowed, 1.0, 0.0).astype(bf16)
    cs = _dot_exact_lhs(mask_b, dt * a)
    expand = _head_expand()
    dt_e = _dot_exact_rhs(dt, expand)
    cs_e = _dot_exact_rhs(cs, expand)
    xdt = xs * dt_e
    csT = cs.T
    xdt_b = xdt.astype(bf16)
    ypairs = []
    lane = lax.broadcasted_iota(jnp.int32, (rows, LANES), 1)
    for g in range(SSM_GROUPS):
        cb = _dot_nt(cm[:, g * D_STATE:(g + 1) * D_STATE].astype(bf16), bm[:, g * D_STATE:(g + 1) * D_STATE].astype(bf16))
        hpg = SSM_HEADS // SSM_GROUPS
        for pair in range(hpg // 2):
            res = []
            for k in range(2):
                h = g * hpg + pair * 2 + k
                seg = cs[:, h:h + 1] - csT[h:h + 1, :]
                m = (cb * jnp.exp(jnp.where(allowed, seg, NEG))).astype(bf16)
                lo = (g * hpg + pair * 2) * SSM_HEAD_DIM
                res.append(_dot(m, xdt_b[:, lo:lo + LANES]))
            ypairs.append(jnp.where(lane < SSM_HEAD_DIM, res[0], res[1]))
    y_diag = jnp.concatenate(ypairs, axis=1)
    return xs, bm, cm, dt_e, cs_e, xdt, y_diag


def _ssd_finish(y, xs, z, dexp_ref, norm_ref):
    y = (y + dexp_ref[...] * xs) * _silu(z)
    gw = D_INNER // SSM_GROUPS
    outs = []
    for g in range(SSM_GROUPS):
        yg = y[:, g * gw:(g + 1) * gw]
        ms = jnp.mean(yg * yg, axis=-1, keepdims=True)
        outs.append(yg * lax.rsqrt(ms + EPS) * norm_ref[:, g * gw:(g + 1) * gw])
    return jnp.concatenate(outs, axis=1)


def _ssd_prompt_kernel(xbc_ref, z_ref, dt_ref, cw_ref, cb_ref, dtb_ref, alog_ref, dexp_ref, norm_ref,
                       y_ref, h_ref, halo_ref, ht_ref, *, rows):
    c = pl.program_id(1)

    @pl.when(c == 0)
    def _():
        halo_ref[...] = jnp.zeros((SUBLANES, CONV_DIM), f32)
        ht_ref[...] = jnp.zeros(ht_ref.shape, f32)

    xt = xbc_ref[...]
    halo = halo_ref[...]
    row8 = lax.broadcasted_iota(jnp.int32, (SUBLANES, CONV_DIM), 0)
    conv = cb_ref[...] + cw_ref[CONV_WIDTH - 1:CONV_WIDTH, :] * xt
    for sft in range(1, CONV_WIDTH):
        rolled = pltpu.roll(xt, sft, axis=0)
        head = jnp.where(row8 < sft, pltpu.roll(halo, sft, axis=0), rolled[0:SUBLANES, :])
        tap = CONV_WIDTH - 1 - sft
        conv = conv + cw_ref[tap:tap + 1, :] * jnp.concatenate([head, rolled[SUBLANES:, :]], axis=0)
    act = _silu(conv)
    halo_ref[...] = xt[rows - SUBLANES:rows, :]

    ri = lax.broadcasted_iota(jnp.int32, (rows, rows), 0)
    ci = lax.broadcasted_iota(jnp.int32, (rows, rows), 1)
    xs, bm, cm, dt_e, cs_e, xdt, y = _ssd_chunk(act, dt_ref[...], ci <= ri, dtb_ref, alog_ref)

    cs_last = cs_e[rows - 1:rows, :]
    ecs = jnp.exp(cs_e)
    xdte = (xdt * jnp.exp(cs_last - cs_e)).astype(bf16)
    dec = jnp.exp(cs_last)
    hw = D_INNER // SSM_GROUPS
    yoff = []
    for g in range(SSM_GROUPS):
        ht_g = ht_ref[:, g * hw:(g + 1) * hw]
        yoff.append(_dot(cm[:, g * D_STATE:(g + 1) * D_STATE].astype(bf16), ht_g.astype(bf16)))
        upd = _dot_tn(bm[:, g * D_STATE:(g + 1) * D_STATE].astype(bf16), xdte[:, g * hw:(g + 1) * hw])
        ht_ref[:, g * hw:(g + 1) * hw] = ht_g * dec[:, g * hw:(g + 1) * hw] + upd
    y = y + jnp.concatenate(yoff, axis=1) * ecs
    y_ref[...] = _ssd_finish(y, xs, z_ref[...], dexp_ref, norm_ref)

    @pl.when(c == pl.num_programs(1) - 1)
    def _():
        h_ref[0] = ht_ref[...].T.reshape(SSM_HEADS, SSM_HEAD_DIM, D_STATE)


def ssd_prompt(proj, conv_w, conv_b_row, dtb_row, alog_row, dexp_row, norm_row, batch, seq, rows):
    nc = seq // rows
    return pl.pallas_call(
        functools.partial(_ssd_prompt_kernel, rows=rows),
        grid=(batch, nc),
        in_specs=[
            pl.BlockSpec((rows, CONV_DIM), lambda b, c: (b * nc + c, OFF_XBC // CONV_DIM)),
            pl.BlockSpec((rows, D_INNER), lambda b, c: (b * nc + c, OFF_Z // D_INNER)),
            pl.BlockSpec((rows, LANES), lambda b, c: (b * nc + c, OFF_DT // LANES)),
            pl.BlockSpec((CONV_WIDTH, CONV_DIM), lambda b, c: (0, 0)),
            pl.BlockSpec((1, CONV_DIM), lambda b, c: (0, 0)),
            pl.BlockSpec((1, LANES), lambda b, c: (0, 0)),
            pl.BlockSpec((1, LANES), lambda b, c: (0, 0)),
            pl.BlockSpec((1, D_INNER), lambda b, c: (0, 0)),
            pl.BlockSpec((1, D_INNER), lambda b, c: (0, 0)),
        ],
        out_specs=[
            pl.BlockSpec((rows, D_INNER), lambda b, c: (b * nc + c, 0)),
            pl.BlockSpec((1, SSM_HEADS, SSM_HEAD_DIM, D_STATE), lambda b, c: (b, 0, 0, 0)),
        ],
        out_shape=[
            jax.ShapeDtypeStruct((batch * seq, D_INNER), f32),
            jax.ShapeDtypeStruct((batch, SSM_HEADS, SSM_HEAD_DIM, D_STATE), f32),
        ],
        scratch_shapes=[
            pltpu.VMEM((SUBLANES, CONV_DIM), f32),
            pltpu.VMEM((D_STATE, D_INNER), f32),
        ],
        compiler_params=_cparams(("arbitrary", "arbitrary")),
        name="ssd_prompt",
    )(proj, proj, proj, conv_w, conv_b_row, dtb_row, alog_row, dexp_row, norm_row)


def _mem_q(q_ref, g_ref, h):
    q = q_ref[:, h * MEM_HEAD_DIM:(h + 1) * MEM_HEAD_DIM]
    ms = jnp.mean(q * q, axis=-1, keepdims=True)
    return (q * lax.rsqrt(ms + EPS) * g_ref[...] * (MEM_HEAD_DIM ** -0.5)).astype(bf16)


def _mem_attend_kernel(q_ref, mk_ref, mv_ref, g_ref, o_ref, *, head_axis):
    if head_axis:
        tq = q_ref.shape[0]
        q4 = jnp.concatenate([_mem_q(q_ref, g_ref, h) for h in range(MEM_HEADS)], axis=0)
        nrow = MEM_TOKENS * MEM_HEADS
        s = _dot_nt(q4, mk_ref[0].reshape(nrow, MEM_HEAD_DIM).astype(bf16))
        rh = lax.broadcasted_iota(jnp.int32, s.shape, 0) // tq
        ch = lax.broadcasted_iota(jnp.int32, s.shape, 1) % MEM_HEADS
        s = jnp.where(rh == ch, s, NEG)
        m = jnp.max(s, axis=-1, keepdims=True)
        p = jnp.exp(s - m)
        l = jnp.sum(p, axis=-1, keepdims=True)
        o = _dot(p.astype(bf16), mv_ref[0].reshape(nrow, MEM_HEAD_DIM).astype(bf16)) / l
        o_ref[...] = jnp.concatenate([o[h * tq:(h + 1) * tq, :] for h in range(MEM_HEADS)], axis=1)
        return
    outs = []
    for h in range(MEM_HEADS):
        lo = h * MEM_HEAD_DIM
        s = _dot_nt(_mem_q(q_ref, g_ref, h), mk_ref[0, :, lo:lo + MEM_HEAD_DIM].astype(bf16))
        m = jnp.max(s, axis=-1, keepdims=True)
        p = jnp.exp(s - m)
        l = jnp.sum(p, axis=-1, keepdims=True)
        outs.append(_dot(p.astype(bf16), mv_ref[0, :, lo:lo + MEM_HEAD_DIM].astype(bf16)) / l)
    o_ref[...] = jnp.concatenate(outs, axis=1)


def mem_attend(proj, mk, mv, mq_gain_row, nseq, rows_per_seq, tq):
    w = MEM_HEADS * MEM_HEAD_DIM
    nq = rows_per_seq // tq
    head_axis = mk.ndim == 4
    if head_axis:
        kv_spec = pl.BlockSpec((1, MEM_TOKENS, MEM_HEADS, MEM_HEAD_DIM), lambda b, i: (b, 0, 0, 0))
    else:
        kv_spec = pl.BlockSpec((1, MEM_TOKENS, w), lambda b, i: (b, 0, 0))
    return pl.pallas_call(
        functools.partial(_mem_attend_kernel, head_axis=head_axis),
        grid=(nseq, nq),
        in_specs=[
            pl.BlockSpec((tq, w), lambda b, i: (b * nq + i, OFF_MQ // w)),
            kv_spec,
            kv_spec,
            pl.BlockSpec((1, MEM_HEAD_DIM), lambda b, i: (0, 0)),
        ],
        out_specs=pl.BlockSpec((tq, w), lambda b, i: (b * nq + i, 0)),
        out_shape=jax.ShapeDtypeStruct((nseq * rows_per_seq, w), f32),
        compiler_params=_cparams(("arbitrary", "arbitrary")),
        name="mem_attend",
    )(proj, mk, mv, mq_gain_row)


def _merge_kernel(x_ref, a_ref, s_ref, m_ref, ga_ref, gs_ref, gm_ref, wa_ref, ws_ref, wm_ref, wo_ref, o_ref):
    mixed = _sigmoid(ga_ref[...]) * _dot(a_ref[...].astype(bf16), wa_ref[...])
    mixed = mixed + _sigmoid(gs_ref[...]) * _dot(s_ref[...].astype(bf16), ws_ref[...])
    mixed = mixed + _sigmoid(gm_ref[...]) * _dot(m_ref[...].astype(bf16), wm_ref[...])
    o_ref[...] = x_ref[...] + _dot(mixed.astype(bf16), wo_ref[...])


def merge(x, attn_o, ssm_o, mem_o, proj, wa, ws, wm, wo, tm):
    t, d = x.shape
    row = lambda i: (i, 0)
    const = lambda i: (0, 0)
    gate_spec = lambda k: pl.BlockSpec((tm, d), lambda i: (i, OFF_GATES // d + k))
    w_spec = pl.BlockSpec((d, d), const)
    return pl.pallas_call(
        _merge_kernel,
        grid=(t // tm,),
        in_specs=[pl.BlockSpec((tm, d), row)] * 4 + [gate_spec(0), gate_spec(1), gate_spec(2)] + [w_spec] * 4,
        out_specs=pl.BlockSpec((tm, d), row),
        out_shape=jax.ShapeDtypeStruct((t, d), f32),
        compiler_params=_cparams(("arbitrary",)),
        name="merge",
    )(x, attn_o, ssm_o, mem_o, proj, proj, proj, wa, ws, wm, wo)


def _ffn_kernel(x_ref, g_ref, wg_ref, wu_ref, wd_ref, o_ref):
    x = x_ref[...]
    ms = jnp.mean(x * x, axis=-1, keepdims=True)
    h = (x * lax.rsqrt(ms + EPS) * g_ref[...]).astype(bf16)
    act = _silu(_dot(h, wg_ref[...])) * _dot(h, wu_ref[...])
    o_ref[...] = x + _dot(act.astype(bf16), wd_ref[...])


def ffn(x, gain_row, wg, wu, wd, tm):
    t, d = x.shape
    dff = wg.shape[1]
    const = lambda i: (0, 0)
    single = pl.Buffered(1)
    return pl.pallas_call(
        _ffn_kernel,
        grid=(t // tm,),
        in_specs=[
            pl.BlockSpec((tm, d), lambda i: (i, 0)),
            pl.BlockSpec((1, d), const),
            pl.BlockSpec((d, dff), const, pipeline_mode=single),
            pl.BlockSpec((d, dff), const, pipeline_mode=single),
            pl.BlockSpec((dff, d), const, pipeline_mode=single),
        ],
        out_specs=pl.BlockSpec((tm, d), lambda i: (i, 0)),
        out_shape=jax.ShapeDtypeStruct((t, d), f32),
        compiler_params=_cparams(("arbitrary",)),
        name="ffn",
    )(x, gain_row, wg, wu, wd)


def _top3_cols(gate, jb):
    sel = jnp.zeros(gate.shape, jnp.bool_)
    for _ in range(MOBA_TOPK):
        m = jnp.max(gate, axis=-1, keepdims=True)
        idx = jnp.min(jnp.where(gate == m, jb, gate.shape[1]), axis=-1, keepdims=True)
        hit = jb == idx
        sel = jnp.logical_or(sel, jnp.logical_and(hit, m > -jnp.inf))
        gate = jnp.where(hit, -jnp.inf, gate)
    return sel


def _moba_sample_kernel(pt_ref, q_ref, kn_ref, vn_ref, qg_ref, sl_ref, ck_hbm, cv_hbm, o_ref,
                        kbuf, vbuf, s_ref, sem, *, n_pages):
    b = pl.program_id(0)
    nseq = pl.num_programs(0)
    kvw = N_KV_HEADS * HEAD_DIM
    half = SAMPLE_ROWS // 2
    nrow = Q_PER_KV * N_KV_HEADS * half
    nblk = n_pages * PAGE_SIZE // MOBA_BLOCK
    ppb = MOBA_BLOCK // PAGE_SIZE
    past = n_pages * PAGE_SIZE
    slot = b % 2

    def k_copy(seq, sl, p):
        return pltpu.make_async_copy(ck_hbm.at[pt_ref[seq, p]], kbuf.at[sl, p], sem.at[0, sl])

    def v_copy(seq, sl, p):
        return pltpu.make_async_copy(cv_hbm.at[pt_ref[seq, p]], vbuf.at[sl, p], sem.at[1, sl])

    def start_all(seq, sl):
        for p in range(n_pages):
            k_copy(seq, sl, p).start()
        for p in range(n_pages):
            v_copy(seq, sl, p).start()

    @pl.when(b == 0)
    def _():
        start_all(0, 0)

    nxt = jnp.minimum(b + 1, nseq - 1)

    qb = q_ref[...]
    lane_g = lax.broadcasted_iota(jnp.int32, (SAMPLE_ROWS, kvw), 1) // HEAD_DIM
    low_half = lax.broadcasted_iota(jnp.int32, (SAMPLE_ROWS, kvw), 0) < half
    pieces = []
    for r in range(Q_PER_KV):
        slab = qb[:, r * kvw:(r + 1) * kvw]
        for g in range(N_KV_HEADS):
            pieces.append(jnp.where(lane_g == g, slab, 0.0))
    tiles = [jnp.where(low_half, pieces[2 * k], pltpu.roll(pieces[2 * k + 1], half, axis=0))
             for k in range(len(pieces) // 2)]
    qbd = jnp.concatenate(tiles, axis=0)
    ms = jnp.sum(qbd * qbd, axis=-1, keepdims=True) * (1.0 / HEAD_DIM)
    qn = qbd * lax.rsqrt(ms + EPS) * qg_ref[...]
    qs = (qn * (HEAD_DIM ** -0.5 * LOG2E)).astype(bf16)

    for p in range(n_pages):
        k_copy(b, slot, p).wait()

    def k_page(p):
        return kbuf[slot, p].reshape(kvw, PAGE_SIZE)

    lane = lax.broadcasted_iota(jnp.int32, (kvw, LANES), 1)
    kmean_t = jnp.zeros((kvw, LANES), f32)
    for j in range(nblk):
        blk = k_page(j * ppb)
        for pp in range(1, ppb):
            blk = blk + k_page(j * ppb + pp)
        kmean_t = kmean_t + jnp.where(lane == j, jnp.sum(blk, axis=-1, keepdims=True) * (1.0 / MOBA_BLOCK), 0.0)
    kh, kmid, kl = _split3(kmean_t)
    qh, qmid, ql = _split3(qn)
    gate = _dot(qh, kh) + _dot(qh, kmid) + _dot(qmid, kh) + _dot(qh, kl) + _dot(ql, kh) + _dot(qmid, kmid)
    jb = lax.broadcasted_iota(jnp.int32, (nrow, LANES), 1)
    gate = jnp.where(jb < nblk, gate, -jnp.inf)
    pen = jnp.where(_top3_cols(gate, jb), 0.0, NEG)

    slope = sl_ref[...]
    kk = lax.broadcasted_iota(jnp.int32, (nrow, PAGE_SIZE), 1)
    pages_per_dot = 4
    for p0 in range(0, n_pages, pages_per_dot):
        for p in range(p0, p0 + pages_per_dot):
            k_copy(nxt, 1 - slot, p).start()
        k4 = jnp.concatenate([k_page(p).astype(bf16) for p in range(p0, p0 + pages_per_dot)], axis=1)
        s4 = _dot(qs, k4)
        for i in range(pages_per_dot):
            p = p0 + i
            j = p // ppb
            rel = (kk + (p * PAGE_SIZE - past)).astype(f32)
            s_ref[:, p * PAGE_SIZE:(p + 1) * PAGE_SIZE] = (
                s4[:, i * PAGE_SIZE:(i + 1) * PAGE_SIZE] + slope * rel + pen[:, j:j + 1])

    zpad = jnp.zeros((LANES - SAMPLE_ROWS, kvw), f32)
    knew = jnp.concatenate([kn_ref[...], zpad], axis=0).astype(bf16)
    vnew = jnp.concatenate([vn_ref[...], zpad], axis=0).astype(bf16)
    kn_lane = lax.broadcasted_iota(jnp.int32, (nrow, LANES), 1)
    tok = lax.broadcasted_iota(jnp.int32, (nrow, LANES), 0) % half
    s_own = _dot_nt(qs, knew) + slope * kn_lane.astype(f32)
    s_ref[:, past:past + LANES] = jnp.where(kn_lane <= tok, s_own, NEG)

    m = jnp.max(s_ref[...], axis=-1, keepdims=True)
    p_own = jnp.exp2(s_ref[:, past:past + LANES] - m)
    n_acc = 4
    psums = [p_own] + [None] * (n_acc - 1)
    accs = [_dot(p_own.astype(bf16), vnew)] + [None] * (n_acc - 1)

    for p in range(n_pages):
        v_copy(b, slot, p).wait()

    for p0 in range(0, n_pages, ppb):
        for p in range(p0, p0 + ppb):
            v_copy(nxt, 1 - slot, p).start()
        pp = jnp.exp2(s_ref[:, p0 * PAGE_SIZE:(p0 + ppb) * PAGE_SIZE] - m)
        vb = jnp.concatenate([vbuf[slot, p].reshape(kvw, PAGE_SIZE).astype(bf16) for p in range(p0, p0 + ppb)], axis=1)
        pv = _dot_nt(pp.astype(bf16), vb)
        a = (p0 // ppb) % n_acc
        pl_sum = pp[:, :PAGE_SIZE]
        for i in range(1, ppb):
            pl_sum = pl_sum + pp[:, i * PAGE_SIZE:(i + 1) * PAGE_SIZE]
        psums[a] = pl_sum if psums[a] is None else psums[a] + pl_sum
        accs[a] = pv if accs[a] is None else accs[a] + pv
    acc = (accs[0] + accs[1]) + (accs[2] + accs[3])
    psum = (psums[0] + psums[1]) + (psums[2] + psums[3])
    o = acc / jnp.sum(psum, axis=-1, keepdims=True)

    @pl.when(b == nseq - 1)
    def _():
        for p in range(n_pages):
            k_copy(nxt, 1 - slot, p).wait()
        for p in range(n_pages):
            v_copy(nxt, 1 - slot, p).wait()

    slabs = []
    for r in range(Q_PER_KV):
        slab = jnp.zeros((SAMPLE_ROWS, kvw), f32)
        for g in range(N_KV_HEADS):
            piece = r * N_KV_HEADS + g
            tile = o[(piece // 2) * SAMPLE_ROWS:(piece // 2 + 1) * SAMPLE_ROWS, :]
            if piece % 2:
                tile = pltpu.roll(tile, half, axis=0)
            slab = slab + jnp.where(lane_g == g, tile, 0.0)
        slabs.append(slab)
    o_ref[...] = jnp.concatenate(slabs, axis=1)


def moba_sample(page_table, proj, kn_new, ck, cv, q_gain_row, slope_rows, nseq):
    kvw = N_KV_HEADS * HEAD_DIM
    qw = N_HEADS * HEAD_DIM
    n_pages = page_table.shape[1]
    nrow = Q_PER_KV * N_KV_HEADS * (SAMPLE_ROWS // 2)
    past = n_pages * PAGE_SIZE
    grid_spec = pltpu.PrefetchScalarGridSpec(
        num_scalar_prefetch=1,
        grid=(nseq,),
        in_specs=[
            pl.BlockSpec((SAMPLE_ROWS, qw), lambda b, pt: (b, OFF_Q // qw)),
            pl.BlockSpec((SAMPLE_ROWS, kvw), lambda b, pt: (b, 0)),
            pl.BlockSpec((SAMPLE_ROWS, kvw), lambda b, pt: (b, OFF_V // kvw)),
            pl.BlockSpec((1, kvw), lambda b, pt: (0, 0)),
            pl.BlockSpec((nrow, LANES), lambda b, pt: (0, 0)),
            pl.BlockSpec(memory_space=pl.ANY),
            pl.BlockSpec(memory_space=pl.ANY),
        ],
        out_specs=pl.BlockSpec((SAMPLE_ROWS, qw), lambda b, pt: (b, 0)),
        scratch_shapes=[
            pltpu.VMEM((2, n_pages, N_KV_HEADS, HEAD_DIM, PAGE_SIZE), f32),
            pltpu.VMEM((2, n_pages, N_KV_HEADS, HEAD_DIM, PAGE_SIZE), f32),
            pltpu.VMEM((nrow, past + LANES), f32),
            pltpu.SemaphoreType.DMA((2, 2)),
        ],
    )
    return pl.pallas_call(
        functools.partial(_moba_sample_kernel, n_pages=n_pages),
        grid_spec=grid_spec,
        out_shape=jax.ShapeDtypeStruct((nseq * SAMPLE_ROWS, qw), f32),
        compiler_params=_cparams(("arbitrary",)),
        name="moba_sample",
    )(page_table, proj, kn_new, proj, q_gain_row, slope_rows, ck, cv)


def _ssd_sample_kernel(ext_ref, z_ref, dt_ref, h0_ref, cw_ref, cb_ref, dtb_ref, alog_ref, dexp_ref, norm_ref,
                       y_ref, h_ref, scr_ref, *, nseq_step, n_new):
    rows = nseq_step * SAMPLE_ROWS
    scr_ref[pl.ds(0, rows), :] = ext_ref[...]
    scr_ref[pl.ds(rows, SUBLANES), :] = jnp.zeros((SUBLANES, CONV_DIM), f32)
    act = _ssd_conv_act(scr_ref, rows, cw_ref, cb_ref, base=0)

    ri = lax.broadcasted_iota(jnp.int32, (rows, rows), 0)
    ci = lax.broadcasted_iota(jnp.int32, (rows, rows), 1)
    allowed = jnp.logical_and(ri // SAMPLE_ROWS == ci // SAMPLE_ROWS,
                              jnp.logical_and(ci % SAMPLE_ROWS <= ri % SAMPLE_ROWS, ci % SAMPLE_ROWS < n_new))
    xs, bm, cm, dt_e, cs_e, xdt, y = _ssd_chunk(act, dt_ref[...], allowed, dtb_ref, alog_ref)

    valid = lax.broadcasted_iota(jnp.int32, (SAMPLE_ROWS, 1), 0) < n_new
    ecs = jnp.exp(cs_e)
    hw = D_INNER // SSM_GROUPS
    hpg = SSM_HEADS // SSM_GROUPS
    yoff_rows = []
    for i in range(nseq_step):
        lo = i * SAMPLE_ROWS
        cs_i = cs_e[lo:lo + SAMPLE_ROWS, :]
        cs_last = cs_e[lo + n_new - 1:lo + n_new, :]
        xdte = jnp.where(valid, xdt[lo:lo + SAMPLE_ROWS, :] * jnp.exp(cs_last - cs_i), 0.0)
        yg = []
        upd = []
        for g in range(SSM_GROUPS):
            h0g = h0_ref[i, g * hpg:(g + 1) * hpg].reshape(hw, D_STATE)
            yg.append(_dot_nt(cm[lo:lo + SAMPLE_ROWS, g * D_STATE:(g + 1) * D_STATE], h0g))
            upd.append(_dot_tn(xdte[:, g * hw:(g + 1) * hw], bm[lo:lo + SAMPLE_ROWS, g * D_STATE:(g + 1) * D_STATE]))
        yoff_rows.append(jnp.concatenate(yg, axis=1))
        dec_col = jnp.broadcast_to(jnp.exp(cs_last), (SUBLANES, D_INNER)).T[:, 0:1]
        h_new = h0_ref[i].reshape(D_INNER, D_STATE) * dec_col + jnp.concatenate(upd, axis=0)
        h_ref[i] = h_new.reshape(SSM_HEADS, SSM_HEAD_DIM, D_STATE)
    y = y + jnp.concatenate(yoff_rows, axis=0) * ecs
    y_ref[...] = _ssd_finish(y, xs, z_ref[...], dexp_ref, norm_ref)


def ssd_sample(ext, proj, h0, conv_w, conv_b_row, dtb_row, alog_row, dexp_row, norm_row, nseq, nseq_step, n_new):
    rows = nseq_step * SAMPLE_ROWS
    const = lambda i: (0, 0)
    state_spec = pl.BlockSpec((nseq_step, SSM_HEADS, SSM_HEAD_DIM, D_STATE), lambda i: (i, 0, 0, 0))
    return pl.pallas_call(
        functools.partial(_ssd_sample_kernel, nseq_step=nseq_step, n_new=n_new),
        grid=(nseq // nseq_step,),
        in_specs=[
            pl.BlockSpec((rows, CONV_DIM), lambda i: (i, 0)),
            pl.BlockSpec((rows, D_INNER), lambda i: (i, OFF_Z // D_INNER)),
            pl.BlockSpec((rows, LANES), lambda i: (i, OFF_DT // LANES)),
            state_spec,
            pl.BlockSpec((CONV_WIDTH, CONV_DIM), const),
            pl.BlockSpec((1, CONV_DIM), const),
            pl.BlockSpec((1, LANES), const),
            pl.BlockSpec((1, LANES), const),
            pl.BlockSpec((1, D_INNER), const),
            pl.BlockSpec((1, D_INNER), const),
        ],
        out_specs=[pl.BlockSpec((rows, D_INNER), lambda i: (i, 0)), state_spec],
        out_shape=[
            jax.ShapeDtypeStruct((nseq * SAMPLE_ROWS, D_INNER), f32),
            jax.ShapeDtypeStruct(h0.shape, f32),
        ],
        scratch_shapes=[pltpu.VMEM((rows + SUBLANES, CONV_DIM), f32)],
        compiler_params=_cparams(("arbitrary",)),
        name="ssd_sample",
    )(ext, proj, proj, h0, conv_w, conv_b_row, dtb_row, alog_row, dexp_row, norm_row)


IN_SIZES = (N_HEADS * HEAD_DIM, N_KV_HEADS * HEAD_DIM, N_KV_HEADS * HEAD_DIM, D_INNER, CONV_DIM, SSM_HEADS,
            MEM_HEADS * MEM_HEAD_DIM, 3 * D_MODEL)
TM = 512
TN_PROJ = 2944
TQ = 256
SSD_ROWS = 256
SSD_SAMPLE_SEQS = 8


def _pad_lanes(v):
    return jnp.pad(v, (0, LANES - v.shape[0])).reshape(1, LANES)


def kernel(x_prompt, x_sample, mem_prompt, cache_k, cache_v, page_table, state_conv, state_ssm, cache_mem_k, cache_mem_v, norm_mix, w_in, q_norm, k_norm, conv_w, conv_b, dt_bias, a_log, d_skip, ssm_norm, mem_norm, w_mem_kv, mq_norm, mk_norm, w_attn_br, w_ssm_br, w_mem_br, w_out, norm_ffn, w_gate, w_up, w_down):
    assert w_in.shape[0] == 1, "single layer"
    assert x_sample.shape[1] <= SAMPLE_ROWS // 2, "sample MoBA packs two heads' tokens into one row tile"
    batch, seq, d = x_prompt.shape
    nseq, n_new, _ = x_sample.shape
    kvw = N_KV_HEADS * HEAD_DIM
    qw = N_HEADS * HEAD_DIM

    offs = np.cumsum(IN_SIZES)[:-1].tolist()
    wq, wk, wv, wz, wxbc, wdt, wmq, wgates = jnp.split(w_in[0], offs, axis=1)
    wdt = jnp.pad(wdt, ((0, 0), (0, LANES - SSM_HEADS)))
    tail = [wz, wxbc, wmq, wgates, wk, wv, wdt]
    w_prompt = jnp.concatenate([wq] + tail, axis=1).astype(bf16)
    wq_rgd = wq.reshape(d, N_KV_HEADS, Q_PER_KV, HEAD_DIM).transpose(0, 2, 1, 3).reshape(d, qw)
    w_sample = jnp.concatenate([wq_rgd] + tail, axis=1).astype(bf16)
    wa = w_attn_br[0].astype(bf16)
    wa_rgd = w_attn_br[0].reshape(N_KV_HEADS, Q_PER_KV, HEAD_DIM, d).transpose(1, 0, 2, 3).reshape(qw, d).astype(bf16)
    ws, wm, wo = w_ssm_br[0].astype(bf16), w_mem_br[0].astype(bf16), w_out[0].astype(bf16)
    wg, wu, wd = w_gate[0].astype(bf16), w_up[0].astype(bf16), w_down[0].astype(bf16)

    k_gain = jnp.tile(k_norm[0], N_KV_HEADS).reshape(1, kvw)
    conv_b_row = conv_b[0].reshape(1, CONV_DIM)
    dtb_row, alog_row = _pad_lanes(dt_bias[0]), _pad_lanes(a_log[0])
    dexp_row = jnp.repeat(d_skip[0], SSM_HEAD_DIM).reshape(1, D_INNER)
    ssm_norm_row = ssm_norm[0].reshape(1, D_INNER)
    mq_gain = mq_norm[0].reshape(1, MEM_HEAD_DIM)
    ffn_gain = norm_ffn[0].reshape(1, d)
    slopes2 = jnp.exp2(-8.0 * jnp.arange(1, N_HEADS + 1, dtype=f32) / N_HEADS) * LOG2E

    xp = x_prompt.reshape(batch * seq, d)
    proj = inproj(xp, norm_mix[0], w_prompt, TM, TN_PROJ)
    kn, kb, vt, kmean = kprep(proj, k_gain, batch, seq, 2048)
    kmean_g = kmean.reshape(batch, seq // MOBA_BLOCK, N_KV_HEADS, HEAD_DIM).transpose(0, 2, 1, 3)
    slope_cols = jnp.repeat(slopes2.reshape(N_KV_HEADS, Q_PER_KV), TQ, axis=1).reshape(N_KV_HEADS, 1, Q_PER_KV * TQ)
    q_gain_col = jnp.broadcast_to(q_norm[0][:, None], (HEAD_DIM, TQ))
    attn_o = moba_prompt(proj, kb, vt, kmean_g, q_gain_col, slope_cols, batch, seq, TQ)
    ssm_o, h_prompt = ssd_prompt(proj, conv_w[0], conv_b_row, dtb_row, alog_row, dexp_row, ssm_norm_row,
                                 batch, seq, SSD_ROWS)
    mem_kv = inproj(mem_prompt.reshape(batch * MEM_TOKENS, d), mem_norm[0], w_mem_kv[0].astype(bf16), TM, qw)
    mk = headnorm(mem_kv, 0, qw, jnp.tile(mk_norm[0], MEM_HEADS).reshape(1, qw), MEM_HEAD_DIM, TM)
    mv = mem_kv[:, qw:]
    mem_o = mem_attend(proj, mk.reshape(batch, MEM_TOKENS, qw), mv.reshape(batch, MEM_TOKENS, qw), mq_gain,
                       batch, seq, TM)
    x1 = merge(xp, attn_o, ssm_o, mem_o, proj, wa, ws, wm, wo, TM)
    y_prompt = ffn(x1, ffn_gain, wg, wu, wd, TM).reshape(batch, seq, d)

    k_prompt = kn.reshape(1, batch, seq, N_KV_HEADS, HEAD_DIM)
    v_prompt = proj[:, OFF_V:OFF_V + kvw].reshape(1, batch, seq, N_KV_HEADS, HEAD_DIM)
    conv_prompt = proj.reshape(batch, seq, N_PROJ)[:, seq - (CONV_WIDTH - 1):, OFF_XBC:OFF_XBC + CONV_DIM][None]
    mem_k_prompt = mk.reshape(1, batch, MEM_TOKENS, MEM_HEADS, MEM_HEAD_DIM)
    mem_v_prompt = mv.reshape(1, batch, MEM_TOKENS, MEM_HEADS, MEM_HEAD_DIM)

    xs = jnp.pad(x_sample, ((0, 0), (0, SAMPLE_ROWS - n_new), (0, 0))).reshape(nseq * SAMPLE_ROWS, d)
    proj_s = inproj(xs, norm_mix[0], w_sample, TM, TN_PROJ)
    kn_s = headnorm(proj_s, OFF_K // kvw, kvw, k_gain, HEAD_DIM, TM)
    slope_rows = jnp.broadcast_to(
        jnp.repeat(slopes2.reshape(N_KV_HEADS, Q_PER_KV).T.reshape(-1), SAMPLE_ROWS // 2)[:, None],
        (N_HEADS * (SAMPLE_ROWS // 2), LANES))
    ck_t = jnp.transpose(cache_k[0], (0, 2, 3, 1))
    cv_t = jnp.transpose(cache_v[0], (0, 2, 3, 1))
    attn_s = moba_sample(page_table, proj_s, kn_s, ck_t, cv_t, jnp.tile(q_norm[0], N_KV_HEADS).reshape(1, kvw),
                         slope_rows, nseq)
    xbc_new = proj_s[:, OFF_XBC:OFF_XBC + CONV_DIM].reshape(nseq, SAMPLE_ROWS, CONV_DIM)[:, :n_new]
    ext = jnp.concatenate(
        [state_conv[0], xbc_new, jnp.zeros((nseq, SAMPLE_ROWS - (CONV_WIDTH - 1) - n_new, CONV_DIM), f32)], axis=1)
    ssm_s, h_sample = ssd_sample(ext.reshape(nseq * SAMPLE_ROWS, CONV_DIM), proj_s, state_ssm[0], conv_w[0], conv_b_row,
                                 dtb_row, alog_row, dexp_row, ssm_norm_row, nseq, SSD_SAMPLE_SEQS, n_new)
    mem_s = mem_attend(proj_s, cache_mem_k[0], cache_mem_v[0], mq_gain, nseq, SAMPLE_ROWS, SAMPLE_ROWS)
    x1s = merge(xs, attn_s, ssm_s, mem_s, proj_s, wa_rgd, ws, wm, wo, TM)
    y_sample = ffn(x1s, ffn_gain, wg, wu, wd, TM).reshape(nseq, SAMPLE_ROWS, d)[:, :n_new]

    k_sample = kn_s.reshape(nseq, SAMPLE_ROWS, N_KV_HEADS, HEAD_DIM)[:, :n_new][None]
    v_sample = proj_s[:, OFF_V:OFF_V + kvw].reshape(nseq, SAMPLE_ROWS, N_KV_HEADS, HEAD_DIM)[:, :n_new][None]
    conv_sample = ext[:, n_new:n_new + CONV_WIDTH - 1][None]

    return (y_prompt, y_sample, k_prompt, v_prompt, conv_prompt, h_prompt[None], mem_k_prompt, mem_v_prompt,
            k_sample, v_sample, conv_sample, h_sample[None])
```

```python
import functools
import math

import jax
import jax.numpy as jnp
import numpy as np
from jax import lax
from jax.experimental import pallas as pl
from jax.experimental.pallas import tpu as pltpu

f32 = jnp.float32
bf16 = jnp.bfloat16

D_MODEL = 1024
N_HEADS = 16
N_KV_HEADS = 4
HEAD_DIM = 64
Q_PER_KV = N_HEADS // N_KV_HEADS
MOBA_BLOCK = 256
MOBA_TOPK = 3
SSM_HEADS = 16
SSM_HEAD_DIM = 64
D_INNER = SSM_HEADS * SSM_HEAD_DIM
SSM_GROUPS = 4
D_STATE = 128
CONV_WIDTH = 4
CONV_DIM = D_INNER + 2 * SSM_GROUPS * D_STATE
MEM_TOKENS = 256
MEM_HEADS = 4
MEM_HEAD_DIM = 256
D_FF = 2816
EPS = 1e-6
PAGE_SIZE = 128

LANES = 128
SUBLANES = 8
LOG2E = 1.4426950408889634
NEG = -1e30
VMEM_LIMIT = 56 * 1024 * 1024

OFF_Q, OFF_Z, OFF_XBC, OFF_MQ, OFF_GATES, OFF_K, OFF_V, OFF_DT = 0, 1024, 2048, 4096, 5120, 8192, 8448, 8704
N_PROJ = 8832
SAMPLE_ROWS = 8
MOBA_GROUP = 4


def _cparams(sem):
    return pltpu.CompilerParams(dimension_semantics=sem, vmem_limit_bytes=VMEM_LIMIT)


def _split3(x):
    h = x.astype(bf16)
    r = x - h.astype(f32)
    m = r.astype(bf16)
    l = (r - m.astype(f32)).astype(bf16)
    return h, m, l


def _dot(a, b):
    return jnp.dot(a, b, preferred_element_type=f32)


def _dot_nt(a, b):
    return lax.dot_general(a, b, (((1,), (1,)), ((), ())), preferred_element_type=f32)


def _dot_tn(a, b):
    return lax.dot_general(a, b, (((0,), (0,)), ((), ())), preferred_element_type=f32)


def _dot_exact_rhs(x, m_bf16):
    h, m, l = _split3(x)
    return _dot(h, m_bf16) + _dot(m, m_bf16) + _dot(l, m_bf16)


def _dot_exact_lhs(m_bf16, x):
    h, m, l = _split3(x)
    return _dot(m_bf16, h) + _dot(m_bf16, m) + _dot(m_bf16, l)


def _sigmoid(x):
    return 0.5 * jnp.tanh(0.5 * x) + 0.5


def _silu(x):
    return x * _sigmoid(x)


def _softplus(x):
    return jnp.maximum(x, 0.0) + jnp.log(1.0 + jnp.exp(-jnp.abs(x)))


def _inproj_kernel(x_ref, g_ref, w_ref, o_ref):
    x = x_ref[...]
    ms = jnp.mean(x * x, axis=-1, keepdims=True)
    xn = (x * lax.rsqrt(ms + EPS) * g_ref[...]).astype(bf16)
    o_ref[...] = _dot(xn, w_ref[...])


def inproj(x, gain, w, tm, tn):
    t, d = x.shape
    n = w.shape[1]
    return pl.pallas_call(
        _inproj_kernel,
        grid=(n // tn, t // tm),
        in_specs=[
            pl.BlockSpec((tm, d), lambda j, i: (i, 0)),
            pl.BlockSpec((1, d), lambda j, i: (0, 0)),
            pl.BlockSpec((d, tn), lambda j, i: (0, j)),
        ],
        out_specs=pl.BlockSpec((tm, tn), lambda j, i: (i, j)),
        out_shape=jax.ShapeDtypeStruct((t, n), f32),
        compiler_params=_cparams(("arbitrary", "arbitrary")),
        name="inproj",
    )(x, gain.reshape(1, d), w)


def _seg_ones(width, seg):
    r = lax.broadcasted_iota(jnp.int32, (width, width), 0) // seg
    c = lax.broadcasted_iota(jnp.int32, (width, width), 1) // seg
    return jnp.where(r == c, 1.0, 0.0).astype(bf16)


def _seg_rmsnorm(x, gain_row, seg):
    ss = _dot_exact_rhs(x * x, _seg_ones(x.shape[1], seg))
    return x * lax.rsqrt(ss * (1.0 / seg) + EPS) * gain_row


def _headnorm_kernel(x_ref, g_ref, o_ref, *, seg):
    o_ref[...] = _seg_rmsnorm(x_ref[...], g_ref[...], seg)


def headnorm(x, col_block, width, gain_row, seg, tm):
    t = x.shape[0]
    return pl.pallas_call(
        functools.partial(_headnorm_kernel, seg=seg),
        grid=(t // tm,),
        in_specs=[
            pl.BlockSpec((tm, width), lambda i: (i, col_block)),
            pl.BlockSpec((1, width), lambda i: (0, 0)),
        ],
        out_specs=pl.BlockSpec((tm, width), lambda i: (i, 0)),
        out_shape=jax.ShapeDtypeStruct((t, width), f32),
        compiler_params=_cparams(("arbitrary",)),
        name="headnorm",
    )(x, gain_row)


def _kprep_kernel(k_ref, v_ref, g_ref, knt_ref, vt32_ref, ka_ref, vt_ref, km_ref):
    kn = _seg_rmsnorm(k_ref[...], g_ref[...], HEAD_DIM)
    knt_ref[0] = kn.T
    kn_b = kn.astype(bf16)
    rows = kn.shape[0]
    kvw = kn.shape[1]
    r = lax.broadcasted_iota(jnp.int32, (kvw, LANES), 0)
    c = lax.broadcasted_iota(jnp.int32, (kvw, LANES), 1)
    lane = lax.broadcasted_iota(jnp.int32, (rows, LANES), 1)
    pos = (lax.broadcasted_iota(jnp.int32, (rows, LANES), 0) % MOBA_BLOCK).astype(f32)
    pos_lanes = jnp.where(jnp.logical_and(lane >= HEAD_DIM, lane < HEAD_DIM + 3), pos, 0.0)
    for g in range(N_KV_HEADS):
        pick = jnp.where(jnp.logical_and(r == c + g * HEAD_DIM, c < HEAD_DIM), 1.0, 0.0).astype(bf16)
        ka_ref[0, g] = (_dot(kn_b, pick) + pos_lanes).astype(bf16)
    nb = kn.shape[0] // MOBA_BLOCK
    km_ref[0] = jnp.sum(kn.reshape(nb, MOBA_BLOCK, kn.shape[1]), axis=1) * (1.0 / MOBA_BLOCK)
    vt = v_ref[...].T
    vt32_ref[0] = vt
    vt_ref[0] = vt.astype(bf16)


def kprep(proj, k_gain_row, batch, seq, tm):
    kvw = N_KV_HEADS * HEAD_DIM
    nt = seq // tm
    nb = tm // MOBA_BLOCK
    return pl.pallas_call(
        _kprep_kernel,
        grid=(batch, nt),
        in_specs=[
            pl.BlockSpec((tm, kvw), lambda b, i: (b * nt + i, OFF_K // kvw)),
            pl.BlockSpec((tm, kvw), lambda b, i: (b * nt + i, OFF_V // kvw)),
            pl.BlockSpec((1, kvw), lambda b, i: (0, 0)),
        ],
        out_specs=[
            pl.BlockSpec((1, kvw, tm), lambda b, i: (b, 0, i)),
            pl.BlockSpec((1, kvw, tm), lambda b, i: (b, 0, i)),
            pl.BlockSpec((1, N_KV_HEADS, tm, LANES), lambda b, i: (b, 0, i, 0)),
            pl.BlockSpec((1, kvw, tm), lambda b, i: (b, 0, i)),
            pl.BlockSpec((1, nb, kvw), lambda b, i: (b, i, 0)),
        ],
        out_shape=[
            jax.ShapeDtypeStruct((batch, kvw, seq), f32),
            jax.ShapeDtypeStruct((batch, kvw, seq), f32),
            jax.ShapeDtypeStruct((batch, N_KV_HEADS, seq, LANES), bf16),
            jax.ShapeDtypeStruct((batch, kvw, seq), bf16),
            jax.ShapeDtypeStruct((batch, seq // MOBA_BLOCK, kvw), f32),
        ],
        compiler_params=_cparams(("arbitrary", "arbitrary")),
        name="kprep",
    )(proj, proj, k_gain_row)


def _top3_penalty(gate, jb, axis):
    n = gate.shape[axis]
    pen = jnp.full(gate.shape, NEG, f32)
    for _ in range(MOBA_TOPK):
        m = jnp.max(gate, axis=axis, keepdims=True)
        idx = jnp.min(jnp.where(gate == m, jb, n), axis=axis, keepdims=True)
        idx = jnp.where(m > -jnp.inf, idx, n)
        hit = jb == idx
        pen = jnp.where(hit, 0.0, pen)
        gate = jnp.where(hit, -jnp.inf, gate)
    return pen


def _moba_prompt_kernel(q_ref, kb_ref, vt_ref, km_ref, qg_ref, sl_ref, o_ref,
                        qbd_ref, pen_ref, acc_ref, m_ref, s_ref, cm_ref, *, tq):
    qt = pl.program_id(2)
    nb_tile = tq // MOBA_BLOCK
    own = qt * nb_tile
    cols = Q_PER_KV * tq
    nblk = km_ref.shape[2]
    q_in_tile = lax.broadcasted_iota(jnp.int32, (1, cols), 1) % tq
    own_col = own + q_in_tile // MOBA_BLOCK

    qT = q_ref[...].T
    parts = []
    for r in range(Q_PER_KV):
        xr = qT[r * HEAD_DIM:(r + 1) * HEAD_DIM, :]
        ms = jnp.mean(xr * xr, axis=0, keepdims=True)
        parts.append(xr * lax.rsqrt(ms + EPS) * qg_ref[...])
    qn = jnp.concatenate(parts, axis=1)

    km = km_ref[0, 0]
    kh, kmid, kl = _split3(km)
    qh, qmid, ql = _split3(qn)
    gate = _dot(kh, qh) + _dot(kh, qmid) + _dot(kmid, qh) + _dot(kh, ql) + _dot(kl, qh) + _dot(kmid, qmid)
    jb = lax.broadcasted_iota(jnp.int32, (nblk, cols), 0)
    gate = jnp.where(jb < own_col, gate, -jnp.inf)
    pen_ref[...] = jnp.where(jb == own_col, 0.0, _top3_penalty(gate, jb, 0))

    slope = sl_ref[0]
    s_hi, s_mid, s_lo = _split3(slope)
    rowi = lax.broadcasted_iota(jnp.int32, (HEAD_DIM, cols), 0)
    slope_rows = jnp.where(rowi == 0, s_hi.astype(f32), jnp.where(rowi == 1, s_mid.astype(f32),
                           jnp.where(rowi == 2, s_lo.astype(f32), 0.0)))
    qbd_ref[...] = jnp.concatenate([qn * (HEAD_DIM ** -0.5 * LOG2E), slope_rows], axis=0).astype(bf16)

    kk = lax.broadcasted_iota(jnp.int32, (MOBA_BLOCK, cols), 0)
    ones = jnp.ones((2 * SUBLANES, MOBA_BLOCK), bf16)

    def blk(j):
        return pl.ds(pl.multiple_of(j * MOBA_BLOCK, MOBA_BLOCK), MOBA_BLOCK)

    def v_aug(j):
        return jnp.concatenate([vt_ref[0, :, blk(j)], ones], axis=0)

    def shift_of(j):
        return slope * ((j - own) * MOBA_BLOCK).astype(f32) + pen_ref[pl.ds(j, 1), :]

    def stage_scores(j, buf):
        s = _dot(kb_ref[0, 0, blk(j), :], qbd_ref[...])
        s_ref[buf] = s
        cm_ref[buf] = jnp.max(s, axis=0, keepdims=True)

    def stage_accum(j, buf, valid=None):
        shift = shift_of(j)
        if valid is not None:
            shift = shift + jnp.where(valid, 0.0, NEG)
        m_old = m_ref[...]
        m_new = jnp.maximum(m_old, cm_ref[buf] + shift)
        p = jnp.exp2(s_ref[buf] - (m_new - shift)).astype(bf16)
        acc_ref[...] = acc_ref[...] * jnp.exp2(m_old - m_new) + _dot(v_aug(j), p)
        m_ref[...] = m_new

    last = jnp.maximum(own - 1, 0)
    stage_scores(0, 1)
    col_blk = q_in_tile // MOBA_BLOCK
    causal = kk <= q_in_tile % MOBA_BLOCK
    ss, shifts = [], []
    for t in range(nb_tile):
        s = _dot(kb_ref[0, 0, blk(own + t), :], qbd_ref[...])
        visible = jnp.logical_or(col_blk > t, jnp.logical_and(col_blk == t, causal))
        ss.append(jnp.where(visible, s, NEG))
        shifts.append(shift_of(own + t) if nb_tile > 1 else jnp.zeros((1, cols), f32))
    m0 = None
    for s, sh in zip(ss, shifts):
        mj = jnp.max(s, axis=0, keepdims=True) + sh
        m0 = mj if m0 is None else jnp.maximum(m0, mj)
    pv0 = None
    for t, (s, sh) in enumerate(zip(ss, shifts)):
        pv = _dot(v_aug(own + t), jnp.exp2(s - (m0 - sh)).astype(bf16))
        pv0 = pv if pv0 is None else pv0 + pv
    acc_ref[...] = pv0
    m_ref[...] = m0


    def pair(j0):
        j1, j2 = j0 + 1, j0 + 2
        stage_scores(jnp.minimum(j1, last), 0)
        stage_accum(j0, 1)
        stage_scores(jnp.minimum(j2, last), 1)
        stage_accum(jnp.minimum(j1, last), 0, valid=j1 < own)

    def body8(i, c):
        for k in range(4):
            pair(8 * i + 2 * k)
        return c

    def body4(i, c):
        pair(8 * n8)
        pair(8 * n8 + 2)
        return c

    def body2(i, c):
        pair(8 * n8 + 4 * n4 + 2 * i)
        return c

    n8 = own // 8
    n4 = (own - 8 * n8) // 4
    lax.fori_loop(0, n8, body8, 0)
    lax.fori_loop(0, n4, body4, 0)
    lax.fori_loop(0, (own - 8 * n8 - 4 * n4 + 1) // 2, body2, 0)

    acc = acc_ref[...]
    o = acc[0:HEAD_DIM, :] / acc[HEAD_DIM:HEAD_DIM + 1, :]
    oT = jnp.concatenate([o[:, r * tq:(r + 1) * tq] for r in range(Q_PER_KV)], axis=0)
    o_ref[...] = oT.T


def moba_prompt(proj, kb, vt, kmean_g, q_gain_col, slopes2, batch, seq, tq):
    nq = seq // tq
    qw = Q_PER_KV * HEAD_DIM
    kvw = N_KV_HEADS * HEAD_DIM
    nblk = seq // MOBA_BLOCK
    cols = Q_PER_KV * tq
    return pl.pallas_call(
        functools.partial(_moba_prompt_kernel, tq=tq),
        grid=(batch, N_KV_HEADS, nq),
        in_specs=[
            pl.BlockSpec((tq, qw), lambda b, g, i: (b * nq + i, OFF_Q // qw + g)),
            pl.BlockSpec((1, 1, seq, LANES), lambda b, g, i: (b, g, 0, 0)),
            pl.BlockSpec((1, HEAD_DIM, seq), lambda b, g, i: (b, g, 0)),
            pl.BlockSpec((1, 1, nblk, HEAD_DIM), lambda b, g, i: (b, g, 0, 0)),
            pl.BlockSpec((HEAD_DIM, tq), lambda b, g, i: (0, 0)),
            pl.BlockSpec((1, 1, cols), lambda b, g, i: (g, 0, 0)),
        ],
        out_specs=pl.BlockSpec((tq, qw), lambda b, g, i: (b * nq + i, g)),
        out_shape=jax.ShapeDtypeStruct((batch * seq, N_HEADS * HEAD_DIM), f32),
        scratch_shapes=[
            pltpu.VMEM((LANES, cols), bf16),
            pltpu.VMEM((nblk, cols), f32),
            pltpu.VMEM((HEAD_DIM + 2 * SUBLANES, cols), f32),
            pltpu.VMEM((1, cols), f32),
            pltpu.VMEM((2, MOBA_BLOCK, cols), f32),
            pltpu.VMEM((2, 1, cols), f32),
        ],
        compiler_params=_cparams(("arbitrary", "arbitrary", "arbitrary")),
        name="moba_prompt",
    )(proj, kb, vt, kmean_g, q_gain_col, slopes2)


def _head_expand():
    r = lax.broadcasted_iota(jnp.int32, (LANES, D_INNER), 0)
    c = lax.broadcasted_iota(jnp.int32, (LANES, D_INNER), 1) // SSM_HEAD_DIM
    return jnp.where(r == c, 1.0, 0.0).astype(bf16)


def _ssd_conv_act(ext_ref, rows, cw_ref, cb_ref, base):
    conv = cb_ref[...] + cw_ref[0:1, :] * ext_ref[pl.ds(base, rows), :]
    for i in range(1, CONV_WIDTH):
        conv = conv + cw_ref[i:i + 1, :] * ext_ref[pl.ds(base + i, rows), :]
    return _silu(conv)


def _ssd_chunk(act, dt_raw, allowed, dtb_ref, alog_ref):
    rows = act.shape[0]
    xs = act[:, :D_INNER]
    gn = SSM_GROUPS * D_STATE
    bm = act[:, D_INNER:D_INNER + gn]
    cm = act[:, D_INNER + gn:]
    dt = _softplus(dt_raw + dtb_ref[...])
    a = -jnp.exp(alog_ref[...])
    mask_b = jnp.where(allowed, 1.0, 0.0).astype(bf16)
    cs = _dot_exact_lhs(mask_b, dt * (a * LOG2E))
    expand = _head_expand()
    dt_e = _dot_exact_rhs(dt, expand)
    cs_e = _dot_exact_rhs(cs, expand)
    xdt = xs * dt_e
    csT = cs.T
    xdt_b = xdt.astype(bf16)
    ypairs = []
    lane = lax.broadcasted_iota(jnp.int32, (rows, LANES), 1)
    for g in range(SSM_GROUPS):
        cb = _dot_nt(cm[:, g * D_STATE:(g + 1) * D_STATE].astype(bf16), bm[:, g * D_STATE:(g + 1) * D_STATE].astype(bf16))
        hpg = SSM_HEADS // SSM_GROUPS
        for pair in range(hpg // 2):
            res = []
            for k in range(2):
                h = g * hpg + pair * 2 + k
                seg = cs[:, h:h + 1] - csT[h:h + 1, :]
                m = (cb * jnp.exp2(jnp.where(allowed, seg, NEG))).astype(bf16)
                lo = (g * hpg + pair * 2) * SSM_HEAD_DIM
                res.append(_dot(m, xdt_b[:, lo:lo + LANES]))
            ypairs.append(jnp.where(lane < SSM_HEAD_DIM, res[0], res[1]))
    y_diag = jnp.concatenate(ypairs, axis=1)
    return xs, bm, cm, dt_e, cs_e, xdt, y_diag


def _ssd_finish(y, xs, z, dexp_ref, norm_ref):
    y = (y + dexp_ref[...] * xs) * _silu(z)
    gw = D_INNER // SSM_GROUPS
    outs = []
    for g in range(SSM_GROUPS):
        yg = y[:, g * gw:(g + 1) * gw]
        ms = jnp.mean(yg * yg, axis=-1, keepdims=True)
        outs.append(yg * lax.rsqrt(ms + EPS) * norm_ref[:, g * gw:(g + 1) * gw])
    return jnp.concatenate(outs, axis=1)


def _ssd_prompt_kernel(xbc_ref, z_ref, dt_ref, cw_ref, cb_ref, dtb_ref, alog_ref, dexp_ref, norm_ref,
                       y_ref, h_ref, halo_ref, ht_ref, *, rows):
    c = pl.program_id(1)

    @pl.when(c == 0)
    def _():
        halo_ref[...] = jnp.zeros((SUBLANES, CONV_DIM), f32)
        ht_ref[...] = jnp.zeros(ht_ref.shape, f32)

    xt = xbc_ref[...]
    halo = halo_ref[...]
    row8 = lax.broadcasted_iota(jnp.int32, (SUBLANES, CONV_DIM), 0)
    conv = cb_ref[...] + cw_ref[CONV_WIDTH - 1:CONV_WIDTH, :] * xt
    for sft in range(1, CONV_WIDTH):
        rolled = pltpu.roll(xt, sft, axis=0)
        head = jnp.where(row8 < sft, pltpu.roll(halo, sft, axis=0), rolled[0:SUBLANES, :])
        tap = CONV_WIDTH - 1 - sft
        conv = conv + cw_ref[tap:tap + 1, :] * jnp.concatenate([head, rolled[SUBLANES:, :]], axis=0)
    act = _silu(conv)
    halo_ref[...] = xt[rows - SUBLANES:rows, :]

    ri = lax.broadcasted_iota(jnp.int32, (rows, rows), 0)
    ci = lax.broadcasted_iota(jnp.int32, (rows, rows), 1)
    xs, bm, cm, dt_e, cs_e, xdt, y = _ssd_chunk(act, dt_ref[...], ci <= ri, dtb_ref, alog_ref)

    cs_last = cs_e[rows - 1:rows, :]
    ecs = jnp.exp2(cs_e)
    xdte = (xdt * jnp.exp2(cs_last - cs_e)).astype(bf16)
    dec = jnp.exp2(cs_last)
    hw = D_INNER // SSM_GROUPS
    yoff = []
    for g in range(SSM_GROUPS):
        ht_g = ht_ref[:, g * hw:(g + 1) * hw]
        yoff.append(_dot(cm[:, g * D_STATE:(g + 1) * D_STATE].astype(bf16), ht_g.astype(bf16)))
        upd = _dot_tn(bm[:, g * D_STATE:(g + 1) * D_STATE].astype(bf16), xdte[:, g * hw:(g + 1) * hw])
        ht_ref[:, g * hw:(g + 1) * hw] = ht_g * dec[:, g * hw:(g + 1) * hw] + upd
    y = y + jnp.concatenate(yoff, axis=1) * ecs
    y_ref[...] = _ssd_finish(y, xs, z_ref[...], dexp_ref, norm_ref)

    @pl.when(c == pl.num_programs(1) - 1)
    def _():
        h_ref[0] = ht_ref[...].T.reshape(SSM_HEADS, SSM_HEAD_DIM, D_STATE)


def ssd_prompt(proj, conv_w, conv_b_row, dtb_row, alog_row, dexp_row, norm_row, batch, seq, rows):
    nc = seq // rows
    return pl.pallas_call(
        functools.partial(_ssd_prompt_kernel, rows=rows),
        grid=(batch, nc),
        in_specs=[
            pl.BlockSpec((rows, CONV_DIM), lambda b, c: (b * nc + c, OFF_XBC // CONV_DIM)),
            pl.BlockSpec((rows, D_INNER), lambda b, c: (b * nc + c, OFF_Z // D_INNER)),
            pl.BlockSpec((rows, LANES), lambda b, c: (b * nc + c, OFF_DT // LANES)),
            pl.BlockSpec((CONV_WIDTH, CONV_DIM), lambda b, c: (0, 0)),
            pl.BlockSpec((1, CONV_DIM), lambda b, c: (0, 0)),
            pl.BlockSpec((1, LANES), lambda b, c: (0, 0)),
            pl.BlockSpec((1, LANES), lambda b, c: (0, 0)),
            pl.BlockSpec((1, D_INNER), lambda b, c: (0, 0)),
            pl.BlockSpec((1, D_INNER), lambda b, c: (0, 0)),
        ],
        out_specs=[
            pl.BlockSpec((rows, D_INNER), lambda b, c: (b * nc + c, 0)),
            pl.BlockSpec((1, SSM_HEADS, SSM_HEAD_DIM, D_STATE), lambda b, c: (b, 0, 0, 0)),
        ],
        out_shape=[
            jax.ShapeDtypeStruct((batch * seq, D_INNER), f32),
            jax.ShapeDtypeStruct((batch, SSM_HEADS, SSM_HEAD_DIM, D_STATE), f32),
        ],
        scratch_shapes=[
            pltpu.VMEM((SUBLANES, CONV_DIM), f32),
            pltpu.VMEM((D_STATE, D_INNER), f32),
        ],
        compiler_params=_cparams(("arbitrary", "arbitrary")),
        name="ssd_prompt",
    )(proj, proj, proj, conv_w, conv_b_row, dtb_row, alog_row, dexp_row, norm_row)


def _mem_q(q_ref, g_ref, h):
    q = q_ref[:, h * MEM_HEAD_DIM:(h + 1) * MEM_HEAD_DIM]
    ms = jnp.mean(q * q, axis=-1, keepdims=True)
    return (q * lax.rsqrt(ms + EPS) * g_ref[...] * (MEM_HEAD_DIM ** -0.5)).astype(bf16)


def _mem_attend_kernel(q_ref, mk_ref, mv_ref, g_ref, o_ref, *, head_axis):
    if head_axis:
        tq = q_ref.shape[0]
        q4 = jnp.concatenate([_mem_q(q_ref, g_ref, h) for h in range(MEM_HEADS)], axis=0)
        nrow = MEM_TOKENS * MEM_HEADS
        s = _dot_nt(q4, mk_ref[0].reshape(nrow, MEM_HEAD_DIM).astype(bf16))
        rh = lax.broadcasted_iota(jnp.int32, s.shape, 0) // tq
        ch = lax.broadcasted_iota(jnp.int32, s.shape, 1) % MEM_HEADS
        s = jnp.where(rh == ch, s, NEG)
        m = jnp.max(s, axis=-1, keepdims=True)
        p = jnp.exp(s - m)
        l = jnp.sum(p, axis=-1, keepdims=True)
        o = _dot(p.astype(bf16), mv_ref[0].reshape(nrow, MEM_HEAD_DIM).astype(bf16)) / l
        o_ref[...] = jnp.concatenate([o[h * tq:(h + 1) * tq, :] for h in range(MEM_HEADS)], axis=1)
        return
    outs = []
    for h in range(MEM_HEADS):
        lo = h * MEM_HEAD_DIM
        s = _dot_nt(_mem_q(q_ref, g_ref, h), mk_ref[0, :, lo:lo + MEM_HEAD_DIM].astype(bf16))
        m = jnp.max(s, axis=-1, keepdims=True)
        p = jnp.exp(s - m)
        l = jnp.sum(p, axis=-1, keepdims=True)
        outs.append(_dot(p.astype(bf16), mv_ref[0, :, lo:lo + MEM_HEAD_DIM].astype(bf16)) / l)
    o_ref[...] = jnp.concatenate(outs, axis=1)


def mem_attend(proj, mk, mv, mq_gain_row, nseq, rows_per_seq, tq):
    w = MEM_HEADS * MEM_HEAD_DIM
    nq = rows_per_seq // tq
    head_axis = mk.ndim == 4
    if head_axis:
        kv_spec = pl.BlockSpec((1, MEM_TOKENS, MEM_HEADS, MEM_HEAD_DIM), lambda b, i: (b, 0, 0, 0))
    else:
        kv_spec = pl.BlockSpec((1, MEM_TOKENS, w), lambda b, i: (b, 0, 0))
    return pl.pallas_call(
        functools.partial(_mem_attend_kernel, head_axis=head_axis),
        grid=(nseq, nq),
        in_specs=[
            pl.BlockSpec((tq, w), lambda b, i: (b * nq + i, OFF_MQ // w)),
            kv_spec,
            kv_spec,
            pl.BlockSpec((1, MEM_HEAD_DIM), lambda b, i: (0, 0)),
        ],
        out_specs=pl.BlockSpec((tq, w), lambda b, i: (b * nq + i, 0)),
        out_shape=jax.ShapeDtypeStruct((nseq * rows_per_seq, w), f32),
        compiler_params=_cparams(("arbitrary", "arbitrary")),
        name="mem_attend",
    )(proj, mk, mv, mq_gain_row)


def _merge_kernel(x_ref, a_ref, s_ref, m_ref, ga_ref, gs_ref, gm_ref, wa_ref, ws_ref, wm_ref, wo_ref, o_ref):
    mixed = _sigmoid(ga_ref[...]) * _dot(a_ref[...].astype(bf16), wa_ref[...])
    mixed = mixed + _sigmoid(gs_ref[...]) * _dot(s_ref[...].astype(bf16), ws_ref[...])
    mixed = mixed + _sigmoid(gm_ref[...]) * _dot(m_ref[...].astype(bf16), wm_ref[...])
    o_ref[...] = x_ref[...] + _dot(mixed.astype(bf16), wo_ref[...])


def merge(x, attn_o, ssm_o, mem_o, proj, wa, ws, wm, wo, tm):
    t, d = x.shape
    row = lambda i: (i, 0)
    const = lambda i: (0, 0)
    gate_spec = lambda k: pl.BlockSpec((tm, d), lambda i: (i, OFF_GATES // d + k))
    w_spec = pl.BlockSpec((d, d), const)
    return pl.pallas_call(
        _merge_kernel,
        grid=(t // tm,),
        in_specs=[pl.BlockSpec((tm, d), row)] * 4 + [gate_spec(0), gate_spec(1), gate_spec(2)] + [w_spec] * 4,
        out_specs=pl.BlockSpec((tm, d), row),
        out_shape=jax.ShapeDtypeStruct((t, d), f32),
        compiler_params=_cparams(("arbitrary",)),
        name="merge",
    )(x, attn_o, ssm_o, mem_o, proj, proj, proj, wa, ws, wm, wo)


def _ffn_kernel(x_ref, g_ref, wg_ref, wu_ref, wd_ref, o_ref):
    x = x_ref[...]
    ms = jnp.mean(x * x, axis=-1, keepdims=True)
    h = (x * lax.rsqrt(ms + EPS) * g_ref[...]).astype(bf16)
    act = _silu(_dot(h, wg_ref[...])) * _dot(h, wu_ref[...])
    o_ref[...] = x + _dot(act.astype(bf16), wd_ref[...])


def ffn(x, gain_row, wg, wu, wd, tm):
    t, d = x.shape
    dff = wg.shape[1]
    const = lambda i: (0, 0)
    single = pl.Buffered(1)
    return pl.pallas_call(
        _ffn_kernel,
        grid=(t // tm,),
        in_specs=[
            pl.BlockSpec((tm, d), lambda i: (i, 0)),
            pl.BlockSpec((1, d), const),
            pl.BlockSpec((d, dff), const, pipeline_mode=single),
            pl.BlockSpec((d, dff), const, pipeline_mode=single),
            pl.BlockSpec((dff, d), const, pipeline_mode=single),
        ],
        out_specs=pl.BlockSpec((tm, d), lambda i: (i, 0)),
        out_shape=jax.ShapeDtypeStruct((t, d), f32),
        compiler_params=_cparams(("arbitrary",)),
        name="ffn",
    )(x, gain_row, wg, wu, wd)


def _moba_sample_kernel(pt_ref, q_ref, kn_ref, vn_ref, qg_ref, sl_ref, ck_hbm, cv_hbm, o_ref,
                        kbuf, vbuf, s_ref, sem, *, n_pages):
    b = pl.program_id(0)
    nseq = pl.num_programs(0)
    kvw = N_KV_HEADS * HEAD_DIM
    half = SAMPLE_ROWS // 2
    nrow = Q_PER_KV * N_KV_HEADS * half
    nblk = n_pages * PAGE_SIZE // MOBA_BLOCK
    ppb = MOBA_BLOCK // PAGE_SIZE
    past = n_pages * PAGE_SIZE
    slot = b % 2

    def k_copy(seq, sl, p):
        return pltpu.make_async_copy(ck_hbm.at[pt_ref[seq, p]], kbuf.at[sl, p], sem.at[0, sl])

    def v_copy(seq, sl, p):
        return pltpu.make_async_copy(cv_hbm.at[pt_ref[seq, p]], vbuf.at[sl, p], sem.at[1, sl])

    def start_all(seq, sl):
        for p in range(n_pages):
            k_copy(seq, sl, p).start()
        for p in range(n_pages):
            v_copy(seq, sl, p).start()

    @pl.when(b == 0)
    def _():
        start_all(0, 0)

    nxt = jnp.minimum(b + 1, nseq - 1)

    qb = q_ref[...]
    lane_g = lax.broadcasted_iota(jnp.int32, (SAMPLE_ROWS, kvw), 1) // HEAD_DIM
    low_half = lax.broadcasted_iota(jnp.int32, (SAMPLE_ROWS, kvw), 0) < half
    pieces = []
    for r in range(Q_PER_KV):
        slab = qb[:, r * kvw:(r + 1) * kvw]
        for g in range(N_KV_HEADS):
            pieces.append(jnp.where(lane_g == g, slab, 0.0))
    tiles = [jnp.where(low_half, pieces[2 * k], pltpu.roll(pieces[2 * k + 1], half, axis=0))
             for k in range(len(pieces) // 2)]
    qbd = jnp.concatenate(tiles, axis=0)
    ms = jnp.sum(qbd * qbd, axis=-1, keepdims=True) * (1.0 / HEAD_DIM)
    qn = qbd * lax.rsqrt(ms + EPS) * qg_ref[...]
    qs = (qn * (HEAD_DIM ** -0.5 * LOG2E)).astype(bf16)

    for p in range(n_pages):
        k_copy(b, slot, p).wait()

    def k_page(p):
        return kbuf[slot, p].reshape(kvw, PAGE_SIZE)

    lane = lax.broadcasted_iota(jnp.int32, (kvw, LANES), 1)
    kmean_t = jnp.zeros((kvw, LANES), f32)
    for j in range(nblk):
        blk = k_page(j * ppb)
        for pp in range(1, ppb):
            blk = blk + k_page(j * ppb + pp)
        kmean_t = kmean_t + jnp.where(lane == j, jnp.sum(blk, axis=-1, keepdims=True) * (1.0 / MOBA_BLOCK), 0.0)
    kh, kmid, kl = _split3(kmean_t)
    qh, qmid, ql = _split3(qn)
    gate = _dot(qh, kh) + _dot(qh, kmid) + _dot(qmid, kh) + _dot(qh, kl) + _dot(ql, kh) + _dot(qmid, kmid)
    jb = lax.broadcasted_iota(jnp.int32, (nrow, LANES), 1)
    gate = jnp.where(jb < nblk, gate, -jnp.inf)
    pen = _top3_penalty(gate, jb, 1)

    slope = sl_ref[...]
    kk = lax.broadcasted_iota(jnp.int32, (nrow, PAGE_SIZE), 1)
    pages_per_dot = 4
    for p0 in range(0, n_pages, pages_per_dot):
        for p in range(p0, p0 + pages_per_dot):
            k_copy(nxt, 1 - slot, p).start()
        k4 = jnp.concatenate([k_page(p).astype(bf16) for p in range(p0, p0 + pages_per_dot)], axis=1)
        s4 = _dot(qs, k4)
        for i in range(pages_per_dot):
            p = p0 + i
            j = p // ppb
            rel = (kk + (p * PAGE_SIZE - past)).astype(f32)
            s_ref[:, p * PAGE_SIZE:(p + 1) * PAGE_SIZE] = (
                s4[:, i * PAGE_SIZE:(i + 1) * PAGE_SIZE] + slope * rel + pen[:, j:j + 1])

    zpad = jnp.zeros((LANES - SAMPLE_ROWS, kvw), f32)
    knew = jnp.concatenate([kn_ref[...], zpad], axis=0).astype(bf16)
    vnew = jnp.concatenate([vn_ref[...], zpad], axis=0).astype(bf16)
    kn_lane = lax.broadcasted_iota(jnp.int32, (nrow, LANES), 1)
    tok = lax.broadcasted_iota(jnp.int32, (nrow, LANES), 0) % half
    s_own = _dot_nt(qs, knew) + slope * kn_lane.astype(f32)
    s_ref[:, past:past + LANES] = jnp.where(kn_lane <= tok, s_own, NEG)

    m = jnp.max(s_ref[...], axis=-1, keepdims=True)
    p_own = jnp.exp2(s_ref[:, past:past + LANES] - m)
    n_acc = 4
    psums = [p_own] + [None] * (n_acc - 1)
    accs = [_dot(p_own.astype(bf16), vnew)] + [None] * (n_acc - 1)

    for p in range(n_pages):
        v_copy(b, slot, p).wait()

    for p0 in range(0, n_pages, ppb):
        for p in range(p0, p0 + ppb):
            v_copy(nxt, 1 - slot, p).start()
        pp = jnp.exp2(s_ref[:, p0 * PAGE_SIZE:(p0 + ppb) * PAGE_SIZE] - m)
        vb = jnp.concatenate([vbuf[slot, p].reshape(kvw, PAGE_SIZE).astype(bf16) for p in range(p0, p0 + ppb)], axis=1)
        pv = _dot_nt(pp.astype(bf16), vb)
        a = (p0 // ppb) % n_acc
        pl_sum = pp[:, :PAGE_SIZE]
        for i in range(1, ppb):
            pl_sum = pl_sum + pp[:, i * PAGE_SIZE:(i + 1) * PAGE_SIZE]
        psums[a] = pl_sum if psums[a] is None else psums[a] + pl_sum
        accs[a] = pv if accs[a] is None else accs[a] + pv
    acc = (accs[0] + accs[1]) + (accs[2] + accs[3])
    psum = (psums[0] + psums[1]) + (psums[2] + psums[3])
    o = acc / jnp.sum(psum, axis=-1, keepdims=True)

    @pl.when(b == nseq - 1)
    def _():
        for p in range(n_pages):
            k_copy(nxt, 1 - slot, p).wait()
        for p in range(n_pages):
            v_copy(nxt, 1 - slot, p).wait()

    slabs = []
    for r in range(Q_PER_KV):
        slab = jnp.zeros((SAMPLE_ROWS, kvw), f32)
        for g in range(N_KV_HEADS):
            piece = r * N_KV_HEADS + g
            tile = o[(piece // 2) * SAMPLE_ROWS:(piece // 2 + 1) * SAMPLE_ROWS, :]
            if piece % 2:
                tile = pltpu.roll(tile, half, axis=0)
            slab = slab + jnp.where(lane_g == g, tile, 0.0)
        slabs.append(slab)
    o_ref[...] = jnp.concatenate(slabs, axis=1)


def moba_sample(page_table, proj, kn_new, ck, cv, q_gain_row, slope_rows, nseq):
    kvw = N_KV_HEADS * HEAD_DIM
    qw = N_HEADS * HEAD_DIM
    n_pages = page_table.shape[1]
    nrow = Q_PER_KV * N_KV_HEADS * (SAMPLE_ROWS // 2)
    past = n_pages * PAGE_SIZE
    grid_spec = pltpu.PrefetchScalarGridSpec(
        num_scalar_prefetch=1,
        grid=(nseq,),
        in_specs=[
            pl.BlockSpec((SAMPLE_ROWS, qw), lambda b, pt: (b, OFF_Q // qw)),
            pl.BlockSpec((SAMPLE_ROWS, kvw), lambda b, pt: (b, 0)),
            pl.BlockSpec((SAMPLE_ROWS, kvw), lambda b, pt: (b, OFF_V // kvw)),
            pl.BlockSpec((1, kvw), lambda b, pt: (0, 0)),
            pl.BlockSpec((nrow, LANES), lambda b, pt: (0, 0)),
            pl.BlockSpec(memory_space=pl.ANY),
            pl.BlockSpec(memory_space=pl.ANY),
        ],
        out_specs=pl.BlockSpec((SAMPLE_ROWS, qw), lambda b, pt: (b, 0)),
        scratch_shapes=[
            pltpu.VMEM((2, n_pages, N_KV_HEADS, HEAD_DIM, PAGE_SIZE), f32),
            pltpu.VMEM((2, n_pages, N_KV_HEADS, HEAD_DIM, PAGE_SIZE), f32),
            pltpu.VMEM((nrow, past + LANES), f32),
            pltpu.SemaphoreType.DMA((2, 2)),
        ],
    )
    return pl.pallas_call(
        functools.partial(_moba_sample_kernel, n_pages=n_pages),
        grid_spec=grid_spec,
        out_shape=jax.ShapeDtypeStruct((nseq * SAMPLE_ROWS, qw), f32),
        compiler_params=_cparams(("arbitrary",)),
        name="moba_sample",
    )(page_table, proj, kn_new, proj, q_gain_row, slope_rows, ck, cv)


def _ssd_sample_kernel(ext_ref, z_ref, dt_ref, h0_ref, cw_ref, cb_ref, dtb_ref, alog_ref, dexp_ref, norm_ref,
                       y_ref, h_ref, scr_ref, *, nseq_step, n_new):
    rows = nseq_step * SAMPLE_ROWS
    scr_ref[pl.ds(0, rows), :] = ext_ref[...]
    scr_ref[pl.ds(rows, SUBLANES), :] = jnp.zeros((SUBLANES, CONV_DIM), f32)
    act = _ssd_conv_act(scr_ref, rows, cw_ref, cb_ref, base=0)

    ri = lax.broadcasted_iota(jnp.int32, (rows, rows), 0)
    ci = lax.broadcasted_iota(jnp.int32, (rows, rows), 1)
    allowed = jnp.logical_and(ri // SAMPLE_ROWS == ci // SAMPLE_ROWS,
                              jnp.logical_and(ci % SAMPLE_ROWS <= ri % SAMPLE_ROWS, ci % SAMPLE_ROWS < n_new))
    xs, bm, cm, dt_e, cs_e, xdt, y = _ssd_chunk(act, dt_ref[...], allowed, dtb_ref, alog_ref)

    valid = lax.broadcasted_iota(jnp.int32, (SAMPLE_ROWS, 1), 0) < n_new
    ecs = jnp.exp2(cs_e)
    hw = D_INNER // SSM_GROUPS
    hpg = SSM_HEADS // SSM_GROUPS
    yoff_rows = []
    for i in range(nseq_step):
        lo = i * SAMPLE_ROWS
        cs_i = cs_e[lo:lo + SAMPLE_ROWS, :]
        cs_last = cs_e[lo + n_new - 1:lo + n_new, :]
        xdte = jnp.where(valid, xdt[lo:lo + SAMPLE_ROWS, :] * jnp.exp2(cs_last - cs_i), 0.0)
        yg = []
        upd = []
        for g in range(SSM_GROUPS):
            h0g = h0_ref[i, g * hpg:(g + 1) * hpg].reshape(hw, D_STATE)
            yg.append(_dot_nt(cm[lo:lo + SAMPLE_ROWS, g * D_STATE:(g + 1) * D_STATE], h0g))
            upd.append(_dot_tn(xdte[:, g * hw:(g + 1) * hw], bm[lo:lo + SAMPLE_ROWS, g * D_STATE:(g + 1) * D_STATE]))
        yoff_rows.append(jnp.concatenate(yg, axis=1))
        dec_col = jnp.broadcast_to(jnp.exp2(cs_last), (SUBLANES, D_INNER)).T[:, 0:1]
        h_new = h0_ref[i].reshape(D_INNER, D_STATE) * dec_col + jnp.concatenate(upd, axis=0)
        h_ref[i] = h_new.reshape(SSM_HEADS, SSM_HEAD_DIM, D_STATE)
    y = y + jnp.concatenate(yoff_rows, axis=0) * ecs
    y_ref[...] = _ssd_finish(y, xs, z_ref[...], dexp_ref, norm_ref)


def ssd_sample(ext, proj, h0, conv_w, conv_b_row, dtb_row, alog_row, dexp_row, norm_row, nseq, nseq_step, n_new):
    rows = nseq_step * SAMPLE_ROWS
    const = lambda i: (0, 0)
    state_spec = pl.BlockSpec((nseq_step, SSM_HEADS, SSM_HEAD_DIM, D_STATE), lambda i: (i, 0, 0, 0))
    return pl.pallas_call(
        functools.partial(_ssd_sample_kernel, nseq_step=nseq_step, n_new=n_new),
        grid=(nseq // nseq_step,),
        in_specs=[
            pl.BlockSpec((rows, CONV_DIM), lambda i: (i, 0)),
            pl.BlockSpec((rows, D_INNER), lambda i: (i, OFF_Z // D_INNER)),
            pl.BlockSpec((rows, LANES), lambda i: (i, OFF_DT // LANES)),
            state_spec,
            pl.BlockSpec((CONV_WIDTH, CONV_DIM), const),
            pl.BlockSpec((1, CONV_DIM), const),
            pl.BlockSpec((1, LANES), const),
            pl.BlockSpec((1, LANES), const),
            pl.BlockSpec((1, D_INNER), const),
            pl.BlockSpec((1, D_INNER), const),
        ],
        out_specs=[pl.BlockSpec((rows, D_INNER), lambda i: (i, 0)), state_spec],
        out_shape=[
            jax.ShapeDtypeStruct((nseq * SAMPLE_ROWS, D_INNER), f32),
            jax.ShapeDtypeStruct(h0.shape, f32),
        ],
        scratch_shapes=[pltpu.VMEM((rows + SUBLANES, CONV_DIM), f32)],
        compiler_params=_cparams(("arbitrary",)),
        name="ssd_sample",
    )(ext, proj, proj, h0, conv_w, conv_b_row, dtb_row, alog_row, dexp_row, norm_row)


IN_SIZES = (N_HEADS * HEAD_DIM, N_KV_HEADS * HEAD_DIM, N_KV_HEADS * HEAD_DIM, D_INNER, CONV_DIM, SSM_HEADS,
            MEM_HEADS * MEM_HEAD_DIM, 3 * D_MODEL)
TM = 512
TM_PROJ = 1024
TN_PROJ = 2944
TQ = 256
SSD_ROWS = 256
SSD_SAMPLE_SEQS = 8


def _pad_lanes(v):
    return jnp.pad(v, (0, LANES - v.shape[0])).reshape(1, LANES)


def kernel(x_prompt, x_sample, mem_prompt, cache_k, cache_v, page_table, state_conv, state_ssm, cache_mem_k, cache_mem_v, norm_mix, w_in, q_norm, k_norm, conv_w, conv_b, dt_bias, a_log, d_skip, ssm_norm, mem_norm, w_mem_kv, mq_norm, mk_norm, w_attn_br, w_ssm_br, w_mem_br, w_out, norm_ffn, w_gate, w_up, w_down):
    assert w_in.shape[0] == 1, "single layer"
    assert x_sample.shape[1] <= SAMPLE_ROWS // 2, "sample MoBA packs two heads' tokens into one row tile"
    batch, seq, d = x_prompt.shape
    nseq, n_new, _ = x_sample.shape
    kvw = N_KV_HEADS * HEAD_DIM
    qw = N_HEADS * HEAD_DIM

    offs = np.cumsum(IN_SIZES)[:-1].tolist()
    wq, wk, wv, wz, wxbc, wdt, wmq, wgates = jnp.split(w_in[0], offs, axis=1)
    wdt = jnp.pad(wdt, ((0, 0), (0, LANES - SSM_HEADS)))
    tail = [wz, wxbc, wmq, wgates, wk, wv, wdt]
    w_prompt = jnp.concatenate([wq] + tail, axis=1).astype(bf16)
    wq_rgd = wq.reshape(d, N_KV_HEADS, Q_PER_KV, HEAD_DIM).transpose(0, 2, 1, 3).reshape(d, qw)
    w_sample = jnp.concatenate([wq_rgd] + tail, axis=1).astype(bf16)
    wa = w_attn_br[0].astype(bf16)
    wa_rgd = w_attn_br[0].reshape(N_KV_HEADS, Q_PER_KV, HEAD_DIM, d).transpose(1, 0, 2, 3).reshape(qw, d).astype(bf16)
    ws, wm, wo = w_ssm_br[0].astype(bf16), w_mem_br[0].astype(bf16), w_out[0].astype(bf16)
    wg, wu, wd = w_gate[0].astype(bf16), w_up[0].astype(bf16), w_down[0].astype(bf16)

    k_gain = jnp.tile(k_norm[0], N_KV_HEADS).reshape(1, kvw)
    conv_b_row = conv_b[0].reshape(1, CONV_DIM)
    dtb_row, alog_row = _pad_lanes(dt_bias[0]), _pad_lanes(a_log[0])
    dexp_row = jnp.repeat(d_skip[0], SSM_HEAD_DIM).reshape(1, D_INNER)
    ssm_norm_row = ssm_norm[0].reshape(1, D_INNER)
    mq_gain = mq_norm[0].reshape(1, MEM_HEAD_DIM)
    ffn_gain = norm_ffn[0].reshape(1, d)
    slopes2 = jnp.exp2(-8.0 * jnp.arange(1, N_HEADS + 1, dtype=f32) / N_HEADS) * LOG2E

    xp = x_prompt.reshape(batch * seq, d)
    proj = inproj(xp, norm_mix[0], w_prompt, TM_PROJ, TN_PROJ)
    knt, vt32, kb, vt, kmean = kprep(proj, k_gain, batch, seq, 2048)
    kmean_g = kmean.reshape(batch, seq // MOBA_BLOCK, N_KV_HEADS, HEAD_DIM).transpose(0, 2, 1, 3)
    slope_cols = jnp.repeat(slopes2.reshape(N_KV_HEADS, Q_PER_KV), TQ, axis=1).reshape(N_KV_HEADS, 1, Q_PER_KV * TQ)
    q_gain_col = jnp.broadcast_to(q_norm[0][:, None], (HEAD_DIM, TQ))
    attn_o = moba_prompt(proj, kb, vt, kmean_g, q_gain_col, slope_cols, batch, seq, TQ)
    ssm_o, h_prompt = ssd_prompt(proj, conv_w[0], conv_b_row, dtb_row, alog_row, dexp_row, ssm_norm_row,
                                 batch, seq, SSD_ROWS)
    mem_kv = inproj(mem_prompt.reshape(batch * MEM_TOKENS, d), mem_norm[0], w_mem_kv[0].astype(bf16), TM, qw)
    mk = headnorm(mem_kv, 0, qw, jnp.tile(mk_norm[0], MEM_HEADS).reshape(1, qw), MEM_HEAD_DIM, TM)
    mv = mem_kv[:, qw:]
    mem_o = mem_attend(proj, mk.reshape(batch, MEM_TOKENS, qw), mv.reshape(batch, MEM_TOKENS, qw), mq_gain,
                       batch, seq, TM)
    x1 = merge(xp, attn_o, ssm_o, mem_o, proj, wa, ws, wm, wo, TM)
    y_prompt = ffn(x1, ffn_gain, wg, wu, wd, TM).reshape(batch, seq, d)

    k_prompt = jnp.transpose(knt.reshape(batch, N_KV_HEADS, HEAD_DIM, seq), (0, 3, 1, 2))[None]
    v_prompt = jnp.transpose(vt32.reshape(batch, N_KV_HEADS, HEAD_DIM, seq), (0, 3, 1, 2))[None]
    conv_prompt = proj.reshape(batch, seq, N_PROJ)[:, seq - (CONV_WIDTH - 1):, OFF_XBC:OFF_XBC + CONV_DIM][None]
    mem_k_prompt = mk.reshape(1, batch, MEM_TOKENS, MEM_HEADS, MEM_HEAD_DIM)
    mem_v_prompt = mv.reshape(1, batch, MEM_TOKENS, MEM_HEADS, MEM_HEAD_DIM)

    xs = jnp.pad(x_sample, ((0, 0), (0, SAMPLE_ROWS - n_new), (0, 0))).reshape(nseq * SAMPLE_ROWS, d)
    proj_s = inproj(xs, norm_mix[0], w_sample, TM_PROJ, TN_PROJ)
    kn_s = headnorm(proj_s, OFF_K // kvw, kvw, k_gain, HEAD_DIM, TM)
    slope_rows = jnp.broadcast_to(
        jnp.repeat(slopes2.reshape(N_KV_HEADS, Q_PER_KV).T.reshape(-1), SAMPLE_ROWS // 2)[:, None],
        (N_HEADS * (SAMPLE_ROWS // 2), LANES))
    ck_t = jnp.transpose(cache_k[0], (0, 2, 3, 1))
    cv_t = jnp.transpose(cache_v[0], (0, 2, 3, 1))
    attn_s = moba_sample(page_table, proj_s, kn_s, ck_t, cv_t, jnp.tile(q_norm[0], N_KV_HEADS).reshape(1, kvw),
                         slope_rows, nseq)
    xbc_new = proj_s[:, OFF_XBC:OFF_XBC + CONV_DIM].reshape(nseq, SAMPLE_ROWS, CONV_DIM)[:, :n_new]
    ext = jnp.concatenate(
        [state_conv[0], xbc_new, jnp.zeros((nseq, SAMPLE_ROWS - (CONV_WIDTH - 1) - n_new, CONV_DIM), f32)], axis=1)
    ssm_s, h_sample = ssd_sample(ext.reshape(nseq * SAMPLE_ROWS, CONV_DIM), proj_s, state_ssm[0], conv_w[0], conv_b_row,
                                 dtb_row, alog_row, dexp_row, ssm_norm_row, nseq, SSD_SAMPLE_SEQS, n_new)
    mem_s = mem_attend(proj_s, cache_mem_k[0], cache_mem_v[0], mq_gain, nseq, SAMPLE_ROWS, SAMPLE_ROWS)
    x1s = merge(xs, attn_s, ssm_s, mem_s, proj_s, wa_rgd, ws, wm, wo, TM)
    y_sample = ffn(x1s, ffn_gain, wg, wu, wd, TM).reshape(nseq, SAMPLE_ROWS, d)[:, :n_new]

    k_sample = kn_s.reshape(nseq, SAMPLE_ROWS, N_KV_HEADS, HEAD_DIM)[:, :n_new][None]
    v_sample = proj_s[:, OFF_V:OFF_V + kvw].reshape(nseq, SAMPLE_ROWS, N_KV_HEADS, HEAD_DIM)[:, :n_new][None]
    conv_sample = ext[:, n_new:n_new + CONV_WIDTH - 1][None]

    return (y_prompt, y_sample, k_prompt, v_prompt, conv_prompt, h_prompt[None], mem_k_prompt, mem_v_prompt,
            k_sample, v_sample, conv_sample, h_sample[None])
```

```python
import functools
import math

import jax
import jax.numpy as jnp
import numpy as np
from jax import lax
from jax.experimental import pallas as pl
from jax.experimental.pallas import tpu as pltpu

f32 = jnp.float32
bf16 = jnp.bfloat16

D_MODEL = 1024
N_HEADS = 16
N_KV_HEADS = 4
HEAD_DIM = 64
Q_PER_KV = N_HEADS // N_KV_HEADS
MOBA_BLOCK = 256
MOBA_TOPK = 3
SSM_HEADS = 16
SSM_HEAD_DIM = 64
D_INNER = SSM_HEADS * SSM_HEAD_DIM
SSM_GROUPS = 4
D_STATE = 128
CONV_WIDTH = 4
CONV_DIM = D_INNER + 2 * SSM_GROUPS * D_STATE
MEM_TOKENS = 256
MEM_HEADS = 4
MEM_HEAD_DIM = 256
D_FF = 2816
EPS = 1e-6
PAGE_SIZE = 128

LANES = 128
SUBLANES = 8
LOG2E = 1.4426950408889634
NEG = -1e30
VMEM_LIMIT = 56 * 1024 * 1024

OFF_Q, OFF_Z, OFF_XBC, OFF_MQ, OFF_GATES, OFF_K, OFF_V, OFF_DT = 0, 1024, 2048, 4096, 5120, 8192, 8448, 8704
N_PROJ = 8832
SAMPLE_ROWS = 8
MOBA_GROUP = 4


def _cparams(sem):
    return pltpu.CompilerParams(dimension_semantics=sem, vmem_limit_bytes=VMEM_LIMIT)


def _split3(x):
    h = x.astype(bf16)
    r = x - h.astype(f32)
    m = r.astype(bf16)
    l = (r - m.astype(f32)).astype(bf16)
    return h, m, l


def _dot(a, b):
    return jnp.dot(a, b, preferred_element_type=f32)


def _dot_nt(a, b):
    return lax.dot_general(a, b, (((1,), (1,)), ((), ())), preferred_element_type=f32)


def _dot_tn(a, b):
    return lax.dot_general(a, b, (((0,), (0,)), ((), ())), preferred_element_type=f32)


def _dot_exact_rhs(x, m_bf16):
    h, m, l = _split3(x)
    return _dot(h, m_bf16) + _dot(m, m_bf16) + _dot(l, m_bf16)


def _dot_exact_lhs(m_bf16, x):
    h, m, l = _split3(x)
    return _dot(m_bf16, h) + _dot(m_bf16, m) + _dot(m_bf16, l)


def _sigmoid(x):
    return 0.5 * jnp.tanh(0.5 * x) + 0.5


def _silu(x):
    return x * _sigmoid(x)


def _softplus(x):
    return jnp.maximum(x, 0.0) + jnp.log(1.0 + jnp.exp(-jnp.abs(x)))


def _inproj_kernel(x_ref, g_ref, w_ref, o_ref):
    x = x_ref[...]
    ms = jnp.mean(x * x, axis=-1, keepdims=True)
    xn = (x * lax.rsqrt(ms + EPS) * g_ref[...]).astype(bf16)
    o_ref[...] = _dot(xn, w_ref[...])


def inproj(x, gain, w, tm, tn):
    t, d = x.shape
    n = w.shape[1]
    return pl.pallas_call(
        _inproj_kernel,
        grid=(n // tn, t // tm),
        in_specs=[
            pl.BlockSpec((tm, d), lambda j, i: (i, 0)),
            pl.BlockSpec((1, d), lambda j, i: (0, 0)),
            pl.BlockSpec((d, tn), lambda j, i: (0, j)),
        ],
        out_specs=pl.BlockSpec((tm, tn), lambda j, i: (i, j)),
        out_shape=jax.ShapeDtypeStruct((t, n), f32),
        compiler_params=_cparams(("arbitrary", "arbitrary")),
        name="inproj",
    )(x, gain.reshape(1, d), w)


def _seg_ones(width, seg):
    r = lax.broadcasted_iota(jnp.int32, (width, width), 0) // seg
    c = lax.broadcasted_iota(jnp.int32, (width, width), 1) // seg
    return jnp.where(r == c, 1.0, 0.0).astype(bf16)


def _seg_rmsnorm(x, gain_row, seg):
    ss = _dot_exact_rhs(x * x, _seg_ones(x.shape[1], seg))
    return x * lax.rsqrt(ss * (1.0 / seg) + EPS) * gain_row


def _headnorm_kernel(x_ref, g_ref, o_ref, *, seg):
    o_ref[...] = _seg_rmsnorm(x_ref[...], g_ref[...], seg)


def headnorm(x, col_block, width, gain_row, seg, tm):
    t = x.shape[0]
    return pl.pallas_call(
        functools.partial(_headnorm_kernel, seg=seg),
        grid=(t // tm,),
        in_specs=[
            pl.BlockSpec((tm, width), lambda i: (i, col_block)),
            pl.BlockSpec((1, width), lambda i: (0, 0)),
        ],
        out_specs=pl.BlockSpec((tm, width), lambda i: (i, 0)),
        out_shape=jax.ShapeDtypeStruct((t, width), f32),
        compiler_params=_cparams(("arbitrary",)),
        name="headnorm",
    )(x, gain_row)


def _kprep_kernel(k_ref, v_ref, g_ref, knt_ref, vt32_ref, ka_ref, vt_ref, km_ref):
    kn = _seg_rmsnorm(k_ref[...], g_ref[...], HEAD_DIM)
    knt_ref[0] = kn.T
    kn_b = kn.astype(bf16)
    rows = kn.shape[0]
    kvw = kn.shape[1]
    r = lax.broadcasted_iota(jnp.int32, (kvw, LANES), 0)
    c = lax.broadcasted_iota(jnp.int32, (kvw, LANES), 1)
    lane = lax.broadcasted_iota(jnp.int32, (rows, LANES), 1)
    pos = (lax.broadcasted_iota(jnp.int32, (rows, LANES), 0) % MOBA_BLOCK).astype(f32)
    pos_lanes = jnp.where(jnp.logical_and(lane >= HEAD_DIM, lane < HEAD_DIM + 3), pos, 0.0)
    for g in range(N_KV_HEADS):
        pick = jnp.where(jnp.logical_and(r == c + g * HEAD_DIM, c < HEAD_DIM), 1.0, 0.0).astype(bf16)
        ka_ref[0, g] = (_dot(kn_b, pick) + pos_lanes).astype(bf16)
    nb = kn.shape[0] // MOBA_BLOCK
    km_ref[0] = jnp.sum(kn.reshape(nb, MOBA_BLOCK, kn.shape[1]), axis=1) * (1.0 / MOBA_BLOCK)
    vt = v_ref[...].T
    vt32_ref[0] = vt
    vt_ref[0] = vt.astype(bf16)


def kprep(proj, k_gain_row, batch, seq, tm):
    kvw = N_KV_HEADS * HEAD_DIM
    nt = seq // tm
    nb = tm // MOBA_BLOCK
    return pl.pallas_call(
        _kprep_kernel,
        grid=(batch, nt),
        in_specs=[
            pl.BlockSpec((tm, kvw), lambda b, i: (b * nt + i, OFF_K // kvw)),
            pl.BlockSpec((tm, kvw), lambda b, i: (b * nt + i, OFF_V // kvw)),
            pl.BlockSpec((1, kvw), lambda b, i: (0, 0)),
        ],
        out_specs=[
            pl.BlockSpec((1, kvw, tm), lambda b, i: (b, 0, i)),
            pl.BlockSpec((1, kvw, tm), lambda b, i: (b, 0, i)),
            pl.BlockSpec((1, N_KV_HEADS, tm, LANES), lambda b, i: (b, 0, i, 0)),
            pl.BlockSpec((1, kvw, tm), lambda b, i: (b, 0, i)),
            pl.BlockSpec((1, nb, kvw), lambda b, i: (b, i, 0)),
        ],
        out_shape=[
            jax.ShapeDtypeStruct((batch, kvw, seq), f32),
            jax.ShapeDtypeStruct((batch, kvw, seq), f32),
            jax.ShapeDtypeStruct((batch, N_KV_HEADS, seq, LANES), bf16),
            jax.ShapeDtypeStruct((batch, kvw, seq), bf16),
            jax.ShapeDtypeStruct((batch, seq // MOBA_BLOCK, kvw), f32),
        ],
        compiler_params=_cparams(("arbitrary", "arbitrary")),
        name="kprep",
    )(proj, proj, k_gain_row)


def _top3_penalty(gate, jb, axis):
    n = gate.shape[axis]
    pen = jnp.full(gate.shape, NEG, f32)
    for _ in range(MOBA_TOPK):
        m = jnp.max(gate, axis=axis, keepdims=True)
        idx = jnp.min(jnp.where(gate == m, jb, n), axis=axis, keepdims=True)
        idx = jnp.where(m > -jnp.inf, idx, n)
        hit = jb == idx
        pen = jnp.where(hit, 0.0, pen)
        gate = jnp.where(hit, -jnp.inf, gate)
    return pen


def _moba_prompt_kernel(q_ref, kb_ref, vt_ref, km_ref, qg_ref, sl_ref, o_ref,
                        qbd_ref, pen_ref, acc_ref, m_ref, s_ref, cm_ref, *, tq):
    qt = pl.program_id(2)
    nb_tile = tq // MOBA_BLOCK
    own = qt * nb_tile
    cols = Q_PER_KV * tq
    nblk = km_ref.shape[2]
    q_in_tile = lax.broadcasted_iota(jnp.int32, (1, cols), 1) % tq
    own_col = own + q_in_tile // MOBA_BLOCK

    qT = q_ref[...].T
    parts = []
    for r in range(Q_PER_KV):
        xr = qT[r * HEAD_DIM:(r + 1) * HEAD_DIM, :]
        ms = jnp.mean(xr * xr, axis=0, keepdims=True)
        parts.append(xr * lax.rsqrt(ms + EPS) * qg_ref[...])
    qn = jnp.concatenate(parts, axis=1)

    km = km_ref[0, 0]
    kh, kmid, kl = _split3(km)
    qh, qmid, ql = _split3(qn)
    gate = _dot(kh, qh) + _dot(kh, qmid) + _dot(kmid, qh) + _dot(kh, ql) + _dot(kl, qh) + _dot(kmid, qmid)
    jb = lax.broadcasted_iota(jnp.int32, (nblk, cols), 0)
    gate = jnp.where(jb < own_col, gate, -jnp.inf)
    pen_ref[...] = jnp.where(jb == own_col, 0.0, _top3_penalty(gate, jb, 0))

    slope = sl_ref[0]
    s_hi, s_mid, s_lo = _split3(slope)
    rowi = lax.broadcasted_iota(jnp.int32, (HEAD_DIM, cols), 0)
    slope_rows = jnp.where(rowi == 0, s_hi.astype(f32), jnp.where(rowi == 1, s_mid.astype(f32),
                           jnp.where(rowi == 2, s_lo.astype(f32), 0.0)))
    qbd_ref[...] = jnp.concatenate([qn * (HEAD_DIM ** -0.5 * LOG2E), slope_rows], axis=0).astype(bf16)

    kk = lax.broadcasted_iota(jnp.int32, (MOBA_BLOCK, cols), 0)
    ones = jnp.ones((2 * SUBLANES, MOBA_BLOCK), bf16)

    def blk(j):
        return pl.ds(pl.multiple_of(j * MOBA_BLOCK, MOBA_BLOCK), MOBA_BLOCK)

    def v_aug(j):
        return jnp.concatenate([vt_ref[0, :, blk(j)], ones], axis=0)

    def shift_of(j):
        return slope * ((j - own) * MOBA_BLOCK).astype(f32) + pen_ref[pl.ds(j, 1), :]

    def stage_scores(j, buf):
        s = _dot(kb_ref[0, 0, blk(j), :], qbd_ref[...])
        s_ref[buf] = s
        cm_ref[buf] = jnp.max(s, axis=0, keepdims=True)

    def stage_accum(j, buf, valid=None):
        shift = shift_of(j)
        if valid is not None:
            shift = shift + jnp.where(valid, 0.0, NEG)
        m_old = m_ref[...]
        m_new = jnp.maximum(m_old, cm_ref[buf] + shift)
        p = jnp.exp2(s_ref[buf] - (m_new - shift)).astype(bf16)
        acc_ref[...] = acc_ref[...] * jnp.exp2(m_old - m_new) + _dot(v_aug(j), p)
        m_ref[...] = m_new

    last = jnp.maximum(own - 1, 0)
    stage_scores(0, 1)
    col_blk = q_in_tile // MOBA_BLOCK
    causal = kk <= q_in_tile % MOBA_BLOCK
    ss, shifts = [], []
    for t in range(nb_tile):
        s = _dot(kb_ref[0, 0, blk(own + t), :], qbd_ref[...])
        visible = jnp.logical_or(col_blk > t, jnp.logical_and(col_blk == t, causal))
        ss.append(jnp.where(visible, s, NEG))
        shifts.append(shift_of(own + t) if nb_tile > 1 else jnp.zeros((1, cols), f32))
    m0 = None
    for s, sh in zip(ss, shifts):
        mj = jnp.max(s, axis=0, keepdims=True) + sh
        m0 = mj if m0 is None else jnp.maximum(m0, mj)
    pv0 = None
    for t, (s, sh) in enumerate(zip(ss, shifts)):
        pv = _dot(v_aug(own + t), jnp.exp2(s - (m0 - sh)).astype(bf16))
        pv0 = pv if pv0 is None else pv0 + pv
    acc_ref[...] = pv0
    m_ref[...] = m0


    def pair(j0):
        j1, j2 = j0 + 1, j0 + 2
        stage_scores(jnp.minimum(j1, last), 0)
        stage_accum(j0, 1)
        stage_scores(jnp.minimum(j2, last), 1)
        stage_accum(jnp.minimum(j1, last), 0, valid=j1 < own)

    def body8(i, c):
        for k in range(4):
            pair(8 * i + 2 * k)
        return c

    def body4(i, c):
        pair(8 * n8)
        pair(8 * n8 + 2)
        return c

    def body2(i, c):
        pair(8 * n8 + 4 * n4 + 2 * i)
        return c

    n8 = own // 8
    n4 = (own - 8 * n8) // 4
    lax.fori_loop(0, n8, body8, 0)
    lax.fori_loop(0, n4, body4, 0)
    lax.fori_loop(0, (own - 8 * n8 - 4 * n4 + 1) // 2, body2, 0)

    acc = acc_ref[...]
    o = acc[0:HEAD_DIM, :] / acc[HEAD_DIM:HEAD_DIM + 1, :]
    oT = jnp.concatenate([o[:, r * tq:(r + 1) * tq] for r in range(Q_PER_KV)], axis=0)
    o_ref[...] = oT.T


def moba_prompt(proj, kb, vt, kmean_g, q_gain_col, slopes2, batch, seq, tq):
    nq = seq // tq
    qw = Q_PER_KV * HEAD_DIM
    kvw = N_KV_HEADS * HEAD_DIM
    nblk = seq // MOBA_BLOCK
    cols = Q_PER_KV * tq
    return pl.pallas_call(
        functools.partial(_moba_prompt_kernel, tq=tq),
        grid=(batch, N_KV_HEADS, nq),
        in_specs=[
            pl.BlockSpec((tq, qw), lambda b, g, i: (b * nq + i, OFF_Q // qw + g)),
            pl.BlockSpec((1, 1, seq, LANES), lambda b, g, i: (b, g, 0, 0)),
            pl.BlockSpec((1, HEAD_DIM, seq), lambda b, g, i: (b, g, 0)),
            pl.BlockSpec((1, 1, nblk, HEAD_DIM), lambda b, g, i: (b, g, 0, 0)),
            pl.BlockSpec((HEAD_DIM, tq), lambda b, g, i: (0, 0)),
            pl.BlockSpec((1, 1, cols), lambda b, g, i: (g, 0, 0)),
        ],
        out_specs=pl.BlockSpec((tq, qw), lambda b, g, i: (b * nq + i, g)),
        out_shape=jax.ShapeDtypeStruct((batch * seq, N_HEADS * HEAD_DIM), f32),
        scratch_shapes=[
            pltpu.VMEM((LANES, cols), bf16),
            pltpu.VMEM((nblk, cols), f32),
            pltpu.VMEM((HEAD_DIM + 2 * SUBLANES, cols), f32),
            pltpu.VMEM((1, cols), f32),
            pltpu.VMEM((2, MOBA_BLOCK, cols), f32),
            pltpu.VMEM((2, 1, cols), f32),
        ],
        compiler_params=_cparams(("arbitrary", "arbitrary", "arbitrary")),
        name="moba_prompt",
    )(proj, kb, vt, kmean_g, q_gain_col, slopes2)


def _head_expand():
    r = lax.broadcasted_iota(jnp.int32, (LANES, D_INNER), 0)
    c = lax.broadcasted_iota(jnp.int32, (LANES, D_INNER), 1) // SSM_HEAD_DIM
    return jnp.where(r == c, 1.0, 0.0).astype(bf16)


def _ssd_conv_act(ext_ref, rows, cw_ref, cb_ref, base):
    conv = cb_ref[...] + cw_ref[0:1, :] * ext_ref[pl.ds(base, rows), :]
    for i in range(1, CONV_WIDTH):
        conv = conv + cw_ref[i:i + 1, :] * ext_ref[pl.ds(base + i, rows), :]
    return _silu(conv)


def _ssd_chunk(act, dt_raw, allowed, dtb_ref, alog_ref):
    rows = act.shape[0]
    xs = act[:, :D_INNER]
    gn = SSM_GROUPS * D_STATE
    bm = act[:, D_INNER:D_INNER + gn]
    cm = act[:, D_INNER + gn:]
    dt = _softplus(dt_raw + dtb_ref[...])
    a = -jnp.exp(alog_ref[...])
    mask_b = jnp.where(allowed, 1.0, 0.0).astype(bf16)
    cs = _dot_exact_lhs(mask_b, dt * (a * LOG2E))
    expand = _head_expand()
    dt_e = _dot_exact_rhs(dt, expand)
    cs_e = _dot_exact_rhs(cs, expand)
    xdt = xs * dt_e
    csT = cs.T
    xdt_b = xdt.astype(bf16)
    ypairs = []
    lane = lax.broadcasted_iota(jnp.int32, (rows, LANES), 1)
    for g in range(SSM_GROUPS):
        cb = _dot_nt(cm[:, g * D_STATE:(g + 1) * D_STATE].astype(bf16), bm[:, g * D_STATE:(g + 1) * D_STATE].astype(bf16))
        hpg = SSM_HEADS // SSM_GROUPS
        for pair in range(hpg // 2):
            res = []
            for k in range(2):
                h = g * hpg + pair * 2 + k
                seg = cs[:, h:h + 1] - csT[h:h + 1, :]
                m = (cb * jnp.exp2(jnp.where(allowed, seg, NEG))).astype(bf16)
                lo = (g * hpg + pair * 2) * SSM_HEAD_DIM
                res.append(_dot(m, xdt_b[:, lo:lo + LANES]))
            ypairs.append(jnp.where(lane < SSM_HEAD_DIM, res[0], res[1]))
    y_diag = jnp.concatenate(ypairs, axis=1)
    return xs, bm, cm, dt_e, cs_e, xdt, y_diag


def _ssd_finish(y, xs, z, dexp_ref, norm_ref):
    y = (y + dexp_ref[...] * xs) * _silu(z)
    gw = D_INNER // SSM_GROUPS
    outs = []
    for g in range(SSM_GROUPS):
        yg = y[:, g * gw:(g + 1) * gw]
        ms = jnp.mean(yg * yg, axis=-1, keepdims=True)
        outs.append(yg * lax.rsqrt(ms + EPS) * norm_ref[:, g * gw:(g + 1) * gw])
    return jnp.concatenate(outs, axis=1)


def _ssd_prompt_kernel(xbc_ref, z_ref, dt_ref, cw_ref, cb_ref, dtb_ref, alog_ref, dexp_ref, norm_ref,
                       y_ref, h_ref, halo_ref, ht_ref, *, rows):
    c = pl.program_id(1)

    @pl.when(c == 0)
    def _():
        halo_ref[...] = jnp.zeros((SUBLANES, CONV_DIM), f32)
        ht_ref[...] = jnp.zeros(ht_ref.shape, f32)

    xt = xbc_ref[...]
    halo = halo_ref[...]
    row8 = lax.broadcasted_iota(jnp.int32, (SUBLANES, CONV_DIM), 0)
    conv = cb_ref[...] + cw_ref[CONV_WIDTH - 1:CONV_WIDTH, :] * xt
    for sft in range(1, CONV_WIDTH):
        rolled = pltpu.roll(xt, sft, axis=0)
        head = jnp.where(row8 < sft, pltpu.roll(halo, sft, axis=0), rolled[0:SUBLANES, :])
        tap = CONV_WIDTH - 1 - sft
        conv = conv + cw_ref[tap:tap + 1, :] * jnp.concatenate([head, rolled[SUBLANES:, :]], axis=0)
    act = _silu(conv)
    halo_ref[...] = xt[rows - SUBLANES:rows, :]

    ri = lax.broadcasted_iota(jnp.int32, (rows, rows), 0)
    ci = lax.broadcasted_iota(jnp.int32, (rows, rows), 1)
    xs, bm, cm, dt_e, cs_e, xdt, y = _ssd_chunk(act, dt_ref[...], ci <= ri, dtb_ref, alog_ref)

    cs_last = cs_e[rows - 1:rows, :]
    ecs = jnp.exp2(cs_e)
    xdte = (xdt * jnp.exp2(cs_last - cs_e)).astype(bf16)
    dec = jnp.exp2(cs_last)
    hw = D_INNER // SSM_GROUPS
    yoff = []
    for g in range(SSM_GROUPS):
        ht_g = ht_ref[:, g * hw:(g + 1) * hw]
        yoff.append(_dot(cm[:, g * D_STATE:(g + 1) * D_STATE].astype(bf16), ht_g.astype(bf16)))
        upd = _dot_tn(bm[:, g * D_STATE:(g + 1) * D_STATE].astype(bf16), xdte[:, g * hw:(g + 1) * hw])
        ht_ref[:, g * hw:(g + 1) * hw] = ht_g * dec[:, g * hw:(g + 1) * hw] + upd
    y = y + jnp.concatenate(yoff, axis=1) * ecs
    y_ref[...] = _ssd_finish(y, xs, z_ref[...], dexp_ref, norm_ref)

    @pl.when(c == pl.num_programs(1) - 1)
    def _():
        h_ref[0] = ht_ref[...].T.reshape(SSM_HEADS, SSM_HEAD_DIM, D_STATE)


def ssd_prompt(proj, conv_w, conv_b_row, dtb_row, alog_row, dexp_row, norm_row, batch, seq, rows):
    nc = seq // rows
    return pl.pallas_call(
        functools.partial(_ssd_prompt_kernel, rows=rows),
        grid=(batch, nc),
        in_specs=[
            pl.BlockSpec((rows, CONV_DIM), lambda b, c: (b * nc + c, OFF_XBC // CONV_DIM)),
            pl.BlockSpec((rows, D_INNER), lambda b, c: (b * nc + c, OFF_Z // D_INNER)),
            pl.BlockSpec((rows, LANES), lambda b, c: (b * nc + c, OFF_DT // LANES)),
            pl.BlockSpec((CONV_WIDTH, CONV_DIM), lambda b, c: (0, 0)),
            pl.BlockSpec((1, CONV_DIM), lambda b, c: (0, 0)),
            pl.BlockSpec((1, LANES), lambda b, c: (0, 0)),
            pl.BlockSpec((1, LANES), lambda b, c: (0, 0)),
            pl.BlockSpec((1, D_INNER), lambda b, c: (0, 0)),
            pl.BlockSpec((1, D_INNER), lambda b, c: (0, 0)),
        ],
        out_specs=[
            pl.BlockSpec((rows, D_INNER), lambda b, c: (b * nc + c, 0)),
            pl.BlockSpec((1, SSM_HEADS, SSM_HEAD_DIM, D_STATE), lambda b, c: (b, 0, 0, 0)),
        ],
        out_shape=[
            jax.ShapeDtypeStruct((batch * seq, D_INNER), f32),
            jax.ShapeDtypeStruct((batch, SSM_HEADS, SSM_HEAD_DIM, D_STATE), f32),
        ],
        scratch_shapes=[
            pltpu.VMEM((SUBLANES, CONV_DIM), f32),
            pltpu.VMEM((D_STATE, D_INNER), f32),
        ],
        compiler_params=_cparams(("arbitrary", "arbitrary")),
        name="ssd_prompt",
    )(proj, proj, proj, conv_w, conv_b_row, dtb_row, alog_row, dexp_row, norm_row)


def _mem_q(q_ref, g_ref, h, rows=slice(None)):
    q = q_ref[rows, h * MEM_HEAD_DIM:(h + 1) * MEM_HEAD_DIM]
    ms = jnp.mean(q * q, axis=-1, keepdims=True)
    return (q * lax.rsqrt(ms + EPS) * g_ref[...] * (MEM_HEAD_DIM ** -0.5)).astype(bf16)


def _mem_attend_kernel(q_ref, mk_ref, mv_ref, g_ref, o_ref, *, head_axis):
    if head_axis:
        nseq_step = mk_ref.shape[0]
        tq = q_ref.shape[0] // nseq_step
        nrow = MEM_TOKENS * MEM_HEADS
        rh = lax.broadcasted_iota(jnp.int32, (MEM_HEADS * tq, nrow), 0) // tq
        ch = lax.broadcasted_iota(jnp.int32, (MEM_HEADS * tq, nrow), 1) % MEM_HEADS
        own_head = rh == ch
        for i in range(nseq_step):
            rows = slice(i * tq, (i + 1) * tq)
            q4 = jnp.concatenate([_mem_q(q_ref, g_ref, h, rows) for h in range(MEM_HEADS)], axis=0)
            s = _dot_nt(q4, mk_ref[i].reshape(nrow, MEM_HEAD_DIM).astype(bf16))
            s = jnp.where(own_head, s, NEG)
            m = jnp.max(s, axis=-1, keepdims=True)
            p = jnp.exp(s - m)
            l = jnp.sum(p, axis=-1, keepdims=True)
            o = _dot(p.astype(bf16), mv_ref[i].reshape(nrow, MEM_HEAD_DIM).astype(bf16)) / l
            o_ref[rows, :] = jnp.concatenate([o[h * tq:(h + 1) * tq, :] for h in range(MEM_HEADS)], axis=1)
        return
    outs = []
    for h in range(MEM_HEADS):
        lo = h * MEM_HEAD_DIM
        s = _dot_nt(_mem_q(q_ref, g_ref, h), mk_ref[0, :, lo:lo + MEM_HEAD_DIM].astype(bf16))
        m = jnp.max(s, axis=-1, keepdims=True)
        p = jnp.exp(s - m)
        l = jnp.sum(p, axis=-1, keepdims=True)
        outs.append(_dot(p.astype(bf16), mv_ref[0, :, lo:lo + MEM_HEAD_DIM].astype(bf16)) / l)
    o_ref[...] = jnp.concatenate(outs, axis=1)


def mem_attend(proj, mk, mv, mq_gain_row, nseq, rows_per_seq, tq):
    w = MEM_HEADS * MEM_HEAD_DIM
    head_axis = mk.ndim == 4
    if head_axis:
        assert tq == rows_per_seq and nseq % MEM_SEQS_PER_STEP == 0
        steps, tq, nq = nseq // MEM_SEQS_PER_STEP, MEM_SEQS_PER_STEP * rows_per_seq, 1
        kv_spec = pl.BlockSpec((MEM_SEQS_PER_STEP, MEM_TOKENS, MEM_HEADS, MEM_HEAD_DIM), lambda b, i: (b, 0, 0, 0))
    else:
        steps, nq = nseq, rows_per_seq // tq
        kv_spec = pl.BlockSpec((1, MEM_TOKENS, w), lambda b, i: (b, 0, 0))
    return pl.pallas_call(
        functools.partial(_mem_attend_kernel, head_axis=head_axis),
        grid=(steps, nq),
        in_specs=[
            pl.BlockSpec((tq, w), lambda b, i: (b * nq + i, OFF_MQ // w)),
            kv_spec,
            kv_spec,
            pl.BlockSpec((1, MEM_HEAD_DIM), lambda b, i: (0, 0)),
        ],
        out_specs=pl.BlockSpec((tq, w), lambda b, i: (b * nq + i, 0)),
        out_shape=jax.ShapeDtypeStruct((nseq * rows_per_seq, w), f32),
        compiler_params=_cparams(("arbitrary", "arbitrary")),
        name="mem_attend",
    )(proj, mk, mv, mq_gain_row)


def _merge_kernel(x_ref, a_ref, s_ref, m_ref, ga_ref, gs_ref, gm_ref, wa_ref, ws_ref, wm_ref, wo_ref, o_ref):
    mixed = _sigmoid(ga_ref[...]) * _dot(a_ref[...].astype(bf16), wa_ref[...])
    mixed = mixed + _sigmoid(gs_ref[...]) * _dot(s_ref[...].astype(bf16), ws_ref[...])
    mixed = mixed + _sigmoid(gm_ref[...]) * _dot(m_ref[...].astype(bf16), wm_ref[...])
    o_ref[...] = x_ref[...] + _dot(mixed.astype(bf16), wo_ref[...])


def merge(x, attn_o, ssm_o, mem_o, proj, wa, ws, wm, wo, tm):
    t, d = x.shape
    row = lambda i: (i, 0)
    const = lambda i: (0, 0)
    gate_spec = lambda k: pl.BlockSpec((tm, d), lambda i: (i, OFF_GATES // d + k))
    w_spec = pl.BlockSpec((d, d), const)
    return pl.pallas_call(
        _merge_kernel,
        grid=(t // tm,),
        in_specs=[pl.BlockSpec((tm, d), row)] * 4 + [gate_spec(0), gate_spec(1), gate_spec(2)] + [w_spec] * 4,
        out_specs=pl.BlockSpec((tm, d), row),
        out_shape=jax.ShapeDtypeStruct((t, d), f32),
        compiler_params=_cparams(("arbitrary",)),
        name="merge",
    )(x, attn_o, ssm_o, mem_o, proj, proj, proj, wa, ws, wm, wo)


def _ffn_kernel(x_ref, g_ref, wg_ref, wu_ref, wd_ref, o_ref):
    x = x_ref[...]
    ms = jnp.mean(x * x, axis=-1, keepdims=True)
    h = (x * lax.rsqrt(ms + EPS) * g_ref[...]).astype(bf16)
    act = _silu(_dot(h, wg_ref[...])) * _dot(h, wu_ref[...])
    o_ref[...] = x + _dot(act.astype(bf16), wd_ref[...])


def ffn(x, gain_row, wg, wu, wd, tm):
    t, d = x.shape
    dff = wg.shape[1]
    const = lambda i: (0, 0)
    single = pl.Buffered(1)
    return pl.pallas_call(
        _ffn_kernel,
        grid=(t // tm,),
        in_specs=[
            pl.BlockSpec((tm, d), lambda i: (i, 0)),
            pl.BlockSpec((1, d), const),
            pl.BlockSpec((d, dff), const, pipeline_mode=single),
            pl.BlockSpec((d, dff), const, pipeline_mode=single),
            pl.BlockSpec((dff, d), const, pipeline_mode=single),
        ],
        out_specs=pl.BlockSpec((tm, d), lambda i: (i, 0)),
        out_shape=jax.ShapeDtypeStruct((t, d), f32),
        compiler_params=_cparams(("arbitrary",)),
        name="ffn",
    )(x, gain_row, wg, wu, wd)


def _moba_sample_kernel(pt_ref, q_ref, kn_ref, vn_ref, qg_ref, sl_ref, ck_hbm, cv_hbm, o_ref,
                        kbuf, vbuf, s_ref, sem, *, n_pages):
    b = pl.program_id(0)
    nseq = pl.num_programs(0)
    kvw = N_KV_HEADS * HEAD_DIM
    half = SAMPLE_ROWS // 2
    nrow = Q_PER_KV * N_KV_HEADS * half
    nblk = n_pages * PAGE_SIZE // MOBA_BLOCK
    ppb = MOBA_BLOCK // PAGE_SIZE
    past = n_pages * PAGE_SIZE
    slot = b % 2

    def k_copy(seq, sl, p):
        return pltpu.make_async_copy(ck_hbm.at[pt_ref[seq, p]], kbuf.at[sl, p], sem.at[0, sl])

    def v_copy(seq, sl, p):
        return pltpu.make_async_copy(cv_hbm.at[pt_ref[seq, p]], vbuf.at[sl, p], sem.at[1, sl])

    def start_all(seq, sl):
        for p in range(n_pages):
            k_copy(seq, sl, p).start()
        for p in range(n_pages):
            v_copy(seq, sl, p).start()

    @pl.when(b == 0)
    def _():
        start_all(0, 0)

    nxt = jnp.minimum(b + 1, nseq - 1)

    qb = q_ref[...]
    lane_g = lax.broadcasted_iota(jnp.int32, (SAMPLE_ROWS, kvw), 1) // HEAD_DIM
    low_half = lax.broadcasted_iota(jnp.int32, (SAMPLE_ROWS, kvw), 0) < half
    pieces = []
    for r in range(Q_PER_KV):
        slab = qb[:, r * kvw:(r + 1) * kvw]
        for g in range(N_KV_HEADS):
            pieces.append(jnp.where(lane_g == g, slab, 0.0))
    tiles = [jnp.where(low_half, pieces[2 * k], pltpu.roll(pieces[2 * k + 1], half, axis=0))
             for k in range(len(pieces) // 2)]
    qbd = jnp.concatenate(tiles, axis=0)
    ms = jnp.sum(qbd * qbd, axis=-1, keepdims=True) * (1.0 / HEAD_DIM)
    qn = qbd * lax.rsqrt(ms + EPS) * qg_ref[...]
    qs = (qn * (HEAD_DIM ** -0.5 * LOG2E)).astype(bf16)

    for p in range(n_pages):
        k_copy(b, slot, p).wait()

    def k_page(p):
        return kbuf[slot, p].reshape(kvw, PAGE_SIZE)

    lane = lax.broadcasted_iota(jnp.int32, (kvw, LANES), 1)
    kmean_t = jnp.zeros((kvw, LANES), f32)
    for j in range(nblk):
        blk = k_page(j * ppb)
        for pp in range(1, ppb):
            blk = blk + k_page(j * ppb + pp)
        kmean_t = kmean_t + jnp.where(lane == j, jnp.sum(blk, axis=-1, keepdims=True) * (1.0 / MOBA_BLOCK), 0.0)
    kh, kmid, kl = _split3(kmean_t)
    qh, qmid, ql = _split3(qn)
    gate = _dot(qh, kh) + _dot(qh, kmid) + _dot(qmid, kh) + _dot(qh, kl) + _dot(ql, kh) + _dot(qmid, kmid)
    jb = lax.broadcasted_iota(jnp.int32, (nrow, LANES), 1)
    gate = jnp.where(jb < nblk, gate, -jnp.inf)
    pen = _top3_penalty(gate, jb, 1)

    slope = sl_ref[...]
    kk = lax.broadcasted_iota(jnp.int32, (nrow, PAGE_SIZE), 1)
    pages_per_dot = 4
    for p0 in range(0, n_pages, pages_per_dot):
        for p in range(p0, p0 + pages_per_dot):
            k_copy(nxt, 1 - slot, p).start()
        k4 = jnp.concatenate([k_page(p).astype(bf16) for p in range(p0, p0 + pages_per_dot)], axis=1)
        s4 = _dot(qs, k4)
        for i in range(pages_per_dot):
            p = p0 + i
            j = p // ppb
            rel = (kk + (p * PAGE_SIZE - past)).astype(f32)
            s_ref[:, p * PAGE_SIZE:(p + 1) * PAGE_SIZE] = (
                s4[:, i * PAGE_SIZE:(i + 1) * PAGE_SIZE] + slope * rel + pen[:, j:j + 1])

    zpad = jnp.zeros((LANES - SAMPLE_ROWS, kvw), f32)
    knew = jnp.concatenate([kn_ref[...], zpad], axis=0).astype(bf16)
    vnew = jnp.concatenate([vn_ref[...], zpad], axis=0).astype(bf16)
    kn_lane = lax.broadcasted_iota(jnp.int32, (nrow, LANES), 1)
    tok = lax.broadcasted_iota(jnp.int32, (nrow, LANES), 0) % half
    s_own = _dot_nt(qs, knew) + slope * kn_lane.astype(f32)
    s_ref[:, past:past + LANES] = jnp.where(kn_lane <= tok, s_own, NEG)

    m = jnp.max(s_ref[...], axis=-1, keepdims=True)
    p_own = jnp.exp2(s_ref[:, past:past + LANES] - m)
    n_acc = 4
    psums = [p_own] + [None] * (n_acc - 1)
    accs = [_dot(p_own.astype(bf16), vnew)] + [None] * (n_acc - 1)

    for p in range(n_pages):
        v_copy(b, slot, p).wait()

    for p0 in range(0, n_pages, ppb):
        for p in range(p0, p0 + ppb):
            v_copy(nxt, 1 - slot, p).start()
        pp = jnp.exp2(s_ref[:, p0 * PAGE_SIZE:(p0 + ppb) * PAGE_SIZE] - m)
        vb = jnp.concatenate([vbuf[slot, p].reshape(kvw, PAGE_SIZE).astype(bf16) for p in range(p0, p0 + ppb)], axis=1)
        pv = _dot_nt(pp.astype(bf16), vb)
        a = (p0 // ppb) % n_acc
        pl_sum = pp[:, :PAGE_SIZE]
        for i in range(1, ppb):
            pl_sum = pl_sum + pp[:, i * PAGE_SIZE:(i + 1) * PAGE_SIZE]
        psums[a] = pl_sum if psums[a] is None else psums[a] + pl_sum
        accs[a] = pv if accs[a] is None else accs[a] + pv
    acc = (accs[0] + accs[1]) + (accs[2] + accs[3])
    psum = (psums[0] + psums[1]) + (psums[2] + psums[3])
    o = acc / jnp.sum(psum, axis=-1, keepdims=True)

    @pl.when(b == nseq - 1)
    def _():
        for p in range(n_pages):
            k_copy(nxt, 1 - slot, p).wait()
        for p in range(n_pages):
            v_copy(nxt, 1 - slot, p).wait()

    slabs = []
    for r in range(Q_PER_KV):
        slab = jnp.zeros((SAMPLE_ROWS, kvw), f32)
        for g in range(N_KV_HEADS):
            piece = r * N_KV_HEADS + g
            tile = o[(piece // 2) * SAMPLE_ROWS:(piece // 2 + 1) * SAMPLE_ROWS, :]
            if piece % 2:
                tile = pltpu.roll(tile, half, axis=0)
            slab = slab + jnp.where(lane_g == g, tile, 0.0)
        slabs.append(slab)
    o_ref[...] = jnp.concatenate(slabs, axis=1)


def moba_sample(page_table, proj, kn_new, ck, cv, q_gain_row, slope_rows, nseq):
    kvw = N_KV_HEADS * HEAD_DIM
    qw = N_HEADS * HEAD_DIM
    n_pages = page_table.shape[1]
    nrow = Q_PER_KV * N_KV_HEADS * (SAMPLE_ROWS // 2)
    past = n_pages * PAGE_SIZE
    grid_spec = pltpu.PrefetchScalarGridSpec(
        num_scalar_prefetch=1,
        grid=(nseq,),
        in_specs=[
            pl.BlockSpec((SAMPLE_ROWS, qw), lambda b, pt: (b, OFF_Q // qw)),
            pl.BlockSpec((SAMPLE_ROWS, kvw), lambda b, pt: (b, 0)),
            pl.BlockSpec((SAMPLE_ROWS, kvw), lambda b, pt: (b, OFF_V // kvw)),
            pl.BlockSpec((1, kvw), lambda b, pt: (0, 0)),
            pl.BlockSpec((nrow, LANES), lambda b, pt: (0, 0)),
            pl.BlockSpec(memory_space=pl.ANY),
            pl.BlockSpec(memory_space=pl.ANY),
        ],
        out_specs=pl.BlockSpec((SAMPLE_ROWS, qw), lambda b, pt: (b, 0)),
        scratch_shapes=[
            pltpu.VMEM((2, n_pages, N_KV_HEADS, HEAD_DIM, PAGE_SIZE), f32),
            pltpu.VMEM((2, n_pages, N_KV_HEADS, HEAD_DIM, PAGE_SIZE), f32),
            pltpu.VMEM((nrow, past + LANES), f32),
            pltpu.SemaphoreType.DMA((2, 2)),
        ],
    )
    return pl.pallas_call(
        functools.partial(_moba_sample_kernel, n_pages=n_pages),
        grid_spec=grid_spec,
        out_shape=jax.ShapeDtypeStruct((nseq * SAMPLE_ROWS, qw), f32),
        compiler_params=_cparams(("arbitrary",)),
        name="moba_sample",
    )(page_table, proj, kn_new, proj, q_gain_row, slope_rows, ck, cv)


def _ssd_sample_kernel(ext_ref, z_ref, dt_ref, h0_ref, cw_ref, cb_ref, dtb_ref, alog_ref, dexp_ref, norm_ref,
                       y_ref, h_ref, scr_ref, *, nseq_step, n_new):
    rows = nseq_step * SAMPLE_ROWS
    scr_ref[pl.ds(0, rows), :] = ext_ref[...]
    scr_ref[pl.ds(rows, SUBLANES), :] = jnp.zeros((SUBLANES, CONV_DIM), f32)
    act = _ssd_conv_act(scr_ref, rows, cw_ref, cb_ref, base=0)

    ri = lax.broadcasted_iota(jnp.int32, (rows, rows), 0)
    ci = lax.broadcasted_iota(jnp.int32, (rows, rows), 1)
    allowed = jnp.logical_and(ri // SAMPLE_ROWS == ci // SAMPLE_ROWS,
                              jnp.logical_and(ci % SAMPLE_ROWS <= ri % SAMPLE_ROWS, ci % SAMPLE_ROWS < n_new))
    xs, bm, cm, dt_e, cs_e, xdt, y = _ssd_chunk(act, dt_ref[...], allowed, dtb_ref, alog_ref)

    valid = lax.broadcasted_iota(jnp.int32, (SAMPLE_ROWS, 1), 0) < n_new
    ecs = jnp.exp2(cs_e)
    hw = D_INNER // SSM_GROUPS
    hpg = SSM_HEADS // SSM_GROUPS
    yoff_rows = []
    for i in range(nseq_step):
        lo = i * SAMPLE_ROWS
        cs_i = cs_e[lo:lo + SAMPLE_ROWS, :]
        cs_last = cs_e[lo + n_new - 1:lo + n_new, :]
        xdte = jnp.where(valid, xdt[lo:lo + SAMPLE_ROWS, :] * jnp.exp2(cs_last - cs_i), 0.0)
        yg = []
        upd = []
        for g in range(SSM_GROUPS):
            h0g = h0_ref[i, g * hpg:(g + 1) * hpg].reshape(hw, D_STATE)
            yg.append(_dot_nt(cm[lo:lo + SAMPLE_ROWS, g * D_STATE:(g + 1) * D_STATE], h0g))
            upd.append(_dot_tn(xdte[:, g * hw:(g + 1) * hw], bm[lo:lo + SAMPLE_ROWS, g * D_STATE:(g + 1) * D_STATE]))
        yoff_rows.append(jnp.concatenate(yg, axis=1))
        dec_row = jnp.exp2(cs_last)
        for h in range(SSM_HEADS):
            dec_h = jnp.broadcast_to(dec_row[:, h * SSM_HEAD_DIM:h * SSM_HEAD_DIM + 1], (SSM_HEAD_DIM, D_STATE))
            r = h % hpg
            h_ref[i, h] = h0_ref[i, h] * dec_h + upd[h // hpg][r * SSM_HEAD_DIM:(r + 1) * SSM_HEAD_DIM, :]
    y = y + jnp.concatenate(yoff_rows, axis=0) * ecs
    y_ref[...] = _ssd_finish(y, xs, z_ref[...], dexp_ref, norm_ref)


def ssd_sample(ext, proj, h0, conv_w, conv_b_row, dtb_row, alog_row, dexp_row, norm_row, nseq, nseq_step, n_new):
    rows = nseq_step * SAMPLE_ROWS
    const = lambda i: (0, 0)
    state_spec = pl.BlockSpec((nseq_step, SSM_HEADS, SSM_HEAD_DIM, D_STATE), lambda i: (i, 0, 0, 0))
    return pl.pallas_call(
        functools.partial(_ssd_sample_kernel, nseq_step=nseq_step, n_new=n_new),
        grid=(nseq // nseq_step,),
        in_specs=[
            pl.BlockSpec((rows, CONV_DIM), lambda i: (i, 0)),
            pl.BlockSpec((rows, D_INNER), lambda i: (i, OFF_Z // D_INNER)),
            pl.BlockSpec((rows, LANES), lambda i: (i, OFF_DT // LANES)),
            state_spec,
            pl.BlockSpec((CONV_WIDTH, CONV_DIM), const),
            pl.BlockSpec((1, CONV_DIM), const),
            pl.BlockSpec((1, LANES), const),
            pl.BlockSpec((1, LANES), const),
            pl.BlockSpec((1, D_INNER), const),
            pl.BlockSpec((1, D_INNER), const),
        ],
        out_specs=[pl.BlockSpec((rows, D_INNER), lambda i: (i, 0)), state_spec],
        out_shape=[
            jax.ShapeDtypeStruct((nseq * SAMPLE_ROWS, D_INNER), f32),
            jax.ShapeDtypeStruct(h0.shape, f32),
        ],
        scratch_shapes=[pltpu.VMEM((rows + SUBLANES, CONV_DIM), f32)],
        compiler_params=_cparams(("arbitrary",)),
        name="ssd_sample",
    )(ext, proj, proj, h0, conv_w, conv_b_row, dtb_row, alog_row, dexp_row, norm_row)


IN_SIZES = (N_HEADS * HEAD_DIM, N_KV_HEADS * HEAD_DIM, N_KV_HEADS * HEAD_DIM, D_INNER, CONV_DIM, SSM_HEADS,
            MEM_HEADS * MEM_HEAD_DIM, 3 * D_MODEL)
TM = 512
TM_PROJ = 1024
TN_PROJ = 2944
TQ = 256
SSD_ROWS = 256
SSD_SAMPLE_SEQS = 8
MEM_SEQS_PER_STEP = 4


def _pad_lanes(v):
    return jnp.pad(v, (0, LANES - v.shape[0])).reshape(1, LANES)


def kernel(x_prompt, x_sample, mem_prompt, cache_k, cache_v, page_table, state_conv, state_ssm, cache_mem_k, cache_mem_v, norm_mix, w_in, q_norm, k_norm, conv_w, conv_b, dt_bias, a_log, d_skip, ssm_norm, mem_norm, w_mem_kv, mq_norm, mk_norm, w_attn_br, w_ssm_br, w_mem_br, w_out, norm_ffn, w_gate, w_up, w_down):
    assert w_in.shape[0] == 1, "single layer"
    assert x_sample.shape[1] <= SAMPLE_ROWS // 2, "sample MoBA packs two heads' tokens into one row tile"
    batch, seq, d = x_prompt.shape
    nseq, n_new, _ = x_sample.shape
    kvw = N_KV_HEADS * HEAD_DIM
    qw = N_HEADS * HEAD_DIM

    offs = np.cumsum(IN_SIZES)[:-1].tolist()
    wq, wk, wv, wz, wxbc, wdt, wmq, wgates = jnp.split(w_in[0], offs, axis=1)
    wdt = jnp.pad(wdt, ((0, 0), (0, LANES - SSM_HEADS)))
    tail = [wz, wxbc, wmq, wgates, wk, wv, wdt]
    w_prompt = jnp.concatenate([wq] + tail, axis=1).astype(bf16)
    wq_rgd = wq.reshape(d, N_KV_HEADS, Q_PER_KV, HEAD_DIM).transpose(0, 2, 1, 3).reshape(d, qw)
    w_sample = jnp.concatenate([wq_rgd] + tail, axis=1).astype(bf16)
    wa = w_attn_br[0].astype(bf16)
    wa_rgd = w_attn_br[0].reshape(N_KV_HEADS, Q_PER_KV, HEAD_DIM, d).transpose(1, 0, 2, 3).reshape(qw, d).astype(bf16)
    ws, wm, wo = w_ssm_br[0].astype(bf16), w_mem_br[0].astype(bf16), w_out[0].astype(bf16)
    wg, wu, wd = w_gate[0].astype(bf16), w_up[0].astype(bf16), w_down[0].astype(bf16)

    k_gain = jnp.tile(k_norm[0], N_KV_HEADS).reshape(1, kvw)
    conv_b_row = conv_b[0].reshape(1, CONV_DIM)
    dtb_row, alog_row = _pad_lanes(dt_bias[0]), _pad_lanes(a_log[0])
    dexp_row = jnp.repeat(d_skip[0], SSM_HEAD_DIM).reshape(1, D_INNER)
    ssm_norm_row = ssm_norm[0].reshape(1, D_INNER)
    mq_gain = mq_norm[0].reshape(1, MEM_HEAD_DIM)
    ffn_gain = norm_ffn[0].reshape(1, d)
    slopes2 = jnp.exp2(-8.0 * jnp.arange(1, N_HEADS + 1, dtype=f32) / N_HEADS) * LOG2E

    xp = x_prompt.reshape(batch * seq, d)
    proj = inproj(xp, norm_mix[0], w_prompt, TM_PROJ, TN_PROJ)
    knt, vt32, kb, vt, kmean = kprep(proj, k_gain, batch, seq, 2048)
    kmean_g = kmean.reshape(batch, seq // MOBA_BLOCK, N_KV_HEADS, HEAD_DIM).transpose(0, 2, 1, 3)
    slope_cols = jnp.repeat(slopes2.reshape(N_KV_HEADS, Q_PER_KV), TQ, axis=1).reshape(N_KV_HEADS, 1, Q_PER_KV * TQ)
    q_gain_col = jnp.broadcast_to(q_norm[0][:, None], (HEAD_DIM, TQ))
    attn_o = moba_prompt(proj, kb, vt, kmean_g, q_gain_col, slope_cols, batch, seq, TQ)
    ssm_o, h_prompt = ssd_prompt(proj, conv_w[0], conv_b_row, dtb_row, alog_row, dexp_row, ssm_norm_row,
                                 batch, seq, SSD_ROWS)
    mem_kv = inproj(mem_prompt.reshape(batch * MEM_TOKENS, d), mem_norm[0], w_mem_kv[0].astype(bf16), TM, qw)
    mk = headnorm(mem_kv, 0, qw, jnp.tile(mk_norm[0], MEM_HEADS).reshape(1, qw), MEM_HEAD_DIM, TM)
    mv = mem_kv[:, qw:]
    mem_o = mem_attend(proj, mk.reshape(batch, MEM_TOKENS, qw), mv.reshape(batch, MEM_TOKENS, qw), mq_gain,
                       batch, seq, TM)
    x1 = merge(xp, attn_o, ssm_o, mem_o, proj, wa, ws, wm, wo, TM)
    y_prompt = ffn(x1, ffn_gain, wg, wu, wd, TM).reshape(batch, seq, d)

    k_prompt = jnp.transpose(knt.reshape(batch, N_KV_HEADS, HEAD_DIM, seq), (0, 3, 1, 2))[None]
    v_prompt = jnp.transpose(vt32.reshape(batch, N_KV_HEADS, HEAD_DIM, seq), (0, 3, 1, 2))[None]
    conv_prompt = proj.reshape(batch, seq, N_PROJ)[:, seq - (CONV_WIDTH - 1):, OFF_XBC:OFF_XBC + CONV_DIM][None]
    mem_k_prompt = mk.reshape(1, batch, MEM_TOKENS, MEM_HEADS, MEM_HEAD_DIM)
    mem_v_prompt = mv.reshape(1, batch, MEM_TOKENS, MEM_HEADS, MEM_HEAD_DIM)

    xs = jnp.pad(x_sample, ((0, 0), (0, SAMPLE_ROWS - n_new), (0, 0))).reshape(nseq * SAMPLE_ROWS, d)
    proj_s = inproj(xs, norm_mix[0], w_sample, TM_PROJ, TN_PROJ)
    kn_s = headnorm(proj_s, OFF_K // kvw, kvw, k_gain, HEAD_DIM, TM)
    slope_rows = jnp.broadcast_to(
        jnp.repeat(slopes2.reshape(N_KV_HEADS, Q_PER_KV).T.reshape(-1), SAMPLE_ROWS // 2)[:, None],
        (N_HEADS * (SAMPLE_ROWS // 2), LANES))
    ck_t = jnp.transpose(cache_k[0], (0, 2, 3, 1))
    cv_t = jnp.transpose(cache_v[0], (0, 2, 3, 1))
    attn_s = moba_sample(page_table, proj_s, kn_s, ck_t, cv_t, jnp.tile(q_norm[0], N_KV_HEADS).reshape(1, kvw),
                         slope_rows, nseq)
    xbc_new = proj_s[:, OFF_XBC:OFF_XBC + CONV_DIM].reshape(nseq, SAMPLE_ROWS, CONV_DIM)[:, :n_new]
    ext = jnp.concatenate(
        [state_conv[0], xbc_new, jnp.zeros((nseq, SAMPLE_ROWS - (CONV_WIDTH - 1) - n_new, CONV_DIM), f32)], axis=1)
    ssm_s, h_sample = ssd_sample(ext.reshape(nseq * SAMPLE_ROWS, CONV_DIM), proj_s, state_ssm[0], conv_w[0], conv_b_row,
                                 dtb_row, alog_row, dexp_row, ssm_norm_row, nseq, SSD_SAMPLE_SEQS, n_new)
    mem_s = mem_attend(proj_s, cache_mem_k[0], cache_mem_v[0], mq_gain, nseq, SAMPLE_ROWS, SAMPLE_ROWS)
    x1s = merge(xs, attn_s, ssm_s, mem_s, proj_s, wa_rgd, ws, wm, wo, TM)
    y_sample = ffn(x1s, ffn_gain, wg, wu, wd, TM).reshape(nseq, SAMPLE_ROWS, d)[:, :n_new]

    k_sample = kn_s.reshape(nseq, SAMPLE_ROWS, N_KV_HEADS, HEAD_DIM)[:, :n_new][None]
    v_sample = proj_s[:, OFF_V:OFF_V + kvw].reshape(nseq, SAMPLE_ROWS, N_KV_HEADS, HEAD_DIM)[:, :n_new][None]
    conv_sample = ext[:, n_new:n_new + CONV_WIDTH - 1][None]

    return (y_prompt, y_sample, k_prompt, v_prompt, conv_prompt, h_prompt[None], mem_k_prompt, mem_v_prompt,
            k_sample, v_sample, conv_sample, h_sample[None])
```

```python
import functools
import math

import jax
import jax.numpy as jnp
import numpy as np
from jax import lax
from jax.experimental import pallas as pl
from jax.experimental.pallas import tpu as pltpu

f32 = jnp.float32
bf16 = jnp.bfloat16

D_MODEL = 1024
N_HEADS = 16
N_KV_HEADS = 4
HEAD_DIM = 64
Q_PER_KV = N_HEADS // N_KV_HEADS
MOBA_BLOCK = 256
MOBA_TOPK = 3
SSM_HEADS = 16
SSM_HEAD_DIM = 64
D_INNER = SSM_HEADS * SSM_HEAD_DIM
SSM_GROUPS = 4
D_STATE = 128
CONV_WIDTH = 4
CONV_DIM = D_INNER + 2 * SSM_GROUPS * D_STATE
MEM_TOKENS = 256
MEM_HEADS = 4
MEM_HEAD_DIM = 256
D_FF = 2816
EPS = 1e-6
PAGE_SIZE = 128

LANES = 128
SUBLANES = 8
LOG2E = 1.4426950408889634
NEG = -1e30
VMEM_LIMIT = 56 * 1024 * 1024

OFF_Q, OFF_Z, OFF_XBC, OFF_MQ, OFF_K, OFF_V, OFF_DT = 0, 1024, 2048, 4096, 5120, 5376, 5632
N_PROJ = 5760
SAMPLE_ROWS = 8


def _cparams(sem):
    return pltpu.CompilerParams(dimension_semantics=sem, vmem_limit_bytes=VMEM_LIMIT)


def _split3(x):
    h = x.astype(bf16)
    r = x - h.astype(f32)
    m = r.astype(bf16)
    l = (r - m.astype(f32)).astype(bf16)
    return h, m, l


def _dot(a, b):
    return jnp.dot(a, b, preferred_element_type=f32)


def _dot_nt(a, b):
    return lax.dot_general(a, b, (((1,), (1,)), ((), ())), preferred_element_type=f32)


def _dot_tn(a, b):
    return lax.dot_general(a, b, (((0,), (0,)), ((), ())), preferred_element_type=f32)


def _dot_exact_rhs(x, m_bf16):
    h, m, l = _split3(x)
    return _dot(h, m_bf16) + _dot(m, m_bf16) + _dot(l, m_bf16)


def _dot_exact_lhs(m_bf16, x):
    h, m, l = _split3(x)
    return _dot(m_bf16, h) + _dot(m_bf16, m) + _dot(m_bf16, l)


def _sigmoid(x):
    return 0.5 * jnp.tanh(0.5 * x) + 0.5


def _silu(x):
    return x * _sigmoid(x)


def _softplus(x):
    return jnp.maximum(x, 0.0) + jnp.log(1.0 + jnp.exp(-jnp.abs(x)))


def _inproj_kernel(x_ref, g_ref, w_ref, o_ref):
    x = x_ref[...]
    ms = jnp.mean(x * x, axis=-1, keepdims=True)
    xn = (x * lax.rsqrt(ms + EPS) * g_ref[...]).astype(bf16)
    o_ref[...] = _dot(xn, w_ref[...]).astype(o_ref.dtype)


def inproj(x, gain, w, tm, tn, out_dtype=f32):
    t, d = x.shape
    n = w.shape[1]
    return pl.pallas_call(
        _inproj_kernel,
        grid=(n // tn, t // tm),
        in_specs=[
            pl.BlockSpec((tm, d), lambda j, i: (i, 0)),
            pl.BlockSpec((1, d), lambda j, i: (0, 0)),
            pl.BlockSpec((d, tn), lambda j, i: (0, j)),
        ],
        out_specs=pl.BlockSpec((tm, tn), lambda j, i: (i, j)),
        out_shape=jax.ShapeDtypeStruct((t, n), out_dtype),
        compiler_params=_cparams(("arbitrary", "arbitrary")),
        name="inproj",
    )(x, gain.reshape(1, d), w)


def _seg_ones(width, seg):
    r = lax.broadcasted_iota(jnp.int32, (width, width), 0) // seg
    c = lax.broadcasted_iota(jnp.int32, (width, width), 1) // seg
    return jnp.where(r == c, 1.0, 0.0).astype(bf16)


def _seg_rmsnorm(x, gain_row, seg):
    ss = _dot_exact_rhs(x * x, _seg_ones(x.shape[1], seg))
    return x * lax.rsqrt(ss * (1.0 / seg) + EPS) * gain_row


def _headnorm_kernel(x_ref, g_ref, o_ref, *, seg):
    o_ref[...] = _seg_rmsnorm(x_ref[...], g_ref[...], seg)


def headnorm(x, col_block, width, gain_row, seg, tm):
    t = x.shape[0]
    return pl.pallas_call(
        functools.partial(_headnorm_kernel, seg=seg),
        grid=(t // tm,),
        in_specs=[
            pl.BlockSpec((tm, width), lambda i: (i, col_block)),
            pl.BlockSpec((1, width), lambda i: (0, 0)),
        ],
        out_specs=pl.BlockSpec((tm, width), lambda i: (i, 0)),
        out_shape=jax.ShapeDtypeStruct((t, width), f32),
        compiler_params=_cparams(("arbitrary",)),
        name="headnorm",
    )(x, gain_row)


def _kprep_kernel(k_ref, v_ref, g_ref, knt_ref, vt32_ref, ka_ref, vt_ref, km_ref):
    kn = _seg_rmsnorm(k_ref[...], g_ref[...], HEAD_DIM)
    knt_ref[0] = kn.T
    kn_b = kn.astype(bf16)
    rows = kn.shape[0]
    kvw = kn.shape[1]
    r = lax.broadcasted_iota(jnp.int32, (kvw, LANES), 0)
    c = lax.broadcasted_iota(jnp.int32, (kvw, LANES), 1)
    lane = lax.broadcasted_iota(jnp.int32, (rows, LANES), 1)
    pos = (lax.broadcasted_iota(jnp.int32, (rows, LANES), 0) % MOBA_BLOCK).astype(f32)
    pos_lanes = jnp.where(jnp.logical_and(lane >= HEAD_DIM, lane < HEAD_DIM + 3), pos, 0.0)
    for g in range(N_KV_HEADS):
        pick = jnp.where(jnp.logical_and(r == c + g * HEAD_DIM, c < HEAD_DIM), 1.0, 0.0).astype(bf16)
        ka_ref[0, g] = (_dot(kn_b, pick) + pos_lanes).astype(bf16)
    nb = kn.shape[0] // MOBA_BLOCK
    km_ref[0] = jnp.sum(kn.reshape(nb, MOBA_BLOCK, kn.shape[1]), axis=1) * (1.0 / MOBA_BLOCK)
    vt = v_ref[...].T
    vt32_ref[0] = vt
    vt_ref[0] = vt.astype(bf16)


def kprep(proj, k_gain_row, batch, seq, tm):
    kvw = N_KV_HEADS * HEAD_DIM
    nt = seq // tm
    nb = tm // MOBA_BLOCK
    return pl.pallas_call(
        _kprep_kernel,
        grid=(batch, nt),
        in_specs=[
            pl.BlockSpec((tm, kvw), lambda b, i: (b * nt + i, OFF_K // kvw)),
            pl.BlockSpec((tm, kvw), lambda b, i: (b * nt + i, OFF_V // kvw)),
            pl.BlockSpec((1, kvw), lambda b, i: (0, 0)),
        ],
        out_specs=[
            pl.BlockSpec((1, kvw, tm), lambda b, i: (b, 0, i)),
            pl.BlockSpec((1, kvw, tm), lambda b, i: (b, 0, i)),
            pl.BlockSpec((1, N_KV_HEADS, tm, LANES), lambda b, i: (b, 0, i, 0)),
            pl.BlockSpec((1, kvw, tm), lambda b, i: (b, 0, i)),
            pl.BlockSpec((1, nb, kvw), lambda b, i: (b, i, 0)),
        ],
        out_shape=[
            jax.ShapeDtypeStruct((batch, kvw, seq), f32),
            jax.ShapeDtypeStruct((batch, kvw, seq), f32),
            jax.ShapeDtypeStruct((batch, N_KV_HEADS, seq, LANES), bf16),
            jax.ShapeDtypeStruct((batch, kvw, seq), bf16),
            jax.ShapeDtypeStruct((batch, seq // MOBA_BLOCK, kvw), f32),
        ],
        compiler_params=_cparams(("arbitrary", "arbitrary")),
        name="kprep",
    )(proj, proj, k_gain_row)


def _top3_penalty(gate, jb, axis):
    n = gate.shape[axis]
    pen = jnp.full(gate.shape, NEG, f32)
    for _ in range(MOBA_TOPK):
        m = jnp.max(gate, axis=axis, keepdims=True)
        idx = jnp.min(jnp.where(gate == m, jb, n), axis=axis, keepdims=True)
        idx = jnp.where(m > -jnp.inf, idx, n)
        hit = jb == idx
        pen = jnp.where(hit, 0.0, pen)
        gate = jnp.where(hit, -jnp.inf, gate)
    return pen


def _moba_prompt_kernel(q_ref, kb_ref, vt_ref, km_ref, qg_ref, sl_ref, o_ref,
                        qbd_ref, pen_ref, acc_ref, m_ref, s_ref, cm_ref, *, tq):
    qt = pl.program_id(2)
    nb_tile = tq // MOBA_BLOCK
    own = qt * nb_tile
    cols = Q_PER_KV * tq
    nblk = km_ref.shape[2]
    q_in_tile = lax.broadcasted_iota(jnp.int32, (1, cols), 1) % tq
    own_col = own + q_in_tile // MOBA_BLOCK

    qT = q_ref[...].T
    parts = []
    for r in range(Q_PER_KV):
        xr = qT[r * HEAD_DIM:(r + 1) * HEAD_DIM, :]
        ms = jnp.mean(xr * xr, axis=0, keepdims=True)
        parts.append(xr * lax.rsqrt(ms + EPS) * qg_ref[...])
    qn = jnp.concatenate(parts, axis=1)

    km = km_ref[0, 0]
    kh, kmid, kl = _split3(km)
    qh, qmid, ql = _split3(qn)
    gate = _dot(kh, qh) + _dot(kh, qmid) + _dot(kmid, qh) + _dot(kh, ql) + _dot(kl, qh) + _dot(kmid, qmid)
    jb = lax.broadcasted_iota(jnp.int32, (nblk, cols), 0)
    gate = jnp.where(jb < own_col, gate, -jnp.inf)
    pen_ref[...] = jnp.where(jb == own_col, 0.0, _top3_penalty(gate, jb, 0))

    slope = sl_ref[0]
    s_hi, s_mid, s_lo = _split3(slope)
    rowi = lax.broadcasted_iota(jnp.int32, (HEAD_DIM, cols), 0)
    slope_rows = jnp.where(rowi == 0, s_hi.astype(f32), jnp.where(rowi == 1, s_mid.astype(f32),
                           jnp.where(rowi == 2, s_lo.astype(f32), 0.0)))
    qbd_ref[...] = jnp.concatenate([qn * (HEAD_DIM ** -0.5 * LOG2E), slope_rows], axis=0).astype(bf16)

    kk = lax.broadcasted_iota(jnp.int32, (MOBA_BLOCK, cols), 0)
    ones = jnp.ones((2 * SUBLANES, MOBA_BLOCK), bf16)

    def blk(j):
        return pl.ds(pl.multiple_of(j * MOBA_BLOCK, MOBA_BLOCK), MOBA_BLOCK)

    def v_aug(j):
        return jnp.concatenate([vt_ref[0, :, blk(j)], ones], axis=0)

    def shift_of(j):
        return slope * ((j - own) * MOBA_BLOCK).astype(f32) + pen_ref[pl.ds(j, 1), :]

    def stage_scores(j, buf):
        s = _dot(kb_ref[0, 0, blk(j), :], qbd_ref[...])
        s_ref[buf] = s
        cm_ref[buf] = jnp.max(s, axis=0, keepdims=True)

    def stage_accum(j, buf, valid=None):
        shift = shift_of(j)
        if valid is not None:
            shift = shift + jnp.where(valid, 0.0, NEG)
        m_old = m_ref[...]
        m_new = jnp.maximum(m_old, cm_ref[buf] + shift)
        p = jnp.exp2(s_ref[buf] - (m_new - shift)).astype(bf16)
        acc_ref[...] = acc_ref[...] * jnp.exp2(m_old - m_new) + _dot(v_aug(j), p)
        m_ref[...] = m_new

    last = jnp.maximum(own - 1, 0)
    stage_scores(0, 1)
    col_blk = q_in_tile // MOBA_BLOCK
    causal = kk <= q_in_tile % MOBA_BLOCK
    ss, shifts = [], []
    for t in range(nb_tile):
        s = _dot(kb_ref[0, 0, blk(own + t), :], qbd_ref[...])
        visible = jnp.logical_or(col_blk > t, jnp.logical_and(col_blk == t, causal))
        ss.append(jnp.where(visible, s, NEG))
        shifts.append(shift_of(own + t) if nb_tile > 1 else jnp.zeros((1, cols), f32))
    m0 = None
    for s, sh in zip(ss, shifts):
        mj = jnp.max(s, axis=0, keepdims=True) + sh
        m0 = mj if m0 is None else jnp.maximum(m0, mj)
    pv0 = None
    for t, (s, sh) in enumerate(zip(ss, shifts)):
        pv = _dot(v_aug(own + t), jnp.exp2(s - (m0 - sh)).astype(bf16))
        pv0 = pv if pv0 is None else pv0 + pv
    acc_ref[...] = pv0
    m_ref[...] = m0


    def pair(j0):
        j1, j2 = j0 + 1, j0 + 2
        stage_scores(jnp.minimum(j1, last), 0)
        stage_accum(j0, 1)
        stage_scores(jnp.minimum(j2, last), 1)
        stage_accum(jnp.minimum(j1, last), 0, valid=j1 < own)

    def body8(i, c):
        for k in range(4):
            pair(8 * i + 2 * k)
        return c

    def body4(i, c):
        pair(8 * n8)
        pair(8 * n8 + 2)
        return c

    def body2(i, c):
        pair(8 * n8 + 4 * n4 + 2 * i)
        return c

    n8 = own // 8
    n4 = (own - 8 * n8) // 4
    lax.fori_loop(0, n8, body8, 0)
    lax.fori_loop(0, n4, body4, 0)
    lax.fori_loop(0, (own - 8 * n8 - 4 * n4 + 1) // 2, body2, 0)

    acc = acc_ref[...]
    o = acc[0:HEAD_DIM, :] / acc[HEAD_DIM:HEAD_DIM + 1, :]
    oT = jnp.concatenate([o[:, r * tq:(r + 1) * tq] for r in range(Q_PER_KV)], axis=0)
    o_ref[...] = oT.T.astype(o_ref.dtype)


def moba_prompt(proj, kb, vt, kmean_g, q_gain_col, slopes2, batch, seq, tq):
    nq = seq // tq
    qw = Q_PER_KV * HEAD_DIM
    kvw = N_KV_HEADS * HEAD_DIM
    nblk = seq // MOBA_BLOCK
    cols = Q_PER_KV * tq
    return pl.pallas_call(
        functools.partial(_moba_prompt_kernel, tq=tq),
        grid=(batch, N_KV_HEADS, nq),
        in_specs=[
            pl.BlockSpec((tq, qw), lambda b, g, i: (b * nq + i, OFF_Q // qw + g)),
            pl.BlockSpec((1, 1, seq, LANES), lambda b, g, i: (b, g, 0, 0)),
            pl.BlockSpec((1, HEAD_DIM, seq), lambda b, g, i: (b, g, 0)),
            pl.BlockSpec((1, 1, nblk, HEAD_DIM), lambda b, g, i: (b, g, 0, 0)),
            pl.BlockSpec((HEAD_DIM, tq), lambda b, g, i: (0, 0)),
            pl.BlockSpec((1, 1, cols), lambda b, g, i: (g, 0, 0)),
        ],
        out_specs=pl.BlockSpec((tq, qw), lambda b, g, i: (b * nq + i, g)),
        out_shape=jax.ShapeDtypeStruct((batch * seq, N_HEADS * HEAD_DIM), bf16),
        scratch_shapes=[
            pltpu.VMEM((LANES, cols), bf16),
            pltpu.VMEM((nblk, cols), f32),
            pltpu.VMEM((HEAD_DIM + 2 * SUBLANES, cols), f32),
            pltpu.VMEM((1, cols), f32),
            pltpu.VMEM((2, MOBA_BLOCK, cols), f32),
            pltpu.VMEM((2, 1, cols), f32),
        ],
        compiler_params=_cparams(("arbitrary", "arbitrary", "arbitrary")),
        name="moba_prompt",
    )(proj, kb, vt, kmean_g, q_gain_col, slopes2)


def _head_expand():
    r = lax.broadcasted_iota(jnp.int32, (LANES, D_INNER), 0)
    c = lax.broadcasted_iota(jnp.int32, (LANES, D_INNER), 1) // SSM_HEAD_DIM
    return jnp.where(r == c, 1.0, 0.0).astype(bf16)


def _ssd_conv_act(ext_ref, rows, cw_ref, cb_ref, base):
    conv = cb_ref[...] + cw_ref[0:1, :] * ext_ref[pl.ds(base, rows), :]
    for i in range(1, CONV_WIDTH):
        conv = conv + cw_ref[i:i + 1, :] * ext_ref[pl.ds(base + i, rows), :]
    return _silu(conv)


def _ssd_chunk(act, dt_raw, allowed, dtb_ref, alog_ref):
    rows = act.shape[0]
    xs = act[:, :D_INNER]
    gn = SSM_GROUPS * D_STATE
    bm = act[:, D_INNER:D_INNER + gn]
    cm = act[:, D_INNER + gn:]
    dt = _softplus(dt_raw + dtb_ref[...])
    a = -jnp.exp(alog_ref[...])
    mask_b = jnp.where(allowed, 1.0, 0.0).astype(bf16)
    cs = _dot_exact_lhs(mask_b, dt * (a * LOG2E))
    expand = _head_expand()
    dt_e = _dot_exact_rhs(dt, expand)
    cs_e = _dot_exact_rhs(cs, expand)
    xdt = xs * dt_e
    csT = cs.T
    xdt_b = xdt.astype(bf16)
    ypairs = []
    lane = lax.broadcasted_iota(jnp.int32, (rows, LANES), 1)
    for g in range(SSM_GROUPS):
        cb = _dot_nt(cm[:, g * D_STATE:(g + 1) * D_STATE].astype(bf16), bm[:, g * D_STATE:(g + 1) * D_STATE].astype(bf16))
        hpg = SSM_HEADS // SSM_GROUPS
        for pair in range(hpg // 2):
            res = []
            for k in range(2):
                h = g * hpg + pair * 2 + k
                seg = cs[:, h:h + 1] - csT[h:h + 1, :]
                m = (cb * jnp.exp2(jnp.where(allowed, seg, NEG))).astype(bf16)
                lo = (g * hpg + pair * 2) * SSM_HEAD_DIM
                res.append(_dot(m, xdt_b[:, lo:lo + LANES]))
            ypairs.append(jnp.where(lane < SSM_HEAD_DIM, res[0], res[1]))
    y_diag = jnp.concatenate(ypairs, axis=1)
    return xs, bm, cm, dt_e, cs_e, xdt, y_diag


def _ssd_finish(y, xs, z, dexp_ref, norm_ref):
    y = (y + dexp_ref[...] * xs) * _silu(z)
    gw = D_INNER // SSM_GROUPS
    outs = []
    for g in range(SSM_GROUPS):
        yg = y[:, g * gw:(g + 1) * gw]
        ms = jnp.mean(yg * yg, axis=-1, keepdims=True)
        outs.append(yg * lax.rsqrt(ms + EPS) * norm_ref[:, g * gw:(g + 1) * gw])
    return jnp.concatenate(outs, axis=1)


def _ssd_prompt_kernel(xbc_ref, z_ref, dt_ref, cw_ref, cb_ref, dtb_ref, alog_ref, dexp_ref, norm_ref,
                       y_ref, h_ref, halo_ref, ht_ref, *, rows):
    c = pl.program_id(1)

    @pl.when(c == 0)
    def _():
        halo_ref[...] = jnp.zeros((SUBLANES, CONV_DIM), f32)
        ht_ref[...] = jnp.zeros(ht_ref.shape, f32)

    xt = xbc_ref[...]
    halo = halo_ref[...]
    row8 = lax.broadcasted_iota(jnp.int32, (SUBLANES, CONV_DIM), 0)
    conv = cb_ref[...] + cw_ref[CONV_WIDTH - 1:CONV_WIDTH, :] * xt
    for sft in range(1, CONV_WIDTH):
        rolled = pltpu.roll(xt, sft, axis=0)
        head = jnp.where(row8 < sft, pltpu.roll(halo, sft, axis=0), rolled[0:SUBLANES, :])
        tap = CONV_WIDTH - 1 - sft
        conv = conv + cw_ref[tap:tap + 1, :] * jnp.concatenate([head, rolled[SUBLANES:, :]], axis=0)
    act = _silu(conv)
    halo_ref[...] = xt[rows - SUBLANES:rows, :]

    ri = lax.broadcasted_iota(jnp.int32, (rows, rows), 0)
    ci = lax.broadcasted_iota(jnp.int32, (rows, rows), 1)
    xs, bm, cm, dt_e, cs_e, xdt, y = _ssd_chunk(act, dt_ref[...], ci <= ri, dtb_ref, alog_ref)

    cs_last = cs_e[rows - 1:rows, :]
    ecs = jnp.exp2(cs_e)
    xdte = (xdt * jnp.exp2(cs_last - cs_e)).astype(bf16)
    dec = jnp.exp2(cs_last)
    hw = D_INNER // SSM_GROUPS
    yoff = []
    for g in range(SSM_GROUPS):
        ht_g = ht_ref[:, g * hw:(g + 1) * hw]
        yoff.append(_dot(cm[:, g * D_STATE:(g + 1) * D_STATE].astype(bf16), ht_g.astype(bf16)))
        upd = _dot_tn(bm[:, g * D_STATE:(g + 1) * D_STATE].astype(bf16), xdte[:, g * hw:(g + 1) * hw])
        ht_ref[:, g * hw:(g + 1) * hw] = ht_g * dec[:, g * hw:(g + 1) * hw] + upd
    y = y + jnp.concatenate(yoff, axis=1) * ecs
    y_ref[...] = _ssd_finish(y, xs, z_ref[...], dexp_ref, norm_ref).astype(y_ref.dtype)

    @pl.when(c == pl.num_programs(1) - 1)
    def _():
        h_ref[0] = ht_ref[...].T.reshape(SSM_HEADS, SSM_HEAD_DIM, D_STATE)


def ssd_prompt(proj, conv_w, conv_b_row, dtb_row, alog_row, dexp_row, norm_row, batch, seq, rows):
    nc = seq // rows
    return pl.pallas_call(
        functools.partial(_ssd_prompt_kernel, rows=rows),
        grid=(batch, nc),
        in_specs=[
            pl.BlockSpec((rows, CONV_DIM), lambda b, c: (b * nc + c, OFF_XBC // CONV_DIM)),
            pl.BlockSpec((rows, D_INNER), lambda b, c: (b * nc + c, OFF_Z // D_INNER)),
            pl.BlockSpec((rows, LANES), lambda b, c: (b * nc + c, OFF_DT // LANES)),
            pl.BlockSpec((CONV_WIDTH, CONV_DIM), lambda b, c: (0, 0)),
            pl.BlockSpec((1, CONV_DIM), lambda b, c: (0, 0)),
            pl.BlockSpec((1, LANES), lambda b, c: (0, 0)),
            pl.BlockSpec((1, LANES), lambda b, c: (0, 0)),
            pl.BlockSpec((1, D_INNER), lambda b, c: (0, 0)),
            pl.BlockSpec((1, D_INNER), lambda b, c: (0, 0)),
        ],
        out_specs=[
            pl.BlockSpec((rows, D_INNER), lambda b, c: (b * nc + c, 0)),
            pl.BlockSpec((1, SSM_HEADS, SSM_HEAD_DIM, D_STATE), lambda b, c: (b, 0, 0, 0)),
        ],
        out_shape=[
            jax.ShapeDtypeStruct((batch * seq, D_INNER), bf16),
            jax.ShapeDtypeStruct((batch, SSM_HEADS, SSM_HEAD_DIM, D_STATE), f32),
        ],
        scratch_shapes=[
            pltpu.VMEM((SUBLANES, CONV_DIM), f32),
            pltpu.VMEM((D_STATE, D_INNER), f32),
        ],
        compiler_params=_cparams(("arbitrary", "arbitrary")),
        name="ssd_prompt",
    )(proj, proj, proj, conv_w, conv_b_row, dtb_row, alog_row, dexp_row, norm_row)


def _mem_q(q_ref, g_ref, h, rows=slice(None)):
    q = q_ref[rows, h * MEM_HEAD_DIM:(h + 1) * MEM_HEAD_DIM]
    ms = jnp.mean(q * q, axis=-1, keepdims=True)
    return (q * lax.rsqrt(ms + EPS) * g_ref[...] * (MEM_HEAD_DIM ** -0.5)).astype(bf16)


def _mem_attend_kernel(q_ref, mk_ref, mv_ref, g_ref, o_ref, *, head_axis):
    if head_axis:
        nseq_step = mk_ref.shape[0]
        tq = q_ref.shape[0] // nseq_step
        nrow = MEM_TOKENS * MEM_HEADS
        rh = lax.broadcasted_iota(jnp.int32, (MEM_HEADS * tq, nrow), 0) // tq
        ch = lax.broadcasted_iota(jnp.int32, (MEM_HEADS * tq, nrow), 1) % MEM_HEADS
        own_head = rh == ch
        for i in range(nseq_step):
            rows = slice(i * tq, (i + 1) * tq)
            q4 = jnp.concatenate([_mem_q(q_ref, g_ref, h, rows) for h in range(MEM_HEADS)], axis=0)
            s = _dot_nt(q4, mk_ref[i].reshape(nrow, MEM_HEAD_DIM).astype(bf16))
            s = jnp.where(own_head, s, NEG)
            m = jnp.max(s, axis=-1, keepdims=True)
            p = jnp.exp(s - m)
            l = jnp.sum(p, axis=-1, keepdims=True)
            o = _dot(p.astype(bf16), mv_ref[i].reshape(nrow, MEM_HEAD_DIM).astype(bf16)) / l
            o_ref[rows, :] = jnp.concatenate([o[h * tq:(h + 1) * tq, :] for h in range(MEM_HEADS)], axis=1)
        return
    outs = []
    for h in range(MEM_HEADS):
        lo = h * MEM_HEAD_DIM
        s = _dot_nt(_mem_q(q_ref, g_ref, h), mk_ref[0, :, lo:lo + MEM_HEAD_DIM].astype(bf16))
        m = jnp.max(s, axis=-1, keepdims=True)
        p = jnp.exp(s - m)
        l = jnp.sum(p, axis=-1, keepdims=True)
        outs.append(_dot(p.astype(bf16), mv_ref[0, :, lo:lo + MEM_HEAD_DIM].astype(bf16)) / l)
    o_ref[...] = jnp.concatenate(outs, axis=1).astype(o_ref.dtype)


def mem_attend(proj, mk, mv, mq_gain_row, nseq, rows_per_seq, tq):
    w = MEM_HEADS * MEM_HEAD_DIM
    head_axis = mk.ndim == 4
    if head_axis:
        assert tq == rows_per_seq and nseq % MEM_SEQS_PER_STEP == 0
        steps, tq, nq = nseq // MEM_SEQS_PER_STEP, MEM_SEQS_PER_STEP * rows_per_seq, 1
        kv_spec = pl.BlockSpec((MEM_SEQS_PER_STEP, MEM_TOKENS, MEM_HEADS, MEM_HEAD_DIM), lambda b, i: (b, 0, 0, 0))
    else:
        steps, nq = nseq, rows_per_seq // tq
        kv_spec = pl.BlockSpec((1, MEM_TOKENS, w), lambda b, i: (b, 0, 0))
    return pl.pallas_call(
        functools.partial(_mem_attend_kernel, head_axis=head_axis),
        grid=(steps, nq),
        in_specs=[
            pl.BlockSpec((tq, w), lambda b, i: (b * nq + i, OFF_MQ // w)),
            kv_spec,
            kv_spec,
            pl.BlockSpec((1, MEM_HEAD_DIM), lambda b, i: (0, 0)),
        ],
        out_specs=pl.BlockSpec((tq, w), lambda b, i: (b * nq + i, 0)),
        out_shape=jax.ShapeDtypeStruct((nseq * rows_per_seq, w), f32 if head_axis else bf16),
        compiler_params=_cparams(("arbitrary", "arbitrary")),
        name="mem_attend",
    )(proj, mk, mv, mq_gain_row)


def _merge_kernel(x_ref, a_ref, s_ref, m_ref, ga_ref, gs_ref, gm_ref, wa_ref, ws_ref, wm_ref, wo_ref, o_ref):
    mixed = _sigmoid(ga_ref[...].astype(f32)) * _dot(a_ref[...].astype(bf16), wa_ref[...])
    mixed = mixed + _sigmoid(gs_ref[...].astype(f32)) * _dot(s_ref[...].astype(bf16), ws_ref[...])
    mixed = mixed + _sigmoid(gm_ref[...].astype(f32)) * _dot(m_ref[...].astype(bf16), wm_ref[...])
    o_ref[...] = x_ref[...] + _dot(mixed.astype(bf16), wo_ref[...])


def merge(x, attn_o, ssm_o, mem_o, gates, wa, ws, wm, wo, tm):
    t, d = x.shape
    row = lambda i: (i, 0)
    const = lambda i: (0, 0)
    gate_spec = lambda k: pl.BlockSpec((tm, d), lambda i: (i, k))
    w_spec = pl.BlockSpec((d, d), const)
    return pl.pallas_call(
        _merge_kernel,
        grid=(t // tm,),
        in_specs=[pl.BlockSpec((tm, d), row)] * 4 + [gate_spec(0), gate_spec(1), gate_spec(2)] + [w_spec] * 4,
        out_specs=pl.BlockSpec((tm, d), row),
        out_shape=jax.ShapeDtypeStruct((t, d), f32),
        compiler_params=_cparams(("arbitrary",)),
        name="merge",
    )(x, attn_o, ssm_o, mem_o, gates, gates, gates, wa, ws, wm, wo)


def _ffn_kernel(x_ref, g_ref, wg_ref, wu_ref, wd_ref, o_ref):
    x = x_ref[...]
    ms = jnp.mean(x * x, axis=-1, keepdims=True)
    h = (x * lax.rsqrt(ms + EPS) * g_ref[...]).astype(bf16)
    act = _silu(_dot(h, wg_ref[...])) * _dot(h, wu_ref[...])
    o_ref[...] = x + _dot(act.astype(bf16), wd_ref[...])


def ffn(x, gain_row, wg, wu, wd, tm):
    t, d = x.shape
    dff = wg.shape[1]
    const = lambda i: (0, 0)
    single = pl.Buffered(1)
    return pl.pallas_call(
        _ffn_kernel,
        grid=(t // tm,),
        in_specs=[
            pl.BlockSpec((tm, d), lambda i: (i, 0)),
            pl.BlockSpec((1, d), const),
            pl.BlockSpec((d, dff), const, pipeline_mode=single),
            pl.BlockSpec((d, dff), const, pipeline_mode=single),
            pl.BlockSpec((dff, d), const, pipeline_mode=single),
        ],
        out_specs=pl.BlockSpec((tm, d), lambda i: (i, 0)),
        out_shape=jax.ShapeDtypeStruct((t, d), f32),
        compiler_params=_cparams(("arbitrary",)),
        name="ffn",
    )(x, gain_row, wg, wu, wd)


def _moba_sample_kernel(pt_ref, q_ref, kn_ref, vn_ref, qg_ref, sl_ref, ck_hbm, cv_hbm, o_ref,
                        kbuf, vbuf, s_ref, sem, *, n_pages):
    b = pl.program_id(0)
    nseq = pl.num_programs(0)
    kvw = N_KV_HEADS * HEAD_DIM
    half = SAMPLE_ROWS // 2
    nrow = Q_PER_KV * N_KV_HEADS * half
    nblk = n_pages * PAGE_SIZE // MOBA_BLOCK
    ppb = MOBA_BLOCK // PAGE_SIZE
    past = n_pages * PAGE_SIZE
    slot = b % 2

    def k_copy(seq, sl, p):
        return pltpu.make_async_copy(ck_hbm.at[pt_ref[seq, p]], kbuf.at[sl, p], sem.at[0, sl])

    def v_copy(seq, sl, p):
        return pltpu.make_async_copy(cv_hbm.at[pt_ref[seq, p]], vbuf.at[sl, p], sem.at[1, sl])

    def start_all(seq, sl):
        for p in range(n_pages):
            k_copy(seq, sl, p).start()
        for p in range(n_pages):
            v_copy(seq, sl, p).start()

    @pl.when(b == 0)
    def _():
        start_all(0, 0)

    nxt = jnp.minimum(b + 1, nseq - 1)

    qb = q_ref[...]
    lane_g = lax.broadcasted_iota(jnp.int32, (SAMPLE_ROWS, kvw), 1) // HEAD_DIM
    low_half = lax.broadcasted_iota(jnp.int32, (SAMPLE_ROWS, kvw), 0) < half
    pieces = []
    for r in range(Q_PER_KV):
        slab = qb[:, r * kvw:(r + 1) * kvw]
        for g in range(N_KV_HEADS):
            pieces.append(jnp.where(lane_g == g, slab, 0.0))
    tiles = [jnp.where(low_half, pieces[2 * k], pltpu.roll(pieces[2 * k + 1], half, axis=0))
             for k in range(len(pieces) // 2)]
    qbd = jnp.concatenate(tiles, axis=0)
    ms = jnp.sum(qbd * qbd, axis=-1, keepdims=True) * (1.0 / HEAD_DIM)
    qn = qbd * lax.rsqrt(ms + EPS) * qg_ref[...]
    qs = (qn * (HEAD_DIM ** -0.5 * LOG2E)).astype(bf16)

    for p in range(n_pages):
        k_copy(b, slot, p).wait()

    def k_page(p):
        return kbuf[slot, p].reshape(kvw, PAGE_SIZE)

    lane = lax.broadcasted_iota(jnp.int32, (kvw, LANES), 1)
    kmean_t = jnp.zeros((kvw, LANES), f32)
    for j in range(nblk):
        blk = k_page(j * ppb)
        for pp in range(1, ppb):
            blk = blk + k_page(j * ppb + pp)
        kmean_t = kmean_t + jnp.where(lane == j, jnp.sum(blk, axis=-1, keepdims=True) * (1.0 / MOBA_BLOCK), 0.0)
    kh, kmid, kl = _split3(kmean_t)
    qh, qmid, ql = _split3(qn)
    gate = _dot(qh, kh) + _dot(qh, kmid) + _dot(qmid, kh) + _dot(qh, kl) + _dot(ql, kh) + _dot(qmid, kmid)
    jb = lax.broadcasted_iota(jnp.int32, (nrow, LANES), 1)
    gate = jnp.where(jb < nblk, gate, -jnp.inf)
    pen = _top3_penalty(gate, jb, 1)

    slope = sl_ref[...]
    kk = lax.broadcasted_iota(jnp.int32, (nrow, PAGE_SIZE), 1)
    pages_per_dot = 4
    for p0 in range(0, n_pages, pages_per_dot):
        for p in range(p0, p0 + pages_per_dot):
            k_copy(nxt, 1 - slot, p).start()
        k4 = jnp.concatenate([k_page(p).astype(bf16) for p in range(p0, p0 + pages_per_dot)], axis=1)
        s4 = _dot(qs, k4)
        for i in range(pages_per_dot):
            p = p0 + i
            j = p // ppb
            rel = (kk + (p * PAGE_SIZE - past)).astype(f32)
            s_ref[:, p * PAGE_SIZE:(p + 1) * PAGE_SIZE] = (
                s4[:, i * PAGE_SIZE:(i + 1) * PAGE_SIZE] + slope * rel + pen[:, j:j + 1])

    zpad = jnp.zeros((LANES - SAMPLE_ROWS, kvw), f32)
    knew = jnp.concatenate([kn_ref[...], zpad], axis=0).astype(bf16)
    vnew = jnp.concatenate([vn_ref[...], zpad], axis=0).astype(bf16)
    kn_lane = lax.broadcasted_iota(jnp.int32, (nrow, LANES), 1)
    tok = lax.broadcasted_iota(jnp.int32, (nrow, LANES), 0) % half
    s_own = _dot_nt(qs, knew) + slope * kn_lane.astype(f32)
    s_ref[:, past:past + LANES] = jnp.where(kn_lane <= tok, s_own, NEG)

    m = jnp.max(s_ref[...], axis=-1, keepdims=True)
    p_own = jnp.exp2(s_ref[:, past:past + LANES] - m)
    n_acc = 4
    psums = [p_own] + [None] * (n_acc - 1)
    accs = [_dot(p_own.astype(bf16), vnew)] + [None] * (n_acc - 1)

    for p in range(n_pages):
        v_copy(b, slot, p).wait()

    for p0 in range(0, n_pages, ppb):
        for p in range(p0, p0 + ppb):
            v_copy(nxt, 1 - slot, p).start()
        pp = jnp.exp2(s_ref[:, p0 * PAGE_SIZE:(p0 + ppb) * PAGE_SIZE] - m)
        vb = jnp.concatenate([vbuf[slot, p].reshape(kvw, PAGE_SIZE).astype(bf16) for p in range(p0, p0 + ppb)], axis=1)
        pv = _dot_nt(pp.astype(bf16), vb)
        a = (p0 // ppb) % n_acc
        pl_sum = pp[:, :PAGE_SIZE]
        for i in range(1, ppb):
            pl_sum = pl_sum + pp[:, i * PAGE_SIZE:(i + 1) * PAGE_SIZE]
        psums[a] = pl_sum if psums[a] is None else psums[a] + pl_sum
        accs[a] = pv if accs[a] is None else accs[a] + pv
    acc = (accs[0] + accs[1]) + (accs[2] + accs[3])
    psum = (psums[0] + psums[1]) + (psums[2] + psums[3])
    o = acc / jnp.sum(psum, axis=-1, keepdims=True)

    @pl.when(b == nseq - 1)
    def _():
        for p in range(n_pages):
            k_copy(nxt, 1 - slot, p).wait()
        for p in range(n_pages):
            v_copy(nxt, 1 - slot, p).wait()

    slabs = []
    for r in range(Q_PER_KV):
        slab = jnp.zeros((SAMPLE_ROWS, kvw), f32)
        for g in range(N_KV_HEADS):
            piece = r * N_KV_HEADS + g
            tile = o[(piece // 2) * SAMPLE_ROWS:(piece // 2 + 1) * SAMPLE_ROWS, :]
            if piece % 2:
                tile = pltpu.roll(tile, half, axis=0)
            slab = slab + jnp.where(lane_g == g, tile, 0.0)
        slabs.append(slab)
    o_ref[...] = jnp.concatenate(slabs, axis=1)


def moba_sample(page_table, proj, kn_new, ck, cv, q_gain_row, slope_rows, nseq):
    kvw = N_KV_HEADS * HEAD_DIM
    qw = N_HEADS * HEAD_DIM
    n_pages = page_table.shape[1]
    nrow = Q_PER_KV * N_KV_HEADS * (SAMPLE_ROWS // 2)
    past = n_pages * PAGE_SIZE
    grid_spec = pltpu.PrefetchScalarGridSpec(
        num_scalar_prefetch=1,
        grid=(nseq,),
        in_specs=[
            pl.BlockSpec((SAMPLE_ROWS, qw), lambda b, pt: (b, OFF_Q // qw)),
            pl.BlockSpec((SAMPLE_ROWS, kvw), lambda b, pt: (b, 0)),
            pl.BlockSpec((SAMPLE_ROWS, kvw), lambda b, pt: (b, OFF_V // kvw)),
            pl.BlockSpec((1, kvw), lambda b, pt: (0, 0)),
            pl.BlockSpec((nrow, LANES), lambda b, pt: (0, 0)),
            pl.BlockSpec(memory_space=pl.ANY),
            pl.BlockSpec(memory_space=pl.ANY),
        ],
        out_specs=pl.BlockSpec((SAMPLE_ROWS, qw), lambda b, pt: (b, 0)),
        scratch_shapes=[
            pltpu.VMEM((2, n_pages, N_KV_HEADS, HEAD_DIM, PAGE_SIZE), f32),
            pltpu.VMEM((2, n_pages, N_KV_HEADS, HEAD_DIM, PAGE_SIZE), f32),
            pltpu.VMEM((nrow, past + LANES), f32),
            pltpu.SemaphoreType.DMA((2, 2)),
        ],
    )
    return pl.pallas_call(
        functools.partial(_moba_sample_kernel, n_pages=n_pages),
        grid_spec=grid_spec,
        out_shape=jax.ShapeDtypeStruct((nseq * SAMPLE_ROWS, qw), f32),
        compiler_params=_cparams(("arbitrary",)),
        name="moba_sample",
    )(page_table, proj, kn_new, proj, q_gain_row, slope_rows, ck, cv)


def _ssd_sample_kernel(ext_ref, z_ref, dt_ref, h0_ref, cw_ref, cb_ref, dtb_ref, alog_ref, dexp_ref, norm_ref,
                       y_ref, h_ref, scr_ref, *, nseq_step, n_new):
    rows = nseq_step * SAMPLE_ROWS
    scr_ref[pl.ds(0, rows), :] = ext_ref[...]
    scr_ref[pl.ds(rows, SUBLANES), :] = jnp.zeros((SUBLANES, CONV_DIM), f32)
    act = _ssd_conv_act(scr_ref, rows, cw_ref, cb_ref, base=0)

    ri = lax.broadcasted_iota(jnp.int32, (rows, rows), 0)
    ci = lax.broadcasted_iota(jnp.int32, (rows, rows), 1)
    allowed = jnp.logical_and(ri // SAMPLE_ROWS == ci // SAMPLE_ROWS,
                              jnp.logical_and(ci % SAMPLE_ROWS <= ri % SAMPLE_ROWS, ci % SAMPLE_ROWS < n_new))
    xs, bm, cm, dt_e, cs_e, xdt, y = _ssd_chunk(act, dt_ref[...], allowed, dtb_ref, alog_ref)

    valid = lax.broadcasted_iota(jnp.int32, (SAMPLE_ROWS, 1), 0) < n_new
    ecs = jnp.exp2(cs_e)
    hw = D_INNER // SSM_GROUPS
    hpg = SSM_HEADS // SSM_GROUPS
    yoff_rows = []
    for i in range(nseq_step):
        lo = i * SAMPLE_ROWS
        cs_i = cs_e[lo:lo + SAMPLE_ROWS, :]
        cs_last = cs_e[lo + n_new - 1:lo + n_new, :]
        xdte = jnp.where(valid, xdt[lo:lo + SAMPLE_ROWS, :] * jnp.exp2(cs_last - cs_i), 0.0)
        yg = []
        upd = []
        for g in range(SSM_GROUPS):
            h0g = h0_ref[i, g * hpg:(g + 1) * hpg].reshape(hw, D_STATE)
            yg.append(_dot_nt(cm[lo:lo + SAMPLE_ROWS, g * D_STATE:(g + 1) * D_STATE], h0g))
            upd.append(_dot_tn(xdte[:, g * hw:(g + 1) * hw], bm[lo:lo + SAMPLE_ROWS, g * D_STATE:(g + 1) * D_STATE]))
        yoff_rows.append(jnp.concatenate(yg, axis=1))
        dec_row = jnp.exp2(cs_last)
        for h in range(SSM_HEADS):
            dec_h = jnp.broadcast_to(dec_row[:, h * SSM_HEAD_DIM:h * SSM_HEAD_DIM + 1], (SSM_HEAD_DIM, D_STATE))
            r = h % hpg
            h_ref[i, h] = h0_ref[i, h] * dec_h + upd[h // hpg][r * SSM_HEAD_DIM:(r + 1) * SSM_HEAD_DIM, :]
    y = y + jnp.concatenate(yoff_rows, axis=0) * ecs
    y_ref[...] = _ssd_finish(y, xs, z_ref[...], dexp_ref, norm_ref).astype(y_ref.dtype)


def ssd_sample(ext, proj, h0, conv_w, conv_b_row, dtb_row, alog_row, dexp_row, norm_row, nseq, nseq_step, n_new):
    rows = nseq_step * SAMPLE_ROWS
    const = lambda i: (0, 0)
    state_spec = pl.BlockSpec((nseq_step, SSM_HEADS, SSM_HEAD_DIM, D_STATE), lambda i: (i, 0, 0, 0))
    return pl.pallas_call(
        functools.partial(_ssd_sample_kernel, nseq_step=nseq_step, n_new=n_new),
        grid=(nseq // nseq_step,),
        in_specs=[
            pl.BlockSpec((rows, CONV_DIM), lambda i: (i, 0)),
            pl.BlockSpec((rows, D_INNER), lambda i: (i, OFF_Z // D_INNER)),
            pl.BlockSpec((rows, LANES), lambda i: (i, OFF_DT // LANES)),
            state_spec,
            pl.BlockSpec((CONV_WIDTH, CONV_DIM), const),
            pl.BlockSpec((1, CONV_DIM), const),
            pl.BlockSpec((1, LANES), const),
            pl.BlockSpec((1, LANES), const),
            pl.BlockSpec((1, D_INNER), const),
            pl.BlockSpec((1, D_INNER), const),
        ],
        out_specs=[pl.BlockSpec((rows, D_INNER), lambda i: (i, 0)), state_spec],
        out_shape=[
            jax.ShapeDtypeStruct((nseq * SAMPLE_ROWS, D_INNER), f32),
            jax.ShapeDtypeStruct(h0.shape, f32),
        ],
        scratch_shapes=[pltpu.VMEM((rows + SUBLANES, CONV_DIM), f32)],
        compiler_params=_cparams(("arbitrary",)),
        name="ssd_sample",
    )(ext, proj, proj, h0, conv_w, conv_b_row, dtb_row, alog_row, dexp_row, norm_row)


IN_SIZES = (N_HEADS * HEAD_DIM, N_KV_HEADS * HEAD_DIM, N_KV_HEADS * HEAD_DIM, D_INNER, CONV_DIM, SSM_HEADS,
            MEM_HEADS * MEM_HEAD_DIM, 3 * D_MODEL)
TM = 512
TM_PROJ = 1024
TN_PROJ = 1920
TQ = 256
SSD_ROWS = 256
SSD_SAMPLE_SEQS = 8
MEM_SEQS_PER_STEP = 4


def _pad_lanes(v):
    return jnp.pad(v, (0, LANES - v.shape[0])).reshape(1, LANES)


def kernel(x_prompt, x_sample, mem_prompt, cache_k, cache_v, page_table, state_conv, state_ssm, cache_mem_k, cache_mem_v, norm_mix, w_in, q_norm, k_norm, conv_w, conv_b, dt_bias, a_log, d_skip, ssm_norm, mem_norm, w_mem_kv, mq_norm, mk_norm, w_attn_br, w_ssm_br, w_mem_br, w_out, norm_ffn, w_gate, w_up, w_down):
    assert w_in.shape[0] == 1, "single layer"
    assert x_sample.shape[1] <= SAMPLE_ROWS // 2, "sample MoBA packs two heads' tokens into one row tile"
    batch, seq, d = x_prompt.shape
    nseq, n_new, _ = x_sample.shape
    kvw = N_KV_HEADS * HEAD_DIM
    qw = N_HEADS * HEAD_DIM

    offs = np.cumsum(IN_SIZES)[:-1].tolist()
    wq, wk, wv, wz, wxbc, wdt, wmq, wgates = jnp.split(w_in[0], offs, axis=1)
    wdt = jnp.pad(wdt, ((0, 0), (0, LANES - SSM_HEADS)))
    tail = [wz, wxbc, wmq, wk, wv, wdt]
    w_prompt = jnp.concatenate([wq] + tail, axis=1).astype(bf16)
    wq_rgd = wq.reshape(d, N_KV_HEADS, Q_PER_KV, HEAD_DIM).transpose(0, 2, 1, 3).reshape(d, qw)
    w_sample = jnp.concatenate([wq_rgd] + tail, axis=1).astype(bf16)
    w_gates = wgates.astype(bf16)
    wa = w_attn_br[0].astype(bf16)
    wa_rgd = w_attn_br[0].reshape(N_KV_HEADS, Q_PER_KV, HEAD_DIM, d).transpose(1, 0, 2, 3).reshape(qw, d).astype(bf16)
    ws, wm, wo = w_ssm_br[0].astype(bf16), w_mem_br[0].astype(bf16), w_out[0].astype(bf16)
    wg, wu, wd = w_gate[0].astype(bf16), w_up[0].astype(bf16), w_down[0].astype(bf16)

    k_gain = jnp.tile(k_norm[0], N_KV_HEADS).reshape(1, kvw)
    conv_b_row = conv_b[0].reshape(1, CONV_DIM)
    dtb_row, alog_row = _pad_lanes(dt_bias[0]), _pad_lanes(a_log[0])
    dexp_row = jnp.repeat(d_skip[0], SSM_HEAD_DIM).reshape(1, D_INNER)
    ssm_norm_row = ssm_norm[0].reshape(1, D_INNER)
    mq_gain = mq_norm[0].reshape(1, MEM_HEAD_DIM)
    ffn_gain = norm_ffn[0].reshape(1, d)
    slopes2 = jnp.exp2(-8.0 * jnp.arange(1, N_HEADS + 1, dtype=f32) / N_HEADS) * LOG2E

    xp = x_prompt.reshape(batch * seq, d)
    proj = inproj(xp, norm_mix[0], w_prompt, TM_PROJ, TN_PROJ)
    knt, vt32, kb, vt, kmean = kprep(proj, k_gain, batch, seq, 2048)
    kmean_g = kmean.reshape(batch, seq // MOBA_BLOCK, N_KV_HEADS, HEAD_DIM).transpose(0, 2, 1, 3)
    slope_cols = jnp.repeat(slopes2.reshape(N_KV_HEADS, Q_PER_KV), TQ, axis=1).reshape(N_KV_HEADS, 1, Q_PER_KV * TQ)
    q_gain_col = jnp.broadcast_to(q_norm[0][:, None], (HEAD_DIM, TQ))
    attn_o = moba_prompt(proj, kb, vt, kmean_g, q_gain_col, slope_cols, batch, seq, TQ)
    ssm_o, h_prompt = ssd_prompt(proj, conv_w[0], conv_b_row, dtb_row, alog_row, dexp_row, ssm_norm_row,
                                 batch, seq, SSD_ROWS)
    mem_kv = inproj(mem_prompt.reshape(batch * MEM_TOKENS, d), mem_norm[0], w_mem_kv[0].astype(bf16), TM, qw)
    mk = headnorm(mem_kv, 0, qw, jnp.tile(mk_norm[0], MEM_HEADS).reshape(1, qw), MEM_HEAD_DIM, TM)
    mv = mem_kv[:, qw:]
    mem_o = mem_attend(proj, mk.reshape(batch, MEM_TOKENS, qw), mv.reshape(batch, MEM_TOKENS, qw), mq_gain,
                       batch, seq, TM)
    gates = inproj(xp, norm_mix[0], w_gates, TM_PROJ, 3 * d, bf16)
    x1 = merge(xp, attn_o, ssm_o, mem_o, gates, wa, ws, wm, wo, TM)
    y_prompt = ffn(x1, ffn_gain, wg, wu, wd, TM).reshape(batch, seq, d)

    k_prompt = jnp.transpose(knt.reshape(batch, N_KV_HEADS, HEAD_DIM, seq), (0, 3, 1, 2))[None]
    v_prompt = jnp.transpose(vt32.reshape(batch, N_KV_HEADS, HEAD_DIM, seq), (0, 3, 1, 2))[None]
    conv_prompt = proj.reshape(batch, seq, N_PROJ)[:, seq - (CONV_WIDTH - 1):, OFF_XBC:OFF_XBC + CONV_DIM][None]
    mem_k_prompt = mk.reshape(1, batch, MEM_TOKENS, MEM_HEADS, MEM_HEAD_DIM)
    mem_v_prompt = mv.reshape(1, batch, MEM_TOKENS, MEM_HEADS, MEM_HEAD_DIM)

    xs = jnp.pad(x_sample, ((0, 0), (0, SAMPLE_ROWS - n_new), (0, 0))).reshape(nseq * SAMPLE_ROWS, d)
    proj_s = inproj(xs, norm_mix[0], w_sample, TM_PROJ, TN_PROJ)
    kn_s = headnorm(proj_s, OFF_K // kvw, kvw, k_gain, HEAD_DIM, TM)
    slope_rows = jnp.broadcast_to(
        jnp.repeat(slopes2.reshape(N_KV_HEADS, Q_PER_KV).T.reshape(-1), SAMPLE_ROWS // 2)[:, None],
        (N_HEADS * (SAMPLE_ROWS // 2), LANES))
    ck_t = jnp.transpose(cache_k[0], (0, 2, 3, 1))
    cv_t = jnp.transpose(cache_v[0], (0, 2, 3, 1))
    attn_s = moba_sample(page_table, proj_s, kn_s, ck_t, cv_t, jnp.tile(q_norm[0], N_KV_HEADS).reshape(1, kvw),
                         slope_rows, nseq)
    xbc_new = proj_s[:, OFF_XBC:OFF_XBC + CONV_DIM].reshape(nseq, SAMPLE_ROWS, CONV_DIM)[:, :n_new]
    ext = jnp.concatenate(
        [state_conv[0], xbc_new, jnp.zeros((nseq, SAMPLE_ROWS - (CONV_WIDTH - 1) - n_new, CONV_DIM), f32)], axis=1)
    ssm_s, h_sample = ssd_sample(ext.reshape(nseq * SAMPLE_ROWS, CONV_DIM), proj_s, state_ssm[0], conv_w[0], conv_b_row,
                                 dtb_row, alog_row, dexp_row, ssm_norm_row, nseq, SSD_SAMPLE_SEQS, n_new)
    mem_s = mem_attend(proj_s, cache_mem_k[0], cache_mem_v[0], mq_gain, nseq, SAMPLE_ROWS, SAMPLE_ROWS)
    gates_s = inproj(xs, norm_mix[0], w_gates, TM_PROJ, 3 * d, bf16)
    x1s = merge(xs, attn_s, ssm_s, mem_s, gates_s, wa_rgd, ws, wm, wo, TM)
    y_sample = ffn(x1s, ffn_gain, wg, wu, wd, TM).reshape(nseq, SAMPLE_ROWS, d)[:, :n_new]

    k_sample = kn_s.reshape(nseq, SAMPLE_ROWS, N_KV_HEADS, HEAD_DIM)[:, :n_new][None]
    v_sample = proj_s[:, OFF_V:OFF_V + kvw].reshape(nseq, SAMPLE_ROWS, N_KV_HEADS, HEAD_DIM)[:, :n_new][None]
    conv_sample = ext[:, n_new:n_new + CONV_WIDTH - 1][None]

    return (y_prompt, y_sample, k_prompt, v_prompt, conv_prompt, h_prompt[None], mem_k_prompt, mem_v_prompt,
            k_sample, v_sample, conv_sample, h_sample[None])
```

```python
import functools

import jax
import jax.numpy as jnp
import numpy as np
from jax import lax
from jax.experimental import pallas as pl
from jax.experimental.pallas import tpu as pltpu

f32 = jnp.float32
bf16 = jnp.bfloat16

D_MODEL = 1024
N_HEADS = 16
N_KV_HEADS = 4
HEAD_DIM = 64
Q_PER_KV = N_HEADS // N_KV_HEADS
MOBA_BLOCK = 256
MOBA_TOPK = 3
SSM_HEADS = 16
SSM_HEAD_DIM = 64
D_INNER = SSM_HEADS * SSM_HEAD_DIM
SSM_GROUPS = 4
D_STATE = 128
CONV_WIDTH = 4
CONV_DIM = D_INNER + 2 * SSM_GROUPS * D_STATE
MEM_TOKENS = 256
MEM_HEADS = 4
MEM_HEAD_DIM = 256
EPS = 1e-6
PAGE_SIZE = 128

LANES = 128
SUBLANES = 8
LOG2E = 1.4426950408889634
NEG = -1e30
VMEM_LIMIT = 56 * 1024 * 1024

OFF_Q, OFF_Z, OFF_XBC, OFF_MQ, OFF_K, OFF_V, OFF_DT = 0, 1024, 2048, 4096, 5120, 5376, 5632
N_PROJ = 5760
SAMPLE_ROWS = 8


def _cparams(sem):
    return pltpu.CompilerParams(dimension_semantics=sem, vmem_limit_bytes=VMEM_LIMIT)


def _split3(x):
    h = x.astype(bf16)
    r = x - h.astype(f32)
    m = r.astype(bf16)
    l = (r - m.astype(f32)).astype(bf16)
    return h, m, l


def _dot(a, b):
    return jnp.dot(a, b, preferred_element_type=f32)


def _dot_nt(a, b):
    return lax.dot_general(a, b, (((1,), (1,)), ((), ())), preferred_element_type=f32)


def _dot_tn(a, b):
    return lax.dot_general(a, b, (((0,), (0,)), ((), ())), preferred_element_type=f32)


def _dot_exact_rhs(x, m_bf16):
    h, m, l = _split3(x)
    return _dot(h, m_bf16) + _dot(m, m_bf16) + _dot(l, m_bf16)


def _dot_exact_lhs(m_bf16, x):
    h, m, l = _split3(x)
    return _dot(m_bf16, h) + _dot(m_bf16, m) + _dot(m_bf16, l)


def _sigmoid(x):
    return 0.5 * jnp.tanh(0.5 * x) + 0.5


def _silu(x):
    return x * _sigmoid(x)


def _softplus(x):
    return jnp.maximum(x, 0.0) + jnp.log(1.0 + jnp.exp(-jnp.abs(x)))


def _inproj_kernel(x_ref, g_ref, w_ref, o_ref):
    x = x_ref[...]
    ms = jnp.mean(x * x, axis=-1, keepdims=True)
    xn = (x * lax.rsqrt(ms + EPS) * g_ref[...]).astype(bf16)
    o_ref[...] = _dot(xn, w_ref[...]).astype(o_ref.dtype)


def inproj(x, gain, w, tm, tn, out_dtype=f32):
    t, d = x.shape
    n = w.shape[1]
    return pl.pallas_call(
        _inproj_kernel,
        grid=(n // tn, t // tm),
        in_specs=[
            pl.BlockSpec((tm, d), lambda j, i: (i, 0)),
            pl.BlockSpec((1, d), lambda j, i: (0, 0)),
            pl.BlockSpec((d, tn), lambda j, i: (0, j)),
        ],
        out_specs=pl.BlockSpec((tm, tn), lambda j, i: (i, j)),
        out_shape=jax.ShapeDtypeStruct((t, n), out_dtype),
        compiler_params=_cparams(("arbitrary", "arbitrary")),
        name="inproj",
    )(x, gain.reshape(1, d), w)


def _seg_ones(width, seg):
    r = lax.broadcasted_iota(jnp.int32, (width, width), 0) // seg
    c = lax.broadcasted_iota(jnp.int32, (width, width), 1) // seg
    return jnp.where(r == c, 1.0, 0.0).astype(bf16)


def _seg_rmsnorm(x, gain_row, seg):
    ss = _dot_exact_rhs(x * x, _seg_ones(x.shape[1], seg))
    return x * lax.rsqrt(ss * (1.0 / seg) + EPS) * gain_row


def _headnorm_kernel(x_ref, g_ref, o_ref, *, seg):
    o_ref[...] = _seg_rmsnorm(x_ref[...], g_ref[...], seg)


def headnorm(x, col_block, width, gain_row, seg, tm):
    t = x.shape[0]
    return pl.pallas_call(
        functools.partial(_headnorm_kernel, seg=seg),
        grid=(t // tm,),
        in_specs=[
            pl.BlockSpec((tm, width), lambda i: (i, col_block)),
            pl.BlockSpec((1, width), lambda i: (0, 0)),
        ],
        out_specs=pl.BlockSpec((tm, width), lambda i: (i, 0)),
        out_shape=jax.ShapeDtypeStruct((t, width), f32),
        compiler_params=_cparams(("arbitrary",)),
        name="headnorm",
    )(x, gain_row)


def _kprep_kernel(k_ref, v_ref, g_ref, knt_ref, vt32_ref, ka_ref, vt_ref, km_ref):
    kn = _seg_rmsnorm(k_ref[...], g_ref[...], HEAD_DIM)
    knt_ref[0] = kn.T
    kn_b = kn.astype(bf16)
    rows = kn.shape[0]
    kvw = kn.shape[1]
    r = lax.broadcasted_iota(jnp.int32, (kvw, LANES), 0)
    c = lax.broadcasted_iota(jnp.int32, (kvw, LANES), 1)
    lane = lax.broadcasted_iota(jnp.int32, (rows, LANES), 1)
    pos = (lax.broadcasted_iota(jnp.int32, (rows, LANES), 0) % MOBA_BLOCK).astype(f32)
    pos_lanes = jnp.where(jnp.logical_and(lane >= HEAD_DIM, lane < HEAD_DIM + 3), pos, 0.0)
    for g in range(N_KV_HEADS):
        pick = jnp.where(jnp.logical_and(r == c + g * HEAD_DIM, c < HEAD_DIM), 1.0, 0.0).astype(bf16)
        ka_ref[0, g] = (_dot(kn_b, pick) + pos_lanes).astype(bf16)
    nb = kn.shape[0] // MOBA_BLOCK
    km_ref[0] = jnp.sum(kn.reshape(nb, MOBA_BLOCK, kn.shape[1]), axis=1) * (1.0 / MOBA_BLOCK)
    vt = v_ref[...].T
    vt32_ref[0] = vt
    vt_ref[0] = vt.astype(bf16)


def kprep(proj, k_gain_row, batch, seq, tm):
    kvw = N_KV_HEADS * HEAD_DIM
    nt = seq // tm
    nb = tm // MOBA_BLOCK
    return pl.pallas_call(
        _kprep_kernel,
        grid=(batch, nt),
        in_specs=[
            pl.BlockSpec((tm, kvw), lambda b, i: (b * nt + i, OFF_K // kvw)),
            pl.BlockSpec((tm, kvw), lambda b, i: (b * nt + i, OFF_V // kvw)),
            pl.BlockSpec((1, kvw), lambda b, i: (0, 0)),
        ],
        out_specs=[
            pl.BlockSpec((1, kvw, tm), lambda b, i: (b, 0, i)),
            pl.BlockSpec((1, kvw, tm), lambda b, i: (b, 0, i)),
            pl.BlockSpec((1, N_KV_HEADS, tm, LANES), lambda b, i: (b, 0, i, 0)),
            pl.BlockSpec((1, kvw, tm), lambda b, i: (b, 0, i)),
            pl.BlockSpec((1, nb, kvw), lambda b, i: (b, i, 0)),
        ],
        out_shape=[
            jax.ShapeDtypeStruct((batch, kvw, seq), f32),
            jax.ShapeDtypeStruct((batch, kvw, seq), f32),
            jax.ShapeDtypeStruct((batch, N_KV_HEADS, seq, LANES), bf16),
            jax.ShapeDtypeStruct((batch, kvw, seq), bf16),
            jax.ShapeDtypeStruct((batch, seq // MOBA_BLOCK, kvw), f32),
        ],
        compiler_params=_cparams(("arbitrary", "arbitrary")),
        name="kprep",
    )(proj, proj, k_gain_row)


def _top3_penalty(gate, jb, axis):
    n = gate.shape[axis]
    pen = jnp.full(gate.shape, NEG, f32)
    for _ in range(MOBA_TOPK):
        m = jnp.max(gate, axis=axis, keepdims=True)
        idx = jnp.min(jnp.where(gate == m, jb, n), axis=axis, keepdims=True)
        idx = jnp.where(m > -jnp.inf, idx, n)
        hit = jb == idx
        pen = jnp.where(hit, 0.0, pen)
        gate = jnp.where(hit, -jnp.inf, gate)
    return pen


def _moba_prompt_kernel(q_ref, kb_ref, vt_ref, km_ref, qg_ref, sl_ref, o_ref,
                        qbd_ref, pen_ref, acc_ref, m_ref, s_ref, cm_ref, *, tq):
    qt = pl.program_id(2)
    nb_tile = tq // MOBA_BLOCK
    own = qt * nb_tile
    cols = Q_PER_KV * tq
    nblk = km_ref.shape[2]
    q_in_tile = lax.broadcasted_iota(jnp.int32, (1, cols), 1) % tq
    own_col = own + q_in_tile // MOBA_BLOCK

    qT = q_ref[...].T
    parts = []
    for r in range(Q_PER_KV):
        xr = qT[r * HEAD_DIM:(r + 1) * HEAD_DIM, :]
        ms = jnp.mean(xr * xr, axis=0, keepdims=True)
        parts.append(xr * lax.rsqrt(ms + EPS) * qg_ref[...])
    qn = jnp.concatenate(parts, axis=1)

    km = km_ref[0, 0]
    kh, kmid, kl = _split3(km)
    qh, qmid, ql = _split3(qn)
    gate = _dot(kh, qh) + _dot(kh, qmid) + _dot(kmid, qh) + _dot(kh, ql) + _dot(kl, qh) + _dot(kmid, qmid)
    jb = lax.broadcasted_iota(jnp.int32, (nblk, cols), 0)
    gate = jnp.where(jb < own_col, gate, -jnp.inf)
    pen_ref[...] = jnp.where(jb == own_col, 0.0, _top3_penalty(gate, jb, 0))

    slope = sl_ref[0]
    s_hi, s_mid, s_lo = _split3(slope)
    rowi = lax.broadcasted_iota(jnp.int32, (HEAD_DIM, cols), 0)
    slope_rows = jnp.where(rowi == 0, s_hi.astype(f32), jnp.where(rowi == 1, s_mid.astype(f32),
                           jnp.where(rowi == 2, s_lo.astype(f32), 0.0)))
    qbd_ref[...] = jnp.concatenate([qn * (HEAD_DIM ** -0.5 * LOG2E), slope_rows], axis=0).astype(bf16)

    kk = lax.broadcasted_iota(jnp.int32, (MOBA_BLOCK, cols), 0)
    ones = jnp.ones((2 * SUBLANES, MOBA_BLOCK), bf16)

    def blk(j):
        return pl.ds(pl.multiple_of(j * MOBA_BLOCK, MOBA_BLOCK), MOBA_BLOCK)

    def v_aug(j):
        return jnp.concatenate([vt_ref[0, :, blk(j)], ones], axis=0)

    def shift_of(j):
        return slope * ((j - own) * MOBA_BLOCK).astype(f32) + pen_ref[pl.ds(j, 1), :]

    def stage_scores(j, buf):
        s = _dot(kb_ref[0, 0, blk(j), :], qbd_ref[...])
        s_ref[buf] = s
        cm_ref[buf] = jnp.max(s, axis=0, keepdims=True)

    def stage_accum(j, buf, valid=None):
        shift = shift_of(j)
        if valid is not None:
            shift = shift + jnp.where(valid, 0.0, NEG)
        m_old = m_ref[...]
        m_new = jnp.maximum(m_old, cm_ref[buf] + shift)
        p = jnp.exp2(s_ref[buf] - (m_new - shift)).astype(bf16)
        acc_ref[...] = acc_ref[...] * jnp.exp2(m_old - m_new) + _dot(v_aug(j), p)
        m_ref[...] = m_new

    last = jnp.maximum(own - 1, 0)
    stage_scores(0, 1)
    col_blk = q_in_tile // MOBA_BLOCK
    causal = kk <= q_in_tile % MOBA_BLOCK
    ss, shifts = [], []
    for t in range(nb_tile):
        s = _dot(kb_ref[0, 0, blk(own + t), :], qbd_ref[...])
        visible = jnp.logical_or(col_blk > t, jnp.logical_and(col_blk == t, causal))
        ss.append(jnp.where(visible, s, NEG))
        shifts.append(shift_of(own + t) if nb_tile > 1 else jnp.zeros((1, cols), f32))
    m0 = None
    for s, sh in zip(ss, shifts):
        mj = jnp.max(s, axis=0, keepdims=True) + sh
        m0 = mj if m0 is None else jnp.maximum(m0, mj)
    pv0 = None
    for t, (s, sh) in enumerate(zip(ss, shifts)):
        pv = _dot(v_aug(own + t), jnp.exp2(s - (m0 - sh)).astype(bf16))
        pv0 = pv if pv0 is None else pv0 + pv
    acc_ref[...] = pv0
    m_ref[...] = m0


    def pair(j0):
        j1, j2 = j0 + 1, j0 + 2
        stage_scores(jnp.minimum(j1, last), 0)
        stage_accum(j0, 1)
        stage_scores(jnp.minimum(j2, last), 1)
        stage_accum(jnp.minimum(j1, last), 0, valid=j1 < own)

    def body8(i, c):
        for k in range(4):
            pair(8 * i + 2 * k)
        return c

    def body4(i, c):
        pair(8 * n8)
        pair(8 * n8 + 2)
        return c

    def body2(i, c):
        pair(8 * n8 + 4 * n4 + 2 * i)
        return c

    n8 = own // 8
    n4 = (own - 8 * n8) // 4
    lax.fori_loop(0, n8, body8, 0)
    lax.fori_loop(0, n4, body4, 0)
    lax.fori_loop(0, (own - 8 * n8 - 4 * n4 + 1) // 2, body2, 0)

    acc = acc_ref[...]
    o = acc[0:HEAD_DIM, :] / acc[HEAD_DIM:HEAD_DIM + 1, :]
    oT = jnp.concatenate([o[:, r * tq:(r + 1) * tq] for r in range(Q_PER_KV)], axis=0)
    o_ref[...] = oT.T.astype(o_ref.dtype)


def moba_prompt(proj, kb, vt, kmean_g, q_gain_col, slopes2, batch, seq, tq):
    nq = seq // tq
    qw = Q_PER_KV * HEAD_DIM
    kvw = N_KV_HEADS * HEAD_DIM
    nblk = seq // MOBA_BLOCK
    cols = Q_PER_KV * tq
    return pl.pallas_call(
        functools.partial(_moba_prompt_kernel, tq=tq),
        grid=(batch, N_KV_HEADS, nq),
        in_specs=[
            pl.BlockSpec((tq, qw), lambda b, g, i: (b * nq + i, OFF_Q // qw + g)),
            pl.BlockSpec((1, 1, seq, LANES), lambda b, g, i: (b, g, 0, 0)),
            pl.BlockSpec((1, HEAD_DIM, seq), lambda b, g, i: (b, g, 0)),
            pl.BlockSpec((1, 1, nblk, HEAD_DIM), lambda b, g, i: (b, g, 0, 0)),
            pl.BlockSpec((HEAD_DIM, tq), lambda b, g, i: (0, 0)),
            pl.BlockSpec((1, 1, cols), lambda b, g, i: (g, 0, 0)),
        ],
        out_specs=pl.BlockSpec((tq, qw), lambda b, g, i: (b * nq + i, g)),
        out_shape=jax.ShapeDtypeStruct((batch * seq, N_HEADS * HEAD_DIM), bf16),
        scratch_shapes=[
            pltpu.VMEM((LANES, cols), bf16),
            pltpu.VMEM((nblk, cols), f32),
            pltpu.VMEM((HEAD_DIM + 2 * SUBLANES, cols), f32),
            pltpu.VMEM((1, cols), f32),
            pltpu.VMEM((2, MOBA_BLOCK, cols), f32),
            pltpu.VMEM((2, 1, cols), f32),
        ],
        compiler_params=_cparams(("arbitrary", "arbitrary", "arbitrary")),
        name="moba_prompt",
    )(proj, kb, vt, kmean_g, q_gain_col, slopes2)


def _head_expand():
    r = lax.broadcasted_iota(jnp.int32, (LANES, D_INNER), 0)
    c = lax.broadcasted_iota(jnp.int32, (LANES, D_INNER), 1) // SSM_HEAD_DIM
    return jnp.where(r == c, 1.0, 0.0).astype(bf16)


def _ssd_conv_act(ext_ref, rows, cw_ref, cb_ref, base):
    conv = cb_ref[...] + cw_ref[0:1, :] * ext_ref[pl.ds(base, rows), :]
    for i in range(1, CONV_WIDTH):
        conv = conv + cw_ref[i:i + 1, :] * ext_ref[pl.ds(base + i, rows), :]
    return _silu(conv)


def _ssd_chunk(act, dt_raw, allowed, dtb_ref, alog_ref):
    rows = act.shape[0]
    xs = act[:, :D_INNER]
    gn = SSM_GROUPS * D_STATE
    bm = act[:, D_INNER:D_INNER + gn]
    cm = act[:, D_INNER + gn:]
    dt = _softplus(dt_raw + dtb_ref[...])
    a = -jnp.exp(alog_ref[...])
    mask_b = jnp.where(allowed, 1.0, 0.0).astype(bf16)
    cs = _dot_exact_lhs(mask_b, dt * (a * LOG2E))
    expand = _head_expand()
    dt_e = _dot_exact_rhs(dt, expand)
    cs_e = _dot_exact_rhs(cs, expand)
    xdt = xs * dt_e
    csT = cs.T
    xdt_b = xdt.astype(bf16)
    ypairs = []
    lane = lax.broadcasted_iota(jnp.int32, (rows, LANES), 1)
    for g in range(SSM_GROUPS):
        cb = _dot_nt(cm[:, g * D_STATE:(g + 1) * D_STATE].astype(bf16), bm[:, g * D_STATE:(g + 1) * D_STATE].astype(bf16))
        hpg = SSM_HEADS // SSM_GROUPS
        for pair in range(hpg // 2):
            res = []
            for k in range(2):
                h = g * hpg + pair * 2 + k
                seg = cs[:, h:h + 1] - csT[h:h + 1, :]
                m = (cb * jnp.exp2(jnp.where(allowed, seg, NEG))).astype(bf16)
                lo = (g * hpg + pair * 2) * SSM_HEAD_DIM
                res.append(_dot(m, xdt_b[:, lo:lo + LANES]))
            ypairs.append(jnp.where(lane < SSM_HEAD_DIM, res[0], res[1]))
    y_diag = jnp.concatenate(ypairs, axis=1)
    return xs, bm, cm, dt_e, cs_e, xdt, y_diag


def _ssd_finish(y, xs, z, dexp_ref, norm_ref):
    y = (y + dexp_ref[...] * xs) * _silu(z)
    gw = D_INNER // SSM_GROUPS
    outs = []
    for g in range(SSM_GROUPS):
        yg = y[:, g * gw:(g + 1) * gw]
        ms = jnp.mean(yg * yg, axis=-1, keepdims=True)
        outs.append(yg * lax.rsqrt(ms + EPS) * norm_ref[:, g * gw:(g + 1) * gw])
    return jnp.concatenate(outs, axis=1)


def _ssd_prompt_kernel(xbc_ref, z_ref, dt_ref, cw_ref, cb_ref, dtb_ref, alog_ref, dexp_ref, norm_ref,
                       y_ref, h_ref, halo_ref, ht_ref, *, rows):
    c = pl.program_id(1)

    @pl.when(c == 0)
    def _():
        halo_ref[...] = jnp.zeros((SUBLANES, CONV_DIM), f32)
        ht_ref[...] = jnp.zeros(ht_ref.shape, f32)

    xt = xbc_ref[...]
    halo = halo_ref[...]
    row8 = lax.broadcasted_iota(jnp.int32, (SUBLANES, CONV_DIM), 0)
    conv = cb_ref[...] + cw_ref[CONV_WIDTH - 1:CONV_WIDTH, :] * xt
    for sft in range(1, CONV_WIDTH):
        rolled = pltpu.roll(xt, sft, axis=0)
        head = jnp.where(row8 < sft, pltpu.roll(halo, sft, axis=0), rolled[0:SUBLANES, :])
        tap = CONV_WIDTH - 1 - sft
        conv = conv + cw_ref[tap:tap + 1, :] * jnp.concatenate([head, rolled[SUBLANES:, :]], axis=0)
    act = _silu(conv)
    halo_ref[...] = xt[rows - SUBLANES:rows, :]

    ri = lax.broadcasted_iota(jnp.int32, (rows, rows), 0)
    ci = lax.broadcasted_iota(jnp.int32, (rows, rows), 1)
    xs, bm, cm, dt_e, cs_e, xdt, y = _ssd_chunk(act, dt_ref[...], ci <= ri, dtb_ref, alog_ref)

    cs_last = cs_e[rows - 1:rows, :]
    ecs = jnp.exp2(cs_e)
    xdte = (xdt * jnp.exp2(cs_last - cs_e)).astype(bf16)
    dec = jnp.exp2(cs_last)
    hw = D_INNER // SSM_GROUPS
    yoff = []
    for g in range(SSM_GROUPS):
        ht_g = ht_ref[:, g * hw:(g + 1) * hw]
        yoff.append(_dot(cm[:, g * D_STATE:(g + 1) * D_STATE].astype(bf16), ht_g.astype(bf16)))
        upd = _dot_tn(bm[:, g * D_STATE:(g + 1) * D_STATE].astype(bf16), xdte[:, g * hw:(g + 1) * hw])
        ht_ref[:, g * hw:(g + 1) * hw] = ht_g * dec[:, g * hw:(g + 1) * hw] + upd
    y = y + jnp.concatenate(yoff, axis=1) * ecs
    y_ref[...] = _ssd_finish(y, xs, z_ref[...], dexp_ref, norm_ref).astype(y_ref.dtype)

    @pl.when(c == pl.num_programs(1) - 1)
    def _():
        h_ref[0] = ht_ref[...].T.reshape(SSM_HEADS, SSM_HEAD_DIM, D_STATE)


def ssd_prompt(proj, conv_w, conv_b_row, dtb_row, alog_row, dexp_row, norm_row, batch, seq, rows):
    nc = seq // rows
    return pl.pallas_call(
        functools.partial(_ssd_prompt_kernel, rows=rows),
        grid=(batch, nc),
        in_specs=[
            pl.BlockSpec((rows, CONV_DIM), lambda b, c: (b * nc + c, OFF_XBC // CONV_DIM)),
            pl.BlockSpec((rows, D_INNER), lambda b, c: (b * nc + c, OFF_Z // D_INNER)),
            pl.BlockSpec((rows, LANES), lambda b, c: (b * nc + c, OFF_DT // LANES)),
            pl.BlockSpec((CONV_WIDTH, CONV_DIM), lambda b, c: (0, 0)),
            pl.BlockSpec((1, CONV_DIM), lambda b, c: (0, 0)),
            pl.BlockSpec((1, LANES), lambda b, c: (0, 0)),
            pl.BlockSpec((1, LANES), lambda b, c: (0, 0)),
            pl.BlockSpec((1, D_INNER), lambda b, c: (0, 0)),
            pl.BlockSpec((1, D_INNER), lambda b, c: (0, 0)),
        ],
        out_specs=[
            pl.BlockSpec((rows, D_INNER), lambda b, c: (b * nc + c, 0)),
            pl.BlockSpec((1, SSM_HEADS, SSM_HEAD_DIM, D_STATE), lambda b, c: (b, 0, 0, 0)),
        ],
        out_shape=[
            jax.ShapeDtypeStruct((batch * seq, D_INNER), bf16),
            jax.ShapeDtypeStruct((batch, SSM_HEADS, SSM_HEAD_DIM, D_STATE), f32),
        ],
        scratch_shapes=[
            pltpu.VMEM((SUBLANES, CONV_DIM), f32),
            pltpu.VMEM((D_STATE, D_INNER), f32),
        ],
        compiler_params=_cparams(("arbitrary", "arbitrary")),
        name="ssd_prompt",
    )(proj, proj, proj, conv_w, conv_b_row, dtb_row, alog_row, dexp_row, norm_row)


def _mem_q(q_ref, g_ref, h, rows=slice(None)):
    q = q_ref[rows, h * MEM_HEAD_DIM:(h + 1) * MEM_HEAD_DIM]
    ms = jnp.mean(q * q, axis=-1, keepdims=True)
    return (q * lax.rsqrt(ms + EPS) * g_ref[...] * (MEM_HEAD_DIM ** -0.5 * LOG2E)).astype(bf16)


def _mem_attend_kernel(q_ref, mk_ref, mv_ref, g_ref, o_ref, *, head_axis):
    if head_axis:
        nseq_step = mk_ref.shape[0]
        tq = q_ref.shape[0] // nseq_step
        nrow = MEM_TOKENS * MEM_HEADS
        rh = lax.broadcasted_iota(jnp.int32, (MEM_HEADS * tq, nrow), 0) // tq
        ch = lax.broadcasted_iota(jnp.int32, (MEM_HEADS * tq, nrow), 1) % MEM_HEADS
        own_head = rh == ch
        for i in range(nseq_step):
            rows = slice(i * tq, (i + 1) * tq)
            q4 = jnp.concatenate([_mem_q(q_ref, g_ref, h, rows) for h in range(MEM_HEADS)], axis=0)
            s = _dot_nt(q4, mk_ref[i].reshape(nrow, MEM_HEAD_DIM).astype(bf16))
            s = jnp.where(own_head, s, NEG)
            m = jnp.max(s, axis=-1, keepdims=True)
            p = jnp.exp2(s - m)
            l = jnp.sum(p, axis=-1, keepdims=True)
            o = _dot(p.astype(bf16), mv_ref[i].reshape(nrow, MEM_HEAD_DIM).astype(bf16)) / l
            o_ref[rows, :] = jnp.concatenate([o[h * tq:(h + 1) * tq, :] for h in range(MEM_HEADS)], axis=1)
        return
    outs = []
    for h in range(MEM_HEADS):
        lo = h * MEM_HEAD_DIM
        s = _dot_nt(_mem_q(q_ref, g_ref, h), mk_ref[0, :, lo:lo + MEM_HEAD_DIM].astype(bf16))
        m = jnp.max(s, axis=-1, keepdims=True)
        p = jnp.exp2(s - m)
        l = jnp.sum(p, axis=-1, keepdims=True)
        outs.append(_dot(p.astype(bf16), mv_ref[0, :, lo:lo + MEM_HEAD_DIM].astype(bf16)) / l)
    o_ref[...] = jnp.concatenate(outs, axis=1).astype(o_ref.dtype)


def mem_attend(proj, mk, mv, mq_gain_row, nseq, rows_per_seq, tq):
    w = MEM_HEADS * MEM_HEAD_DIM
    head_axis = mk.ndim == 4
    if head_axis:
        assert tq == rows_per_seq and nseq % MEM_SEQS_PER_STEP == 0
        steps, tq, nq = nseq // MEM_SEQS_PER_STEP, MEM_SEQS_PER_STEP * rows_per_seq, 1
        kv_spec = pl.BlockSpec((MEM_SEQS_PER_STEP, MEM_TOKENS, MEM_HEADS, MEM_HEAD_DIM), lambda b, i: (b, 0, 0, 0))
    else:
        steps, nq = nseq, rows_per_seq // tq
        kv_spec = pl.BlockSpec((1, MEM_TOKENS, w), lambda b, i: (b, 0, 0))
    return pl.pallas_call(
        functools.partial(_mem_attend_kernel, head_axis=head_axis),
        grid=(steps, nq),
        in_specs=[
            pl.BlockSpec((tq, w), lambda b, i: (b * nq + i, OFF_MQ // w)),
            kv_spec,
            kv_spec,
            pl.BlockSpec((1, MEM_HEAD_DIM), lambda b, i: (0, 0)),
        ],
        out_specs=pl.BlockSpec((tq, w), lambda b, i: (b * nq + i, 0)),
        out_shape=jax.ShapeDtypeStruct((nseq * rows_per_seq, w), f32 if head_axis else bf16),
        compiler_params=_cparams(("arbitrary", "arbitrary")),
        name="mem_attend",
    )(proj, mk, mv, mq_gain_row)


def _merge_kernel(x_ref, a_ref, s_ref, m_ref, ga_ref, gs_ref, gm_ref, wa_ref, ws_ref, wm_ref, wo_ref, o_ref):
    mixed = _sigmoid(ga_ref[...].astype(f32)) * _dot(a_ref[...].astype(bf16), wa_ref[...])
    mixed = mixed + _sigmoid(gs_ref[...].astype(f32)) * _dot(s_ref[...].astype(bf16), ws_ref[...])
    mixed = mixed + _sigmoid(gm_ref[...].astype(f32)) * _dot(m_ref[...].astype(bf16), wm_ref[...])
    o_ref[...] = x_ref[...] + _dot(mixed.astype(bf16), wo_ref[...])


def merge(x, attn_o, ssm_o, mem_o, gates, wa, ws, wm, wo, tm):
    t, d = x.shape
    row = lambda i: (i, 0)
    const = lambda i: (0, 0)
    gate_spec = lambda k: pl.BlockSpec((tm, d), lambda i: (i, k))
    w_spec = pl.BlockSpec((d, d), const)
    return pl.pallas_call(
        _merge_kernel,
        grid=(t // tm,),
        in_specs=[pl.BlockSpec((tm, d), row)] * 4 + [gate_spec(0), gate_spec(1), gate_spec(2)] + [w_spec] * 4,
        out_specs=pl.BlockSpec((tm, d), row),
        out_shape=jax.ShapeDtypeStruct((t, d), f32),
        compiler_params=_cparams(("arbitrary",)),
        name="merge",
    )(x, attn_o, ssm_o, mem_o, gates, gates, gates, wa, ws, wm, wo)


def _ffn_kernel(x_ref, g_ref, wg_ref, wu_ref, wd_ref, o_ref):
    x = x_ref[...]
    ms = jnp.mean(x * x, axis=-1, keepdims=True)
    h = (x * lax.rsqrt(ms + EPS) * g_ref[...]).astype(bf16)
    act = _silu(_dot(h, wg_ref[...])) * _dot(h, wu_ref[...])
    o_ref[...] = x + _dot(act.astype(bf16), wd_ref[...])


def ffn(x, gain_row, wg, wu, wd, tm):
    t, d = x.shape
    dff = wg.shape[1]
    const = lambda i: (0, 0)
    single = pl.Buffered(1)
    return pl.pallas_call(
        _ffn_kernel,
        grid=(t // tm,),
        in_specs=[
            pl.BlockSpec((tm, d), lambda i: (i, 0)),
            pl.BlockSpec((1, d), const),
            pl.BlockSpec((d, dff), const, pipeline_mode=single),
            pl.BlockSpec((d, dff), const, pipeline_mode=single),
            pl.BlockSpec((dff, d), const, pipeline_mode=single),
        ],
        out_specs=pl.BlockSpec((tm, d), lambda i: (i, 0)),
        out_shape=jax.ShapeDtypeStruct((t, d), f32),
        compiler_params=_cparams(("arbitrary",)),
        name="ffn",
    )(x, gain_row, wg, wu, wd)


def _moba_sample_kernel(pt_ref, q_ref, kn_ref, vn_ref, qg_ref, sl_ref, ck_hbm, cv_hbm, o_ref,
                        kbuf, vbuf, s_ref, sem, *, n_pages):
    b = pl.program_id(0)
    nseq = pl.num_programs(0)
    kvw = N_KV_HEADS * HEAD_DIM
    half = SAMPLE_ROWS // 2
    nrow = Q_PER_KV * N_KV_HEADS * half
    nblk = n_pages * PAGE_SIZE // MOBA_BLOCK
    ppb = MOBA_BLOCK // PAGE_SIZE
    past = n_pages * PAGE_SIZE
    slot = b % 2

    def k_copy(seq, sl, p):
        return pltpu.make_async_copy(ck_hbm.at[pt_ref[seq, p]], kbuf.at[sl, p], sem.at[0, sl])

    def v_copy(seq, sl, p):
        return pltpu.make_async_copy(cv_hbm.at[pt_ref[seq, p]], vbuf.at[sl, p], sem.at[1, sl])

    def start_all(seq, sl):
        for p in range(n_pages):
            k_copy(seq, sl, p).start()
        for p in range(n_pages):
            v_copy(seq, sl, p).start()

    @pl.when(b == 0)
    def _():
        start_all(0, 0)

    nxt = jnp.minimum(b + 1, nseq - 1)

    qb = q_ref[...]
    lane_g = lax.broadcasted_iota(jnp.int32, (SAMPLE_ROWS, kvw), 1) // HEAD_DIM
    low_half = lax.broadcasted_iota(jnp.int32, (SAMPLE_ROWS, kvw), 0) < half
    pieces = []
    for r in range(Q_PER_KV):
        slab = qb[:, r * kvw:(r + 1) * kvw]
        for g in range(N_KV_HEADS):
            pieces.append(jnp.where(lane_g == g, slab, 0.0))
    tiles = [jnp.where(low_half, pieces[2 * k], pltpu.roll(pieces[2 * k + 1], half, axis=0))
             for k in range(len(pieces) // 2)]
    qbd = jnp.concatenate(tiles, axis=0)
    ms = jnp.sum(qbd * qbd, axis=-1, keepdims=True) * (1.0 / HEAD_DIM)
    qn = qbd * lax.rsqrt(ms + EPS) * qg_ref[...]
    qs = (qn * (HEAD_DIM ** -0.5 * LOG2E)).astype(bf16)

    for p in range(n_pages):
        k_copy(b, slot, p).wait()

    def k_page(p):
        return kbuf[slot, p].reshape(kvw, PAGE_SIZE)

    lane = lax.broadcasted_iota(jnp.int32, (kvw, LANES), 1)
    kmean_t = jnp.zeros((kvw, LANES), f32)
    for j in range(nblk):
        blk = k_page(j * ppb)
        for pp in range(1, ppb):
            blk = blk + k_page(j * ppb + pp)
        kmean_t = kmean_t + jnp.where(lane == j, jnp.sum(blk, axis=-1, keepdims=True) * (1.0 / MOBA_BLOCK), 0.0)
    kh, kmid, kl = _split3(kmean_t)
    qh, qmid, ql = _split3(qn)
    gate = _dot(qh, kh) + _dot(qh, kmid) + _dot(qmid, kh) + _dot(qh, kl) + _dot(ql, kh) + _dot(qmid, kmid)
    jb = lax.broadcasted_iota(jnp.int32, (nrow, LANES), 1)
    gate = jnp.where(jb < nblk, gate, -jnp.inf)
    pen = _top3_penalty(gate, jb, 1)

    slope = sl_ref[...]
    kk = lax.broadcasted_iota(jnp.int32, (nrow, PAGE_SIZE), 1)
    pages_per_dot = 4
    for p0 in range(0, n_pages, pages_per_dot):
        for p in range(p0, p0 + pages_per_dot):
            k_copy(nxt, 1 - slot, p).start()
        k4 = jnp.concatenate([k_page(p).astype(bf16) for p in range(p0, p0 + pages_per_dot)], axis=1)
        s4 = _dot(qs, k4)
        for i in range(pages_per_dot):
            p = p0 + i
            j = p // ppb
            rel = (kk + (p * PAGE_SIZE - past)).astype(f32)
            s_ref[:, p * PAGE_SIZE:(p + 1) * PAGE_SIZE] = (
                s4[:, i * PAGE_SIZE:(i + 1) * PAGE_SIZE] + slope * rel + pen[:, j:j + 1])

    zpad = jnp.zeros((LANES - SAMPLE_ROWS, kvw), f32)
    knew = jnp.concatenate([kn_ref[...], zpad], axis=0).astype(bf16)
    vnew = jnp.concatenate([vn_ref[...], zpad], axis=0).astype(bf16)
    kn_lane = lax.broadcasted_iota(jnp.int32, (nrow, LANES), 1)
    tok = lax.broadcasted_iota(jnp.int32, (nrow, LANES), 0) % half
    s_own = _dot_nt(qs, knew) + slope * kn_lane.astype(f32)
    s_ref[:, past:past + LANES] = jnp.where(kn_lane <= tok, s_own, NEG)

    m = jnp.max(s_ref[...], axis=-1, keepdims=True)
    p_own = jnp.exp2(s_ref[:, past:past + LANES] - m)
    n_acc = 4
    psums = [p_own] + [None] * (n_acc - 1)
    accs = [_dot(p_own.astype(bf16), vnew)] + [None] * (n_acc - 1)

    for p in range(n_pages):
        v_copy(b, slot, p).wait()

    for p0 in range(0, n_pages, ppb):
        for p in range(p0, p0 + ppb):
            v_copy(nxt, 1 - slot, p).start()
        pp = jnp.exp2(s_ref[:, p0 * PAGE_SIZE:(p0 + ppb) * PAGE_SIZE] - m)
        vb = jnp.concatenate([vbuf[slot, p].reshape(kvw, PAGE_SIZE).astype(bf16) for p in range(p0, p0 + ppb)], axis=1)
        pv = _dot_nt(pp.astype(bf16), vb)
        a = (p0 // ppb) % n_acc
        pl_sum = pp[:, :PAGE_SIZE]
        for i in range(1, ppb):
            pl_sum = pl_sum + pp[:, i * PAGE_SIZE:(i + 1) * PAGE_SIZE]
        psums[a] = pl_sum if psums[a] is None else psums[a] + pl_sum
        accs[a] = pv if accs[a] is None else accs[a] + pv
    acc = (accs[0] + accs[1]) + (accs[2] + accs[3])
    psum = (psums[0] + psums[1]) + (psums[2] + psums[3])
    o = acc / jnp.sum(psum, axis=-1, keepdims=True)

    @pl.when(b == nseq - 1)
    def _():
        for p in range(n_pages):
            k_copy(nxt, 1 - slot, p).wait()
        for p in range(n_pages):
            v_copy(nxt, 1 - slot, p).wait()

    slabs = []
    for r in range(Q_PER_KV):
        slab = jnp.zeros((SAMPLE_ROWS, kvw), f32)
        for g in range(N_KV_HEADS):
            piece = r * N_KV_HEADS + g
            tile = o[(piece // 2) * SAMPLE_ROWS:(piece // 2 + 1) * SAMPLE_ROWS, :]
            if piece % 2:
                tile = pltpu.roll(tile, half, axis=0)
            slab = slab + jnp.where(lane_g == g, tile, 0.0)
        slabs.append(slab)
    o_ref[...] = jnp.concatenate(slabs, axis=1)


def moba_sample(page_table, proj, kn_new, ck, cv, q_gain_row, slope_rows, nseq):
    kvw = N_KV_HEADS * HEAD_DIM
    qw = N_HEADS * HEAD_DIM
    n_pages = page_table.shape[1]
    nrow = Q_PER_KV * N_KV_HEADS * (SAMPLE_ROWS // 2)
    past = n_pages * PAGE_SIZE
    grid_spec = pltpu.PrefetchScalarGridSpec(
        num_scalar_prefetch=1,
        grid=(nseq,),
        in_specs=[
            pl.BlockSpec((SAMPLE_ROWS, qw), lambda b, pt: (b, OFF_Q // qw)),
            pl.BlockSpec((SAMPLE_ROWS, kvw), lambda b, pt: (b, 0)),
            pl.BlockSpec((SAMPLE_ROWS, kvw), lambda b, pt: (b, OFF_V // kvw)),
            pl.BlockSpec((1, kvw), lambda b, pt: (0, 0)),
            pl.BlockSpec((nrow, LANES), lambda b, pt: (0, 0)),
            pl.BlockSpec(memory_space=pl.ANY),
            pl.BlockSpec(memory_space=pl.ANY),
        ],
        out_specs=pl.BlockSpec((SAMPLE_ROWS, qw), lambda b, pt: (b, 0)),
        scratch_shapes=[
            pltpu.VMEM((2, n_pages, N_KV_HEADS, HEAD_DIM, PAGE_SIZE), f32),
            pltpu.VMEM((2, n_pages, N_KV_HEADS, HEAD_DIM, PAGE_SIZE), f32),
            pltpu.VMEM((nrow, past + LANES), f32),
            pltpu.SemaphoreType.DMA((2, 2)),
        ],
    )
    return pl.pallas_call(
        functools.partial(_moba_sample_kernel, n_pages=n_pages),
        grid_spec=grid_spec,
        out_shape=jax.ShapeDtypeStruct((nseq * SAMPLE_ROWS, qw), f32),
        compiler_params=_cparams(("arbitrary",)),
        name="moba_sample",
    )(page_table, proj, kn_new, proj, q_gain_row, slope_rows, ck, cv)


def _ssd_sample_kernel(ext_ref, z_ref, dt_ref, h0_ref, cw_ref, cb_ref, dtb_ref, alog_ref, dexp_ref, norm_ref,
                       y_ref, h_ref, scr_ref, *, nseq_step, n_new):
    rows = nseq_step * SAMPLE_ROWS
    scr_ref[pl.ds(0, rows), :] = ext_ref[...]
    scr_ref[pl.ds(rows, SUBLANES), :] = jnp.zeros((SUBLANES, CONV_DIM), f32)
    act = _ssd_conv_act(scr_ref, rows, cw_ref, cb_ref, base=0)

    ri = lax.broadcasted_iota(jnp.int32, (rows, rows), 0)
    ci = lax.broadcasted_iota(jnp.int32, (rows, rows), 1)
    allowed = jnp.logical_and(ri // SAMPLE_ROWS == ci // SAMPLE_ROWS,
                              jnp.logical_and(ci % SAMPLE_ROWS <= ri % SAMPLE_ROWS, ci % SAMPLE_ROWS < n_new))
    xs, bm, cm, dt_e, cs_e, xdt, y = _ssd_chunk(act, dt_ref[...], allowed, dtb_ref, alog_ref)

    valid = lax.broadcasted_iota(jnp.int32, (SAMPLE_ROWS, 1), 0) < n_new
    ecs = jnp.exp2(cs_e)
    hw = D_INNER // SSM_GROUPS
    hpg = SSM_HEADS // SSM_GROUPS
    yoff_rows = []
    for i in range(nseq_step):
        lo = i * SAMPLE_ROWS
        cs_i = cs_e[lo:lo + SAMPLE_ROWS, :]
        cs_last = cs_e[lo + n_new - 1:lo + n_new, :]
        xdte = jnp.where(valid, xdt[lo:lo + SAMPLE_ROWS, :] * jnp.exp2(cs_last - cs_i), 0.0)
        yg = []
        upd = []
        for g in range(SSM_GROUPS):
            h0g = h0_ref[i, g * hpg:(g + 1) * hpg].reshape(hw, D_STATE)
            yg.append(_dot_nt(cm[lo:lo + SAMPLE_ROWS, g * D_STATE:(g + 1) * D_STATE], h0g))
            upd.append(_dot_tn(xdte[:, g * hw:(g + 1) * hw], bm[lo:lo + SAMPLE_ROWS, g * D_STATE:(g + 1) * D_STATE]))
        yoff_rows.append(jnp.concatenate(yg, axis=1))
        dec_row = jnp.exp2(cs_last)
        for h in range(SSM_HEADS):
            dec_h = jnp.broadcast_to(dec_row[:, h * SSM_HEAD_DIM:h * SSM_HEAD_DIM + 1], (SSM_HEAD_DIM, D_STATE))
            r = h % hpg
            h_ref[i, h] = h0_ref[i, h] * dec_h + upd[h // hpg][r * SSM_HEAD_DIM:(r + 1) * SSM_HEAD_DIM, :]
    y = y + jnp.concatenate(yoff_rows, axis=0) * ecs
    y_ref[...] = _ssd_finish(y, xs, z_ref[...], dexp_ref, norm_ref).astype(y_ref.dtype)


def ssd_sample(ext, proj, h0, conv_w, conv_b_row, dtb_row, alog_row, dexp_row, norm_row, nseq, nseq_step, n_new):
    rows = nseq_step * SAMPLE_ROWS
    const = lambda i: (0, 0)
    state_spec = pl.BlockSpec((nseq_step, SSM_HEADS, SSM_HEAD_DIM, D_STATE), lambda i: (i, 0, 0, 0))
    return pl.pallas_call(
        functools.partial(_ssd_sample_kernel, nseq_step=nseq_step, n_new=n_new),
        grid=(nseq // nseq_step,),
        in_specs=[
            pl.BlockSpec((rows, CONV_DIM), lambda i: (i, 0)),
            pl.BlockSpec((rows, D_INNER), lambda i: (i, OFF_Z // D_INNER)),
            pl.BlockSpec((rows, LANES), lambda i: (i, OFF_DT // LANES)),
            state_spec,
            pl.BlockSpec((CONV_WIDTH, CONV_DIM), const),
            pl.BlockSpec((1, CONV_DIM), const),
            pl.BlockSpec((1, LANES), const),
            pl.BlockSpec((1, LANES), const),
            pl.BlockSpec((1, D_INNER), const),
            pl.BlockSpec((1, D_INNER), const),
        ],
        out_specs=[pl.BlockSpec((rows, D_INNER), lambda i: (i, 0)), state_spec],
        out_shape=[
            jax.ShapeDtypeStruct((nseq * SAMPLE_ROWS, D_INNER), f32),
            jax.ShapeDtypeStruct(h0.shape, f32),
        ],
        scratch_shapes=[pltpu.VMEM((rows + SUBLANES, CONV_DIM), f32)],
        compiler_params=_cparams(("arbitrary",)),
        name="ssd_sample",
    )(ext, proj, proj, h0, conv_w, conv_b_row, dtb_row, alog_row, dexp_row, norm_row)


IN_SIZES = (N_HEADS * HEAD_DIM, N_KV_HEADS * HEAD_DIM, N_KV_HEADS * HEAD_DIM, D_INNER, CONV_DIM, SSM_HEADS,
            MEM_HEADS * MEM_HEAD_DIM, 3 * D_MODEL)
TM = 512
TM_PROJ = 1024
TN_PROJ = 1920
TQ = 256
SSD_ROWS = 256
SSD_SAMPLE_SEQS = 8
MEM_SEQS_PER_STEP = 4


def _pad_lanes(v):
    return jnp.pad(v, (0, LANES - v.shape[0])).reshape(1, LANES)


def kernel(x_prompt, x_sample, mem_prompt, cache_k, cache_v, page_table, state_conv, state_ssm, cache_mem_k, cache_mem_v, norm_mix, w_in, q_norm, k_norm, conv_w, conv_b, dt_bias, a_log, d_skip, ssm_norm, mem_norm, w_mem_kv, mq_norm, mk_norm, w_attn_br, w_ssm_br, w_mem_br, w_out, norm_ffn, w_gate, w_up, w_down):
    assert w_in.shape[0] == 1, "single layer"
    assert x_sample.shape[1] <= SAMPLE_ROWS // 2, "sample MoBA packs two heads' tokens into one row tile"
    batch, seq, d = x_prompt.shape
    nseq, n_new, _ = x_sample.shape
    kvw = N_KV_HEADS * HEAD_DIM
    qw = N_HEADS * HEAD_DIM

    offs = np.cumsum(IN_SIZES)[:-1].tolist()
    wq, wk, wv, wz, wxbc, wdt, wmq, wgates = jnp.split(w_in[0], offs, axis=1)
    wdt = jnp.pad(wdt, ((0, 0), (0, LANES - SSM_HEADS)))
    tail = [wz, wxbc, wmq, wk, wv, wdt]
    w_prompt = jnp.concatenate([wq] + tail, axis=1).astype(bf16)
    wq_rgd = wq.reshape(d, N_KV_HEADS, Q_PER_KV, HEAD_DIM).transpose(0, 2, 1, 3).reshape(d, qw)
    w_sample = jnp.concatenate([wq_rgd] + tail, axis=1).astype(bf16)
    w_gates = wgates.astype(bf16)
    wa = w_attn_br[0].astype(bf16)
    wa_rgd = w_attn_br[0].reshape(N_KV_HEADS, Q_PER_KV, HEAD_DIM, d).transpose(1, 0, 2, 3).reshape(qw, d).astype(bf16)
    ws, wm, wo = w_ssm_br[0].astype(bf16), w_mem_br[0].astype(bf16), w_out[0].astype(bf16)
    wg, wu, wd = w_gate[0].astype(bf16), w_up[0].astype(bf16), w_down[0].astype(bf16)

    k_gain = jnp.tile(k_norm[0], N_KV_HEADS).reshape(1, kvw)
    conv_b_row = conv_b[0].reshape(1, CONV_DIM)
    dtb_row, alog_row = _pad_lanes(dt_bias[0]), _pad_lanes(a_log[0])
    dexp_row = jnp.repeat(d_skip[0], SSM_HEAD_DIM).reshape(1, D_INNER)
    ssm_norm_row = ssm_norm[0].reshape(1, D_INNER)
    mq_gain = mq_norm[0].reshape(1, MEM_HEAD_DIM)
    ffn_gain = norm_ffn[0].reshape(1, d)
    slopes2 = jnp.exp2(-8.0 * jnp.arange(1, N_HEADS + 1, dtype=f32) / N_HEADS) * LOG2E

    xp = x_prompt.reshape(batch * seq, d)
    proj = inproj(xp, norm_mix[0], w_prompt, TM_PROJ, TN_PROJ)
    knt, vt32, kb, vt, kmean = kprep(proj, k_gain, batch, seq, 2048)
    kmean_g = kmean.reshape(batch, seq // MOBA_BLOCK, N_KV_HEADS, HEAD_DIM).transpose(0, 2, 1, 3)
    slope_cols = jnp.repeat(slopes2.reshape(N_KV_HEADS, Q_PER_KV), TQ, axis=1).reshape(N_KV_HEADS, 1, Q_PER_KV * TQ)
    q_gain_col = jnp.broadcast_to(q_norm[0][:, None], (HEAD_DIM, TQ))
    attn_o = moba_prompt(proj, kb, vt, kmean_g, q_gain_col, slope_cols, batch, seq, TQ)
    ssm_o, h_prompt = ssd_prompt(proj, conv_w[0], conv_b_row, dtb_row, alog_row, dexp_row, ssm_norm_row,
                                 batch, seq, SSD_ROWS)
    mem_kv = inproj(mem_prompt.reshape(batch * MEM_TOKENS, d), mem_norm[0], w_mem_kv[0].astype(bf16), TM, qw)
    mk = headnorm(mem_kv, 0, qw, jnp.tile(mk_norm[0], MEM_HEADS).reshape(1, qw), MEM_HEAD_DIM, TM)
    mv = mem_kv[:, qw:]
    mem_o = mem_attend(proj, mk.reshape(batch, MEM_TOKENS, qw), mv.reshape(batch, MEM_TOKENS, qw), mq_gain,
                       batch, seq, TM)
    gates = inproj(xp, norm_mix[0], w_gates, TM_PROJ, 3 * d, bf16)
    x1 = merge(xp, attn_o, ssm_o, mem_o, gates, wa, ws, wm, wo, TM)
    y_prompt = ffn(x1, ffn_gain, wg, wu, wd, TM).reshape(batch, seq, d)

    k_prompt = jnp.transpose(knt.reshape(batch, N_KV_HEADS, HEAD_DIM, seq), (0, 3, 1, 2))[None]
    v_prompt = jnp.transpose(vt32.reshape(batch, N_KV_HEADS, HEAD_DIM, seq), (0, 3, 1, 2))[None]
    conv_prompt = proj.reshape(batch, seq, N_PROJ)[:, seq - (CONV_WIDTH - 1):, OFF_XBC:OFF_XBC + CONV_DIM][None]
    mem_k_prompt = mk.reshape(1, batch, MEM_TOKENS, MEM_HEADS, MEM_HEAD_DIM)
    mem_v_prompt = mv.reshape(1, batch, MEM_TOKENS, MEM_HEADS, MEM_HEAD_DIM)

    xs = jnp.pad(x_sample, ((0, 0), (0, SAMPLE_ROWS - n_new), (0, 0))).reshape(nseq * SAMPLE_ROWS, d)
    proj_s = inproj(xs, norm_mix[0], w_sample, TM_PROJ, TN_PROJ)
    kn_s = headnorm(proj_s, OFF_K // kvw, kvw, k_gain, HEAD_DIM, TM)
    slope_rows = jnp.broadcast_to(
        jnp.repeat(slopes2.reshape(N_KV_HEADS, Q_PER_KV).T.reshape(-1), SAMPLE_ROWS // 2)[:, None],
        (N_HEADS * (SAMPLE_ROWS // 2), LANES))
    ck_t = jnp.transpose(cache_k[0], (0, 2, 3, 1))
    cv_t = jnp.transpose(cache_v[0], (0, 2, 3, 1))
    attn_s = moba_sample(page_table, proj_s, kn_s, ck_t, cv_t, jnp.tile(q_norm[0], N_KV_HEADS).reshape(1, kvw),
                         slope_rows, nseq)
    xbc_new = proj_s[:, OFF_XBC:OFF_XBC + CONV_DIM].reshape(nseq, SAMPLE_ROWS, CONV_DIM)[:, :n_new]
    ext = jnp.concatenate(
        [state_conv[0], xbc_new, jnp.zeros((nseq, SAMPLE_ROWS - (CONV_WIDTH - 1) - n_new, CONV_DIM), f32)], axis=1)
    ssm_s, h_sample = ssd_sample(ext.reshape(nseq * SAMPLE_ROWS, CONV_DIM), proj_s, state_ssm[0], conv_w[0], conv_b_row,
                                 dtb_row, alog_row, dexp_row, ssm_norm_row, nseq, SSD_SAMPLE_SEQS, n_new)
    mem_s = mem_attend(proj_s, cache_mem_k[0], cache_mem_v[0], mq_gain, nseq, SAMPLE_ROWS, SAMPLE_ROWS)
    gates_s = inproj(xs, norm_mix[0], w_gates, TM_PROJ, 3 * d, bf16)
    x1s = merge(xs, attn_s, ssm_s, mem_s, gates_s, wa_rgd, ws, wm, wo, TM)
    y_sample = ffn(x1s, ffn_gain, wg, wu, wd, TM).reshape(nseq, SAMPLE_ROWS, d)[:, :n_new]

    k_sample = kn_s.reshape(nseq, SAMPLE_ROWS, N_KV_HEADS, HEAD_DIM)[:, :n_new][None]
    v_sample = proj_s[:, OFF_V:OFF_V + kvw].reshape(nseq, SAMPLE_ROWS, N_KV_HEADS, HEAD_DIM)[:, :n_new][None]
    conv_sample = ext[:, n_new:n_new + CONV_WIDTH - 1][None]

    return (y_prompt, y_sample, k_prompt, v_prompt, conv_prompt, h_prompt[None], mem_k_prompt, mem_v_prompt,
            k_sample, v_sample, conv_sample, h_sample[None])
```

```python
import functools
import math

import jax
import jax.numpy as jnp
import numpy as np
from jax import lax
from jax.experimental import pallas as pl
from jax.experimental.pallas import tpu as pltpu

f32 = jnp.float32
bf16 = jnp.bfloat16

D_MODEL = 1024
N_HEADS = 16
N_KV_HEADS = 4
HEAD_DIM = 64
Q_PER_KV = N_HEADS // N_KV_HEADS
MOBA_BLOCK = 256
MOBA_TOPK = 3
SSM_HEADS = 16
SSM_HEAD_DIM = 64
D_INNER = SSM_HEADS * SSM_HEAD_DIM
SSM_GROUPS = 4
D_STATE = 128
CONV_WIDTH = 4
CONV_DIM = D_INNER + 2 * SSM_GROUPS * D_STATE
MEM_TOKENS = 256
MEM_HEADS = 4
MEM_HEAD_DIM = 256
D_FF = 2816
EPS = 1e-6
PAGE_SIZE = 128

LANES = 128
SUBLANES = 8
LOG2E = 1.4426950408889634
NEG = -1e30
VMEM_LIMIT = 56 * 1024 * 1024

OFF_Q, OFF_Z, OFF_XBC, OFF_MQ, OFF_K, OFF_V, OFF_DT = 0, 1024, 2048, 4096, 5120, 5376, 5632
N_PROJ = 5760
SAMPLE_ROWS = 8


def _cparams(sem):
    return pltpu.CompilerParams(dimension_semantics=sem, vmem_limit_bytes=VMEM_LIMIT)


def _split3(x):
    h = x.astype(bf16)
    r = x - h.astype(f32)
    m = r.astype(bf16)
    l = (r - m.astype(f32)).astype(bf16)
    return h, m, l


def _dot(a, b):
    return jnp.dot(a, b, preferred_element_type=f32)


def _dot_nt(a, b):
    return lax.dot_general(a, b, (((1,), (1,)), ((), ())), preferred_element_type=f32)


def _dot_tn(a, b):
    return lax.dot_general(a, b, (((0,), (0,)), ((), ())), preferred_element_type=f32)


def _dot_exact_rhs(x, m_bf16):
    h, m, l = _split3(x)
    return _dot(h, m_bf16) + _dot(m, m_bf16) + _dot(l, m_bf16)


def _dot_exact_lhs(m_bf16, x):
    h, m, l = _split3(x)
    return _dot(m_bf16, h) + _dot(m_bf16, m) + _dot(m_bf16, l)


def _sigmoid(x):
    return 0.5 * jnp.tanh(0.5 * x) + 0.5


def _silu(x):
    return x * _sigmoid(x)


def _softplus(x):
    return jnp.maximum(x, 0.0) + jnp.log(1.0 + jnp.exp(-jnp.abs(x)))


def _inproj_kernel(x_ref, g_ref, w_ref, o_ref):
    x = x_ref[...]
    ms = jnp.mean(x * x, axis=-1, keepdims=True)
    xn = (x * lax.rsqrt(ms + EPS) * g_ref[...]).astype(bf16)
    o_ref[...] = _dot(xn, w_ref[...]).astype(o_ref.dtype)


def inproj(x, gain, w, tm, tn, out_dtype=f32):
    t, d = x.shape
    n = w.shape[1]
    return pl.pallas_call(
        _inproj_kernel,
        grid=(n // tn, t // tm),
        in_specs=[
            pl.BlockSpec((tm, d), lambda j, i: (i, 0)),
            pl.BlockSpec((1, d), lambda j, i: (0, 0)),
            pl.BlockSpec((d, tn), lambda j, i: (0, j)),
        ],
        out_specs=pl.BlockSpec((tm, tn), lambda j, i: (i, j)),
        out_shape=jax.ShapeDtypeStruct((t, n), out_dtype),
        compiler_params=_cparams(("arbitrary", "arbitrary")),
        name="inproj",
    )(x, gain.reshape(1, d), w)


def _seg_ones(width, seg):
    r = lax.broadcasted_iota(jnp.int32, (width, width), 0) // seg
    c = lax.broadcasted_iota(jnp.int32, (width, width), 1) // seg
    return jnp.where(r == c, 1.0, 0.0).astype(bf16)


def _seg_rmsnorm(x, gain_row, seg):
    ss = _dot_exact_rhs(x * x, _seg_ones(x.shape[1], seg))
    return x * lax.rsqrt(ss * (1.0 / seg) + EPS) * gain_row


def _headnorm_kernel(x_ref, g_ref, o_ref, *, seg):
    o_ref[...] = _seg_rmsnorm(x_ref[...], g_ref[...], seg)


def headnorm(x, col_block, width, gain_row, seg, tm):
    t = x.shape[0]
    return pl.pallas_call(
        functools.partial(_headnorm_kernel, seg=seg),
        grid=(t // tm,),
        in_specs=[
            pl.BlockSpec((tm, width), lambda i: (i, col_block)),
            pl.BlockSpec((1, width), lambda i: (0, 0)),
        ],
        out_specs=pl.BlockSpec((tm, width), lambda i: (i, 0)),
        out_shape=jax.ShapeDtypeStruct((t, width), f32),
        compiler_params=_cparams(("arbitrary",)),
        name="headnorm",
    )(x, gain_row)


def _kprep_kernel(k_ref, v_ref, g_ref, knt_ref, vt32_ref, ka_ref, vt_ref, km_ref):
    kn = _seg_rmsnorm(k_ref[...], g_ref[...], HEAD_DIM)
    knt_ref[0] = kn.T
    kn_b = kn.astype(bf16)
    rows = kn.shape[0]
    kvw = kn.shape[1]
    r = lax.broadcasted_iota(jnp.int32, (kvw, LANES), 0)
    c = lax.broadcasted_iota(jnp.int32, (kvw, LANES), 1)
    lane = lax.broadcasted_iota(jnp.int32, (rows, LANES), 1)
    pos = (lax.broadcasted_iota(jnp.int32, (rows, LANES), 0) % MOBA_BLOCK).astype(f32)
    pos_lanes = jnp.where(jnp.logical_and(lane >= HEAD_DIM, lane < HEAD_DIM + 3), pos, 0.0)
    for g in range(N_KV_HEADS):
        pick = jnp.where(jnp.logical_and(r == c + g * HEAD_DIM, c < HEAD_DIM), 1.0, 0.0).astype(bf16)
        ka_ref[0, g] = (_dot(kn_b, pick) + pos_lanes).astype(bf16)
    nb = kn.shape[0] // MOBA_BLOCK
    km_ref[0] = jnp.sum(kn.reshape(nb, MOBA_BLOCK, kn.shape[1]), axis=1) * (1.0 / MOBA_BLOCK)
    vt = v_ref[...].T
    vt32_ref[0] = vt
    vt_ref[0] = vt.astype(bf16)


def kprep(proj, k_gain_row, batch, seq, tm):
    kvw = N_KV_HEADS * HEAD_DIM
    nt = seq // tm
    nb = tm // MOBA_BLOCK
    return pl.pallas_call(
        _kprep_kernel,
        grid=(batch, nt),
        in_specs=[
            pl.BlockSpec((tm, kvw), lambda b, i: (b * nt + i, OFF_K // kvw)),
            pl.BlockSpec((tm, kvw), lambda b, i: (b * nt + i, OFF_V // kvw)),
            pl.BlockSpec((1, kvw), lambda b, i: (0, 0)),
        ],
        out_specs=[
            pl.BlockSpec((1, kvw, tm), lambda b, i: (b, 0, i)),
            pl.BlockSpec((1, kvw, tm), lambda b, i: (b, 0, i)),
            pl.BlockSpec((1, N_KV_HEADS, tm, LANES), lambda b, i: (b, 0, i, 0)),
            pl.BlockSpec((1, kvw, tm), lambda b, i: (b, 0, i)),
            pl.BlockSpec((1, nb, kvw), lambda b, i: (b, i, 0)),
        ],
        out_shape=[
            jax.ShapeDtypeStruct((batch, kvw, seq), f32),
            jax.ShapeDtypeStruct((batch, kvw, seq), f32),
            jax.ShapeDtypeStruct((batch, N_KV_HEADS, seq, LANES), bf16),
            jax.ShapeDtypeStruct((batch, kvw, seq), bf16),
            jax.ShapeDtypeStruct((batch, seq // MOBA_BLOCK, kvw), f32),
        ],
        compiler_params=_cparams(("arbitrary", "arbitrary")),
        name="kprep",
    )(proj, proj, k_gain_row)


def _top3_penalty(gate, jb, axis):
    n = gate.shape[axis]
    pen = jnp.full(gate.shape, NEG, f32)
    for _ in range(MOBA_TOPK):
        m = jnp.max(gate, axis=axis, keepdims=True)
        idx = jnp.min(jnp.where(gate == m, jb, n), axis=axis, keepdims=True)
        idx = jnp.where(m > -jnp.inf, idx, n)
        hit = jb == idx
        pen = jnp.where(hit, 0.0, pen)
        gate = jnp.where(hit, -jnp.inf, gate)
    return pen


def _moba_prompt_kernel(q_ref, kb_ref, vt_ref, km_ref, qg_ref, sl_ref, o_ref,
                        qbd_ref, pen_ref, acc_ref, m_ref, s_ref, cm_ref, *, tq):
    qt = pl.program_id(2)
    nb_tile = tq // MOBA_BLOCK
    own = qt * nb_tile
    cols = Q_PER_KV * tq
    nblk = km_ref.shape[2]
    q_in_tile = lax.broadcasted_iota(jnp.int32, (1, cols), 1) % tq
    own_col = own + q_in_tile // MOBA_BLOCK

    qT = q_ref[...].T
    parts = []
    for r in range(Q_PER_KV):
        xr = qT[r * HEAD_DIM:(r + 1) * HEAD_DIM, :]
        ms = jnp.mean(xr * xr, axis=0, keepdims=True)
        parts.append(xr * lax.rsqrt(ms + EPS) * qg_ref[...])
    qn = jnp.concatenate(parts, axis=1)

    km = km_ref[0, 0]
    kh, kmid, kl = _split3(km)
    qh, qmid, ql = _split3(qn)
    gate = _dot(kh, qh) + _dot(kh, qmid) + _dot(kmid, qh) + _dot(kh, ql) + _dot(kl, qh) + _dot(kmid, qmid)
    jb = lax.broadcasted_iota(jnp.int32, (nblk, cols), 0)
    gate = jnp.where(jb < own_col, gate, -jnp.inf)
    pen_ref[...] = jnp.where(jb == own_col, 0.0, _top3_penalty(gate, jb, 0))

    slope = sl_ref[0]
    s_hi, s_mid, s_lo = _split3(slope)
    rowi = lax.broadcasted_iota(jnp.int32, (HEAD_DIM, cols), 0)
    slope_rows = jnp.where(rowi == 0, s_hi.astype(f32), jnp.where(rowi == 1, s_mid.astype(f32),
                           jnp.where(rowi == 2, s_lo.astype(f32), 0.0)))
    qbd_ref[...] = jnp.concatenate([qn * (HEAD_DIM ** -0.5 * LOG2E), slope_rows], axis=0).astype(bf16)

    kk = lax.broadcasted_iota(jnp.int32, (MOBA_BLOCK, cols), 0)
    ones = jnp.ones((2 * SUBLANES, MOBA_BLOCK), bf16)

    def blk(j):
        return pl.ds(pl.multiple_of(j * MOBA_BLOCK, MOBA_BLOCK), MOBA_BLOCK)

    def v_aug(j):
        return jnp.concatenate([vt_ref[0, :, blk(j)], ones], axis=0)

    def shift_of(j):
        return slope * ((j - own) * MOBA_BLOCK).astype(f32) + pen_ref[pl.ds(j, 1), :]

    def stage_scores(j, buf):
        s = _dot(kb_ref[0, 0, blk(j), :], qbd_ref[...])
        s_ref[buf] = s
        cm_ref[buf] = jnp.max(s, axis=0, keepdims=True)

    def stage_accum(j, buf, valid=None):
        shift = shift_of(j)
        if valid is not None:
            shift = shift + jnp.where(valid, 0.0, NEG)
        m_old = m_ref[...]
        m_new = jnp.maximum(m_old, cm_ref[buf] + shift)
        p = jnp.exp2(s_ref[buf] - (m_new - shift)).astype(bf16)
        acc_ref[...] = acc_ref[...] * jnp.exp2(m_old - m_new) + _dot(v_aug(j), p)
        m_ref[...] = m_new

    last = jnp.maximum(own - 1, 0)
    stage_scores(0, 1)
    col_blk = q_in_tile // MOBA_BLOCK
    causal = kk <= q_in_tile % MOBA_BLOCK
    ss, shifts = [], []
    for t in range(nb_tile):
        s = _dot(kb_ref[0, 0, blk(own + t), :], qbd_ref[...])
        visible = jnp.logical_or(col_blk > t, jnp.logical_and(col_blk == t, causal))
        ss.append(jnp.where(visible, s, NEG))
        shifts.append(shift_of(own + t) if nb_tile > 1 else jnp.zeros((1, cols), f32))
    m0 = None
    for s, sh in zip(ss, shifts):
        mj = jnp.max(s, axis=0, keepdims=True) + sh
        m0 = mj if m0 is None else jnp.maximum(m0, mj)
    pv0 = None
    for t, (s, sh) in enumerate(zip(ss, shifts)):
        pv = _dot(v_aug(own + t), jnp.exp2(s - (m0 - sh)).astype(bf16))
        pv0 = pv if pv0 is None else pv0 + pv
    acc_ref[...] = pv0
    m_ref[...] = m0


    def pair(j0):
        j1, j2 = j0 + 1, j0 + 2
        stage_scores(jnp.minimum(j1, last), 0)
        stage_accum(j0, 1)
        stage_scores(jnp.minimum(j2, last), 1)
        stage_accum(jnp.minimum(j1, last), 0, valid=j1 < own)

    def body8(i, c):
        for k in range(4):
            pair(8 * i + 2 * k)
        return c

    def body4(i, c):
        pair(8 * n8)
        pair(8 * n8 + 2)
        return c

    def body2(i, c):
        pair(8 * n8 + 4 * n4 + 2 * i)
        return c

    n8 = own // 8
    n4 = (own - 8 * n8) // 4
    lax.fori_loop(0, n8, body8, 0)
    lax.fori_loop(0, n4, body4, 0)
    lax.fori_loop(0, (own - 8 * n8 - 4 * n4 + 1) // 2, body2, 0)

    acc = acc_ref[...]
    o = acc[0:HEAD_DIM, :] / acc[HEAD_DIM:HEAD_DIM + 1, :]
    oT = jnp.concatenate([o[:, r * tq:(r + 1) * tq] for r in range(Q_PER_KV)], axis=0)
    o_ref[...] = oT.T.astype(o_ref.dtype)


def moba_prompt(proj, kb, vt, kmean_g, q_gain_col, slopes2, batch, seq, tq):
    nq = seq // tq
    qw = Q_PER_KV * HEAD_DIM
    kvw = N_KV_HEADS * HEAD_DIM
    nblk = seq // MOBA_BLOCK
    cols = Q_PER_KV * tq
    return pl.pallas_call(
        functools.partial(_moba_prompt_kernel, tq=tq),
        grid=(batch, N_KV_HEADS, nq),
        in_specs=[
            pl.BlockSpec((tq, qw), lambda b, g, i: (b * nq + i, OFF_Q // qw + g)),
            pl.BlockSpec((1, 1, seq, LANES), lambda b, g, i: (b, g, 0, 0)),
            pl.BlockSpec((1, HEAD_DIM, seq), lambda b, g, i: (b, g, 0)),
            pl.BlockSpec((1, 1, nblk, HEAD_DIM), lambda b, g, i: (b, g, 0, 0)),
            pl.BlockSpec((HEAD_DIM, tq), lambda b, g, i: (0, 0)),
            pl.BlockSpec((1, 1, cols), lambda b, g, i: (g, 0, 0)),
        ],
        out_specs=pl.BlockSpec((tq, qw), lambda b, g, i: (b * nq + i, g)),
        out_shape=jax.ShapeDtypeStruct((batch * seq, N_HEADS * HEAD_DIM), bf16),
        scratch_shapes=[
            pltpu.VMEM((LANES, cols), bf16),
            pltpu.VMEM((nblk, cols), f32),
            pltpu.VMEM((HEAD_DIM + 2 * SUBLANES, cols), f32),
            pltpu.VMEM((1, cols), f32),
            pltpu.VMEM((2, MOBA_BLOCK, cols), f32),
            pltpu.VMEM((2, 1, cols), f32),
        ],
        compiler_params=_cparams(("arbitrary", "arbitrary", "arbitrary")),
        name="moba_prompt",
    )(proj, kb, vt, kmean_g, q_gain_col, slopes2)


def _head_expand():
    r = lax.broadcasted_iota(jnp.int32, (LANES, D_INNER), 0)
    c = lax.broadcasted_iota(jnp.int32, (LANES, D_INNER), 1) // SSM_HEAD_DIM
    return jnp.where(r == c, 1.0, 0.0).astype(bf16)


def _ssd_conv_act(ext_ref, rows, cw_ref, cb_ref, base):
    conv = cb_ref[...] + cw_ref[0:1, :] * ext_ref[pl.ds(base, rows), :]
    for i in range(1, CONV_WIDTH):
        conv = conv + cw_ref[i:i + 1, :] * ext_ref[pl.ds(base + i, rows), :]
    return _silu(conv)


def _ssd_chunk(act, dt_raw, allowed, dtb_ref, alog_ref):
    rows = act.shape[0]
    xs = act[:, :D_INNER]
    gn = SSM_GROUPS * D_STATE
    bm = act[:, D_INNER:D_INNER + gn]
    cm = act[:, D_INNER + gn:]
    dt = _softplus(dt_raw + dtb_ref[...])
    a = -jnp.exp(alog_ref[...])
    mask_b = jnp.where(allowed, 1.0, 0.0).astype(bf16)
    cs = _dot_exact_lhs(mask_b, dt * (a * LOG2E))
    expand = _head_expand()
    dt_e = _dot_exact_rhs(dt, expand)
    cs_e = _dot_exact_rhs(cs, expand)
    xdt = xs * dt_e
    csT = cs.T
    xdt_b = xdt.astype(bf16)
    ypairs = []
    lane = lax.broadcasted_iota(jnp.int32, (rows, LANES), 1)
    for g in range(SSM_GROUPS):
        cb = _dot_nt(cm[:, g * D_STATE:(g + 1) * D_STATE].astype(bf16), bm[:, g * D_STATE:(g + 1) * D_STATE].astype(bf16))
        hpg = SSM_HEADS // SSM_GROUPS
        for pair in range(hpg // 2):
            res = []
            for k in range(2):
                h = g * hpg + pair * 2 + k
                seg = cs[:, h:h + 1] - csT[h:h + 1, :]
                m = (cb * jnp.exp2(jnp.where(allowed, seg, NEG))).astype(bf16)
                lo = (g * hpg + pair * 2) * SSM_HEAD_DIM
                res.append(_dot(m, xdt_b[:, lo:lo + LANES]))
            ypairs.append(jnp.where(lane < SSM_HEAD_DIM, res[0], res[1]))
    y_diag = jnp.concatenate(ypairs, axis=1)
    return xs, bm, cm, dt_e, cs_e, xdt, y_diag


def _ssd_finish(y, xs, z, dexp_ref, norm_ref):
    y = (y + dexp_ref[...] * xs) * _silu(z)
    gw = D_INNER // SSM_GROUPS
    outs = []
    for g in range(SSM_GROUPS):
        yg = y[:, g * gw:(g + 1) * gw]
        ms = jnp.mean(yg * yg, axis=-1, keepdims=True)
        outs.append(yg * lax.rsqrt(ms + EPS) * norm_ref[:, g * gw:(g + 1) * gw])
    return jnp.concatenate(outs, axis=1)


def _ssd_prompt_kernel(xbc_ref, z_ref, dt_ref, cw_ref, cb_ref, dtb_ref, alog_ref, dexp_ref, norm_ref,
                       y_ref, h_ref, halo_ref, ht_ref, *, rows):
    c = pl.program_id(1)

    @pl.when(c == 0)
    def _():
        halo_ref[...] = jnp.zeros((SUBLANES, CONV_DIM), f32)
        ht_ref[...] = jnp.zeros(ht_ref.shape, f32)

    xt = xbc_ref[...]
    halo = halo_ref[...]
    row8 = lax.broadcasted_iota(jnp.int32, (SUBLANES, CONV_DIM), 0)
    conv = cb_ref[...] + cw_ref[CONV_WIDTH - 1:CONV_WIDTH, :] * xt
    for sft in range(1, CONV_WIDTH):
        rolled = pltpu.roll(xt, sft, axis=0)
        head = jnp.where(row8 < sft, pltpu.roll(halo, sft, axis=0), rolled[0:SUBLANES, :])
        tap = CONV_WIDTH - 1 - sft
        conv = conv + cw_ref[tap:tap + 1, :] * jnp.concatenate([head, rolled[SUBLANES:, :]], axis=0)
    act = _silu(conv)
    halo_ref[...] = xt[rows - SUBLANES:rows, :]

    ri = lax.broadcasted_iota(jnp.int32, (rows, rows), 0)
    ci = lax.broadcasted_iota(jnp.int32, (rows, rows), 1)
    xs, bm, cm, dt_e, cs_e, xdt, y = _ssd_chunk(act, dt_ref[...], ci <= ri, dtb_ref, alog_ref)

    cs_last = cs_e[rows - 1:rows, :]
    ecs = jnp.exp2(cs_e)
    xdte = (xdt * jnp.exp2(cs_last - cs_e)).astype(bf16)
    dec = jnp.exp2(cs_last)
    hw = D_INNER // SSM_GROUPS
    yoff = []
    for g in range(SSM_GROUPS):
        ht_g = ht_ref[:, g * hw:(g + 1) * hw]
        yoff.append(_dot(cm[:, g * D_STATE:(g + 1) * D_STATE].astype(bf16), ht_g.astype(bf16)))
        upd = _dot_tn(bm[:, g * D_STATE:(g + 1) * D_STATE].astype(bf16), xdte[:, g * hw:(g + 1) * hw])
        ht_ref[:, g * hw:(g + 1) * hw] = ht_g * dec[:, g * hw:(g + 1) * hw] + upd
    y = y + jnp.concatenate(yoff, axis=1) * ecs
    y_ref[...] = _ssd_finish(y, xs, z_ref[...], dexp_ref, norm_ref).astype(y_ref.dtype)

    @pl.when(c == pl.num_programs(1) - 1)
    def _():
        h_ref[0] = ht_ref[...].T.reshape(SSM_HEADS, SSM_HEAD_DIM, D_STATE)


def ssd_prompt(proj, conv_w, conv_b_row, dtb_row, alog_row, dexp_row, norm_row, batch, seq, rows):
    nc = seq // rows
    return pl.pallas_call(
        functools.partial(_ssd_prompt_kernel, rows=rows),
        grid=(batch, nc),
        in_specs=[
            pl.BlockSpec((rows, CONV_DIM), lambda b, c: (b * nc + c, OFF_XBC // CONV_DIM)),
            pl.BlockSpec((rows, D_INNER), lambda b, c: (b * nc + c, OFF_Z // D_INNER)),
            pl.BlockSpec((rows, LANES), lambda b, c: (b * nc + c, OFF_DT // LANES)),
            pl.BlockSpec((CONV_WIDTH, CONV_DIM), lambda b, c: (0, 0)),
            pl.BlockSpec((1, CONV_DIM), lambda b, c: (0, 0)),
            pl.BlockSpec((1, LANES), lambda b, c: (0, 0)),
            pl.BlockSpec((1, LANES), lambda b, c: (0, 0)),
            pl.BlockSpec((1, D_INNER), lambda b, c: (0, 0)),
            pl.BlockSpec((1, D_INNER), lambda b, c: (0, 0)),
        ],
        out_specs=[
            pl.BlockSpec((rows, D_INNER), lambda b, c: (b * nc + c, 0)),
            pl.BlockSpec((1, SSM_HEADS, SSM_HEAD_DIM, D_STATE), lambda b, c: (b, 0, 0, 0)),
        ],
        out_shape=[
            jax.ShapeDtypeStruct((batch * seq, D_INNER), bf16),
            jax.ShapeDtypeStruct((batch, SSM_HEADS, SSM_HEAD_DIM, D_STATE), f32),
        ],
        scratch_shapes=[
            pltpu.VMEM((SUBLANES, CONV_DIM), f32),
            pltpu.VMEM((D_STATE, D_INNER), f32),
        ],
        compiler_params=_cparams(("arbitrary", "arbitrary")),
        name="ssd_prompt",
    )(proj, proj, proj, conv_w, conv_b_row, dtb_row, alog_row, dexp_row, norm_row)


def _mem_q(q_ref, g_ref, h, rows=slice(None)):
    q = q_ref[rows, h * MEM_HEAD_DIM:(h + 1) * MEM_HEAD_DIM]
    ms = jnp.mean(q * q, axis=-1, keepdims=True)
    return (q * lax.rsqrt(ms + EPS) * g_ref[...] * (MEM_HEAD_DIM ** -0.5)).astype(bf16)


def _mem_attend_kernel(q_ref, mk_ref, mv_ref, g_ref, o_ref, *, head_axis):
    if head_axis:
        nseq_step = mk_ref.shape[0]
        tq = q_ref.shape[0] // nseq_step
        nrow = MEM_TOKENS * MEM_HEADS
        rh = lax.broadcasted_iota(jnp.int32, (MEM_HEADS * tq, nrow), 0) // tq
        ch = lax.broadcasted_iota(jnp.int32, (MEM_HEADS * tq, nrow), 1) % MEM_HEADS
        own_head = rh == ch
        for i in range(nseq_step):
            rows = slice(i * tq, (i + 1) * tq)
            q4 = jnp.concatenate([_mem_q(q_ref, g_ref, h, rows) for h in range(MEM_HEADS)], axis=0)
            s = _dot_nt(q4, mk_ref[i].reshape(nrow, MEM_HEAD_DIM).astype(bf16))
            s = jnp.where(own_head, s, NEG)
            m = jnp.max(s, axis=-1, keepdims=True)
            p = jnp.exp(s - m)
            l = jnp.sum(p, axis=-1, keepdims=True)
            o = _dot(p.astype(bf16), mv_ref[i].reshape(nrow, MEM_HEAD_DIM).astype(bf16)) / l
            o_ref[rows, :] = jnp.concatenate([o[h * tq:(h + 1) * tq, :] for h in range(MEM_HEADS)], axis=1)
        return
    outs = []
    for h in range(MEM_HEADS):
        lo = h * MEM_HEAD_DIM
        s = _dot_nt(_mem_q(q_ref, g_ref, h), mk_ref[0, :, lo:lo + MEM_HEAD_DIM].astype(bf16))
        m = jnp.max(s, axis=-1, keepdims=True)
        p = jnp.exp(s - m)
        l = jnp.sum(p, axis=-1, keepdims=True)
        outs.append(_dot(p.astype(bf16), mv_ref[0, :, lo:lo + MEM_HEAD_DIM].astype(bf16)) / l)
    o_ref[...] = jnp.concatenate(outs, axis=1).astype(o_ref.dtype)


def mem_attend(proj, mk, mv, mq_gain_row, nseq, rows_per_seq, tq):
    w = MEM_HEADS * MEM_HEAD_DIM
    head_axis = mk.ndim == 4
    if head_axis:
        assert tq == rows_per_seq and nseq % MEM_SEQS_PER_STEP == 0
        steps, tq, nq = nseq // MEM_SEQS_PER_STEP, MEM_SEQS_PER_STEP * rows_per_seq, 1
        kv_spec = pl.BlockSpec((MEM_SEQS_PER_STEP, MEM_TOKENS, MEM_HEADS, MEM_HEAD_DIM), lambda b, i: (b, 0, 0, 0))
    else:
        steps, nq = nseq, rows_per_seq // tq
        kv_spec = pl.BlockSpec((1, MEM_TOKENS, w), lambda b, i: (b, 0, 0))
    return pl.pallas_call(
        functools.partial(_mem_attend_kernel, head_axis=head_axis),
        grid=(steps, nq),
        in_specs=[
            pl.BlockSpec((tq, w), lambda b, i: (b * nq + i, OFF_MQ // w)),
            kv_spec,
            kv_spec,
            pl.BlockSpec((1, MEM_HEAD_DIM), lambda b, i: (0, 0)),
        ],
        out_specs=pl.BlockSpec((tq, w), lambda b, i: (b * nq + i, 0)),
        out_shape=jax.ShapeDtypeStruct((nseq * rows_per_seq, w), f32 if head_axis else bf16),
        compiler_params=_cparams(("arbitrary", "arbitrary")),
        name="mem_attend",
    )(proj, mk, mv, mq_gain_row)


def _merge_kernel(x_ref, a_ref, s_ref, m_ref, ga_ref, gs_ref, gm_ref, wa_ref, ws_ref, wm_ref, wo_ref, o_ref):
    mixed = _sigmoid(ga_ref[...].astype(f32)) * _dot(a_ref[...].astype(bf16), wa_ref[...])
    mixed = mixed + _sigmoid(gs_ref[...].astype(f32)) * _dot(s_ref[...].astype(bf16), ws_ref[...])
    mixed = mixed + _sigmoid(gm_ref[...].astype(f32)) * _dot(m_ref[...].astype(bf16), wm_ref[...])
    o_ref[...] = x_ref[...] + _dot(mixed.astype(bf16), wo_ref[...])


def merge(x, attn_o, ssm_o, mem_o, gates, wa, ws, wm, wo, tm):
    t, d = x.shape
    row = lambda i: (i, 0)
    const = lambda i: (0, 0)
    gate_spec = lambda k: pl.BlockSpec((tm, d), lambda i: (i, k))
    w_spec = pl.BlockSpec((d, d), const)
    return pl.pallas_call(
        _merge_kernel,
        grid=(t // tm,),
        in_specs=[pl.BlockSpec((tm, d), row)] * 4 + [gate_spec(0), gate_spec(1), gate_spec(2)] + [w_spec] * 4,
        out_specs=pl.BlockSpec((tm, d), row),
        out_shape=jax.ShapeDtypeStruct((t, d), f32),
        compiler_params=_cparams(("arbitrary",)),
        name="merge",
    )(x, attn_o, ssm_o, mem_o, gates, gates, gates, wa, ws, wm, wo)


def _ffn_kernel(x_ref, g_ref, wg_ref, wu_ref, wd_ref, o_ref):
    x = x_ref[...]
    ms = jnp.mean(x * x, axis=-1, keepdims=True)
    h = (x * lax.rsqrt(ms + EPS) * g_ref[...]).astype(bf16)
    act = _silu(_dot(h, wg_ref[...])) * _dot(h, wu_ref[...])
    o_ref[...] = x + _dot(act.astype(bf16), wd_ref[...])


def ffn(x, gain_row, wg, wu, wd, tm):
    t, d = x.shape
    dff = wg.shape[1]
    const = lambda i: (0, 0)
    single = pl.Buffered(1)
    return pl.pallas_call(
        _ffn_kernel,
        grid=(t // tm,),
        in_specs=[
            pl.BlockSpec((tm, d), lambda i: (i, 0)),
            pl.BlockSpec((1, d), const),
            pl.BlockSpec((d, dff), const, pipeline_mode=single),
            pl.BlockSpec((d, dff), const, pipeline_mode=single),
            pl.BlockSpec((dff, d), const, pipeline_mode=single),
        ],
        out_specs=pl.BlockSpec((tm, d), lambda i: (i, 0)),
        out_shape=jax.ShapeDtypeStruct((t, d), f32),
        compiler_params=_cparams(("arbitrary",)),
        name="ffn",
    )(x, gain_row, wg, wu, wd)


def _merge_ffn_kernel(x_ref, a_ref, s_ref, m_ref, ga_ref, gs_ref, gm_ref, wa_ref, ws_ref, wm_ref, wo_ref,
                      g_ref, wg_ref, wu_ref, wd_ref, o_ref):
    mixed = _sigmoid(ga_ref[...].astype(f32)) * _dot(a_ref[...].astype(bf16), wa_ref[...])
    mixed = mixed + _sigmoid(gs_ref[...].astype(f32)) * _dot(s_ref[...].astype(bf16), ws_ref[...])
    mixed = mixed + _sigmoid(gm_ref[...].astype(f32)) * _dot(m_ref[...].astype(bf16), wm_ref[...])
    x = x_ref[...] + _dot(mixed.astype(bf16), wo_ref[...])
    ms = jnp.mean(x * x, axis=-1, keepdims=True)
    h = (x * lax.rsqrt(ms + EPS) * g_ref[...]).astype(bf16)
    act = _silu(_dot(h, wg_ref[...])) * _dot(h, wu_ref[...])
    o_ref[...] = x + _dot(act.astype(bf16), wd_ref[...])


def merge_ffn(x, attn_o, ssm_o, mem_o, gates, wa, ws, wm, wo, gain_row, wg, wu, wd, tm):
    t, d = x.shape
    dff = wg.shape[1]
    row = lambda i: (i, 0)
    const = lambda i: (0, 0)
    single = pl.Buffered(1)
    gate_spec = lambda k: pl.BlockSpec((tm, d), lambda i: (i, k))
    w_spec = pl.BlockSpec((d, d), const, pipeline_mode=single)
    return pl.pallas_call(
        _merge_ffn_kernel,
        grid=(t // tm,),
        in_specs=[pl.BlockSpec((tm, d), row)] * 4 + [gate_spec(0), gate_spec(1), gate_spec(2)] + [w_spec] * 4 + [
            pl.BlockSpec((1, d), const),
            pl.BlockSpec((d, dff), const, pipeline_mode=single),
            pl.BlockSpec((d, dff), const, pipeline_mode=single),
            pl.BlockSpec((dff, d), const, pipeline_mode=single),
        ],
        out_specs=pl.BlockSpec((tm, d), row),
        out_shape=jax.ShapeDtypeStruct((t, d), f32),
        compiler_params=_cparams(("arbitrary",)),
        name="merge_ffn",
    )(x, attn_o, ssm_o, mem_o, gates, gates, gates, wa, ws, wm, wo, gain_row, wg, wu, wd)


def _moba_sample_kernel(pt_ref, q_ref, kn_ref, vn_ref, qg_ref, sl_ref, ck_hbm, cv_hbm, o_ref,
                        kbuf, vbuf, s_ref, sem, *, n_pages):
    b = pl.program_id(0)
    nseq = pl.num_programs(0)
    kvw = N_KV_HEADS * HEAD_DIM
    half = SAMPLE_ROWS // 2
    nrow = Q_PER_KV * N_KV_HEADS * half
    nblk = n_pages * PAGE_SIZE // MOBA_BLOCK
    ppb = MOBA_BLOCK // PAGE_SIZE
    past = n_pages * PAGE_SIZE
    slot = b % 2

    def k_copy(seq, sl, p):
        return pltpu.make_async_copy(ck_hbm.at[pt_ref[seq, p]], kbuf.at[sl, p], sem.at[0, sl])

    def v_copy(seq, sl, p):
        return pltpu.make_async_copy(cv_hbm.at[pt_ref[seq, p]], vbuf.at[sl, p], sem.at[1, sl])

    def start_all(seq, sl):
        for p in range(n_pages):
            k_copy(seq, sl, p).start()
        for p in range(n_pages):
            v_copy(seq, sl, p).start()

    @pl.when(b == 0)
    def _():
        start_all(0, 0)

    nxt = jnp.minimum(b + 1, nseq - 1)

    qb = q_ref[...]
    lane_g = lax.broadcasted_iota(jnp.int32, (SAMPLE_ROWS, kvw), 1) // HEAD_DIM
    low_half = lax.broadcasted_iota(jnp.int32, (SAMPLE_ROWS, kvw), 0) < half
    pieces = []
    for r in range(Q_PER_KV):
        slab = qb[:, r * kvw:(r + 1) * kvw]
        for g in range(N_KV_HEADS):
            pieces.append(jnp.where(lane_g == g, slab, 0.0))
    tiles = [jnp.where(low_half, pieces[2 * k], pltpu.roll(pieces[2 * k + 1], half, axis=0))
             for k in range(len(pieces) // 2)]
    qbd = jnp.concatenate(tiles, axis=0)
    ms = jnp.sum(qbd * qbd, axis=-1, keepdims=True) * (1.0 / HEAD_DIM)
    qn = qbd * lax.rsqrt(ms + EPS) * qg_ref[...]
    qs = (qn * (HEAD_DIM ** -0.5 * LOG2E)).astype(bf16)

    for p in range(n_pages):
        k_copy(b, slot, p).wait()

    def k_page(p):
        return kbuf[slot, p].reshape(kvw, PAGE_SIZE)

    lane = lax.broadcasted_iota(jnp.int32, (kvw, LANES), 1)
    kmean_t = jnp.zeros((kvw, LANES), f32)
    for j in range(nblk):
        blk = k_page(j * ppb)
        for pp in range(1, ppb):
            blk = blk + k_page(j * ppb + pp)
        kmean_t = kmean_t + jnp.where(lane == j, jnp.sum(blk, axis=-1, keepdims=True) * (1.0 / MOBA_BLOCK), 0.0)
    kh, kmid, kl = _split3(kmean_t)
    qh, qmid, ql = _split3(qn)
    gate = _dot(qh, kh) + _dot(qh, kmid) + _dot(qmid, kh) + _dot(qh, kl) + _dot(ql, kh) + _dot(qmid, kmid)
    jb = lax.broadcasted_iota(jnp.int32, (nrow, LANES), 1)
    gate = jnp.where(jb < nblk, gate, -jnp.inf)
    pen = _top3_penalty(gate, jb, 1)

    slope = sl_ref[...]
    kk = lax.broadcasted_iota(jnp.int32, (nrow, PAGE_SIZE), 1)
    pages_per_dot = 4
    for p0 in range(0, n_pages, pages_per_dot):
        for p in range(p0, p0 + pages_per_dot):
            k_copy(nxt, 1 - slot, p).start()
        k4 = jnp.concatenate([k_page(p).astype(bf16) for p in range(p0, p0 + pages_per_dot)], axis=1)
        s4 = _dot(qs, k4)
        for i in range(pages_per_dot):
            p = p0 + i
            j = p // ppb
            rel = (kk + (p * PAGE_SIZE - past)).astype(f32)
            s_ref[:, p * PAGE_SIZE:(p + 1) * PAGE_SIZE] = (
                s4[:, i * PAGE_SIZE:(i + 1) * PAGE_SIZE] + slope * rel + pen[:, j:j + 1])

    zpad = jnp.zeros((LANES - SAMPLE_ROWS, kvw), f32)
    knew = jnp.concatenate([kn_ref[...], zpad], axis=0).astype(bf16)
    vnew = jnp.concatenate([vn_ref[...], zpad], axis=0).astype(bf16)
    kn_lane = lax.broadcasted_iota(jnp.int32, (nrow, LANES), 1)
    tok = lax.broadcasted_iota(jnp.int32, (nrow, LANES), 0) % half
    s_own = _dot_nt(qs, knew) + slope * kn_lane.astype(f32)
    s_ref[:, past:past + LANES] = jnp.where(kn_lane <= tok, s_own, NEG)

    m = jnp.max(s_ref[...], axis=-1, keepdims=True)
    p_own = jnp.exp2(s_ref[:, past:past + LANES] - m)
    n_acc = 4
    psums = [p_own] + [None] * (n_acc - 1)
    accs = [_dot(p_own.astype(bf16), vnew)] + [None] * (n_acc - 1)

    for p in range(n_pages):
        v_copy(b, slot, p).wait()

    for p0 in range(0, n_pages, ppb):
        for p in range(p0, p0 + ppb):
            v_copy(nxt, 1 - slot, p).start()
        pp = jnp.exp2(s_ref[:, p0 * PAGE_SIZE:(p0 + ppb) * PAGE_SIZE] - m)
        vb = jnp.concatenate([vbuf[slot, p].reshape(kvw, PAGE_SIZE).astype(bf16) for p in range(p0, p0 + ppb)], axis=1)
        pv = _dot_nt(pp.astype(bf16), vb)
        a = (p0 // ppb) % n_acc
        pl_sum = pp[:, :PAGE_SIZE]
        for i in range(1, ppb):
            pl_sum = pl_sum + pp[:, i * PAGE_SIZE:(i + 1) * PAGE_SIZE]
        psums[a] = pl_sum if psums[a] is None else psums[a] + pl_sum
        accs[a] = pv if accs[a] is None else accs[a] + pv
    acc = (accs[0] + accs[1]) + (accs[2] + accs[3])
    psum = (psums[0] + psums[1]) + (psums[2] + psums[3])
    o = acc / jnp.sum(psum, axis=-1, keepdims=True)

    @pl.when(b == nseq - 1)
    def _():
        for p in range(n_pages):
            k_copy(nxt, 1 - slot, p).wait()
        for p in range(n_pages):
            v_copy(nxt, 1 - slot, p).wait()

    slabs = []
    for r in range(Q_PER_KV):
        slab = jnp.zeros((SAMPLE_ROWS, kvw), f32)
        for g in range(N_KV_HEADS):
            piece = r * N_KV_HEADS + g
            tile = o[(piece // 2) * SAMPLE_ROWS:(piece // 2 + 1) * SAMPLE_ROWS, :]
            if piece % 2:
                tile = pltpu.roll(tile, half, axis=0)
            slab = slab + jnp.where(lane_g == g, tile, 0.0)
        slabs.append(slab)
    o_ref[...] = jnp.concatenate(slabs, axis=1)


def moba_sample(page_table, proj, kn_new, ck, cv, q_gain_row, slope_rows, nseq):
    kvw = N_KV_HEADS * HEAD_DIM
    qw = N_HEADS * HEAD_DIM
    n_pages = page_table.shape[1]
    nrow = Q_PER_KV * N_KV_HEADS * (SAMPLE_ROWS // 2)
    past = n_pages * PAGE_SIZE
    grid_spec = pltpu.PrefetchScalarGridSpec(
        num_scalar_prefetch=1,
        grid=(nseq,),
        in_specs=[
            pl.BlockSpec((SAMPLE_ROWS, qw), lambda b, pt: (b, OFF_Q // qw)),
            pl.BlockSpec((SAMPLE_ROWS, kvw), lambda b, pt: (b, 0)),
            pl.BlockSpec((SAMPLE_ROWS, kvw), lambda b, pt: (b, OFF_V // kvw)),
            pl.BlockSpec((1, kvw), lambda b, pt: (0, 0)),
            pl.BlockSpec((nrow, LANES), lambda b, pt: (0, 0)),
            pl.BlockSpec(memory_space=pl.ANY),
            pl.BlockSpec(memory_space=pl.ANY),
        ],
        out_specs=pl.BlockSpec((SAMPLE_ROWS, qw), lambda b, pt: (b, 0)),
        scratch_shapes=[
            pltpu.VMEM((2, n_pages, N_KV_HEADS, HEAD_DIM, PAGE_SIZE), f32),
            pltpu.VMEM((2, n_pages, N_KV_HEADS, HEAD_DIM, PAGE_SIZE), f32),
            pltpu.VMEM((nrow, past + LANES), f32),
            pltpu.SemaphoreType.DMA((2, 2)),
        ],
    )
    return pl.pallas_call(
        functools.partial(_moba_sample_kernel, n_pages=n_pages),
        grid_spec=grid_spec,
        out_shape=jax.ShapeDtypeStruct((nseq * SAMPLE_ROWS, qw), f32),
        compiler_params=_cparams(("arbitrary",)),
        name="moba_sample",
    )(page_table, proj, kn_new, proj, q_gain_row, slope_rows, ck, cv)


def _ssd_sample_kernel(ext_ref, z_ref, dt_ref, h0_ref, cw_ref, cb_ref, dtb_ref, alog_ref, dexp_ref, norm_ref,
                       y_ref, h_ref, scr_ref, *, nseq_step, n_new):
    rows = nseq_step * SAMPLE_ROWS
    scr_ref[pl.ds(0, rows), :] = ext_ref[...]
    scr_ref[pl.ds(rows, SUBLANES), :] = jnp.zeros((SUBLANES, CONV_DIM), f32)
    act = _ssd_conv_act(scr_ref, rows, cw_ref, cb_ref, base=0)

    ri = lax.broadcasted_iota(jnp.int32, (rows, rows), 0)
    ci = lax.broadcasted_iota(jnp.int32, (rows, rows), 1)
    allowed = jnp.logical_and(ri // SAMPLE_ROWS == ci // SAMPLE_ROWS,
                              jnp.logical_and(ci % SAMPLE_ROWS <= ri % SAMPLE_ROWS, ci % SAMPLE_ROWS < n_new))
    xs, bm, cm, dt_e, cs_e, xdt, y = _ssd_chunk(act, dt_ref[...], allowed, dtb_ref, alog_ref)

    valid = lax.broadcasted_iota(jnp.int32, (SAMPLE_ROWS, 1), 0) < n_new
    ecs = jnp.exp2(cs_e)
    hw = D_INNER // SSM_GROUPS
    hpg = SSM_HEADS // SSM_GROUPS
    yoff_rows = []
    for i in range(nseq_step):
        lo = i * SAMPLE_ROWS
        cs_i = cs_e[lo:lo + SAMPLE_ROWS, :]
        cs_last = cs_e[lo + n_new - 1:lo + n_new, :]
        xdte = jnp.where(valid, xdt[lo:lo + SAMPLE_ROWS, :] * jnp.exp2(cs_last - cs_i), 0.0)
        yg = []
        upd = []
        for g in range(SSM_GROUPS):
            h0g = h0_ref[i, g * hpg:(g + 1) * hpg].reshape(hw, D_STATE)
            yg.append(_dot_nt(cm[lo:lo + SAMPLE_ROWS, g * D_STATE:(g + 1) * D_STATE], h0g))
            upd.append(_dot_tn(xdte[:, g * hw:(g + 1) * hw], bm[lo:lo + SAMPLE_ROWS, g * D_STATE:(g + 1) * D_STATE]))
        yoff_rows.append(jnp.concatenate(yg, axis=1))
        dec_row = jnp.exp2(cs_last)
        for h in range(SSM_HEADS):
            dec_h = jnp.broadcast_to(dec_row[:, h * SSM_HEAD_DIM:h * SSM_HEAD_DIM + 1], (SSM_HEAD_DIM, D_STATE))
            r = h % hpg
            h_ref[i, h] = h0_ref[i, h] * dec_h + upd[h // hpg][r * SSM_HEAD_DIM:(r + 1) * SSM_HEAD_DIM, :]
    y = y + jnp.concatenate(yoff_rows, axis=0) * ecs
    y_ref[...] = _ssd_finish(y, xs, z_ref[...], dexp_ref, norm_ref).astype(y_ref.dtype)


def ssd_sample(ext, proj, h0, conv_w, conv_b_row, dtb_row, alog_row, dexp_row, norm_row, nseq, nseq_step, n_new):
    rows = nseq_step * SAMPLE_ROWS
    const = lambda i: (0, 0)
    state_spec = pl.BlockSpec((nseq_step, SSM_HEADS, SSM_HEAD_DIM, D_STATE), lambda i: (i, 0, 0, 0))
    return pl.pallas_call(
        functools.partial(_ssd_sample_kernel, nseq_step=nseq_step, n_new=n_new),
        grid=(nseq // nseq_step,),
        in_specs=[
            pl.BlockSpec((rows, CONV_DIM), lambda i: (i, 0)),
            pl.BlockSpec((rows, D_INNER), lambda i: (i, OFF_Z // D_INNER)),
            pl.BlockSpec((rows, LANES), lambda i: (i, OFF_DT // LANES)),
            state_spec,
            pl.BlockSpec((CONV_WIDTH, CONV_DIM), const),
            pl.BlockSpec((1, CONV_DIM), const),
            pl.BlockSpec((1, LANES), const),
            pl.BlockSpec((1, LANES), const),
            pl.BlockSpec((1, D_INNER), const),
            pl.BlockSpec((1, D_INNER), const),
        ],
        out_specs=[pl.BlockSpec((rows, D_INNER), lambda i: (i, 0)), state_spec],
        out_shape=[
            jax.ShapeDtypeStruct((nseq * SAMPLE_ROWS, D_INNER), f32),
            jax.ShapeDtypeStruct(h0.shape, f32),
        ],
        scratch_shapes=[pltpu.VMEM((rows + SUBLANES, CONV_DIM), f32)],
        compiler_params=_cparams(("arbitrary",)),
        name="ssd_sample",
    )(ext, proj, proj, h0, conv_w, conv_b_row, dtb_row, alog_row, dexp_row, norm_row)


IN_SIZES = (N_HEADS * HEAD_DIM, N_KV_HEADS * HEAD_DIM, N_KV_HEADS * HEAD_DIM, D_INNER, CONV_DIM, SSM_HEADS,
            MEM_HEADS * MEM_HEAD_DIM, 3 * D_MODEL)
TM = 512
TM_PROJ = 1024
TN_PROJ = 1920
TQ = 256
SSD_ROWS = 256
SSD_SAMPLE_SEQS = 8
MEM_SEQS_PER_STEP = 4


def _pad_lanes(v):
    return jnp.pad(v, (0, LANES - v.shape[0])).reshape(1, LANES)


def kernel(x_prompt, x_sample, mem_prompt, cache_k, cache_v, page_table, state_conv, state_ssm, cache_mem_k, cache_mem_v, norm_mix, w_in, q_norm, k_norm, conv_w, conv_b, dt_bias, a_log, d_skip, ssm_norm, mem_norm, w_mem_kv, mq_norm, mk_norm, w_attn_br, w_ssm_br, w_mem_br, w_out, norm_ffn, w_gate, w_up, w_down):
    assert w_in.shape[0] == 1, "single layer"
    assert x_sample.shape[1] <= SAMPLE_ROWS // 2, "sample MoBA packs two heads' tokens into one row tile"
    batch, seq, d = x_prompt.shape
    nseq, n_new, _ = x_sample.shape
    kvw = N_KV_HEADS * HEAD_DIM
    qw = N_HEADS * HEAD_DIM

    offs = np.cumsum(IN_SIZES)[:-1].tolist()
    wq, wk, wv, wz, wxbc, wdt, wmq, wgates = jnp.split(w_in[0], offs, axis=1)
    wdt = jnp.pad(wdt, ((0, 0), (0, LANES - SSM_HEADS)))
    tail = [wz, wxbc, wmq, wk, wv, wdt]
    w_prompt = jnp.concatenate([wq] + tail, axis=1).astype(bf16)
    wq_rgd = wq.reshape(d, N_KV_HEADS, Q_PER_KV, HEAD_DIM).transpose(0, 2, 1, 3).reshape(d, qw)
    w_sample = jnp.concatenate([wq_rgd] + tail, axis=1).astype(bf16)
    w_gates = wgates.astype(bf16)
    wa = w_attn_br[0].astype(bf16)
    wa_rgd = w_attn_br[0].reshape(N_KV_HEADS, Q_PER_KV, HEAD_DIM, d).transpose(1, 0, 2, 3).reshape(qw, d).astype(bf16)
    ws, wm, wo = w_ssm_br[0].astype(bf16), w_mem_br[0].astype(bf16), w_out[0].astype(bf16)
    wg, wu, wd = w_gate[0].astype(bf16), w_up[0].astype(bf16), w_down[0].astype(bf16)

    k_gain = jnp.tile(k_norm[0], N_KV_HEADS).reshape(1, kvw)
    conv_b_row = conv_b[0].reshape(1, CONV_DIM)
    dtb_row, alog_row = _pad_lanes(dt_bias[0]), _pad_lanes(a_log[0])
    dexp_row = jnp.repeat(d_skip[0], SSM_HEAD_DIM).reshape(1, D_INNER)
    ssm_norm_row = ssm_norm[0].reshape(1, D_INNER)
    mq_gain = mq_norm[0].reshape(1, MEM_HEAD_DIM)
    ffn_gain = norm_ffn[0].reshape(1, d)
    slopes2 = jnp.exp2(-8.0 * jnp.arange(1, N_HEADS + 1, dtype=f32) / N_HEADS) * LOG2E

    xp = x_prompt.reshape(batch * seq, d)
    proj = inproj(xp, norm_mix[0], w_prompt, TM_PROJ, TN_PROJ)
    knt, vt32, kb, vt, kmean = kprep(proj, k_gain, batch, seq, 2048)
    kmean_g = kmean.reshape(batch, seq // MOBA_BLOCK, N_KV_HEADS, HEAD_DIM).transpose(0, 2, 1, 3)
    slope_cols = jnp.repeat(slopes2.reshape(N_KV_HEADS, Q_PER_KV), TQ, axis=1).reshape(N_KV_HEADS, 1, Q_PER_KV * TQ)
    q_gain_col = jnp.broadcast_to(q_norm[0][:, None], (HEAD_DIM, TQ))
    attn_o = moba_prompt(proj, kb, vt, kmean_g, q_gain_col, slope_cols, batch, seq, TQ)
    ssm_o, h_prompt = ssd_prompt(proj, conv_w[0], conv_b_row, dtb_row, alog_row, dexp_row, ssm_norm_row,
                                 batch, seq, SSD_ROWS)
    mem_kv = inproj(mem_prompt.reshape(batch * MEM_TOKENS, d), mem_norm[0], w_mem_kv[0].astype(bf16), TM, qw)
    mk = headnorm(mem_kv, 0, qw, jnp.tile(mk_norm[0], MEM_HEADS).reshape(1, qw), MEM_HEAD_DIM, TM)
    mv = mem_kv[:, qw:]
    mem_o = mem_attend(proj, mk.reshape(batch, MEM_TOKENS, qw), mv.reshape(batch, MEM_TOKENS, qw), mq_gain,
                       batch, seq, TM)
    gates = inproj(xp, norm_mix[0], w_gates, TM_PROJ, 3 * d, bf16)
    y_prompt = merge_ffn(xp, attn_o, ssm_o, mem_o, gates, wa, ws, wm, wo, ffn_gain, wg, wu, wd,
                         TM).reshape(batch, seq, d)

    k_prompt = jnp.transpose(knt.reshape(batch, N_KV_HEADS, HEAD_DIM, seq), (0, 3, 1, 2))[None]
    v_prompt = jnp.transpose(vt32.reshape(batch, N_KV_HEADS, HEAD_DIM, seq), (0, 3, 1, 2))[None]
    conv_prompt = proj.reshape(batch, seq, N_PROJ)[:, seq - (CONV_WIDTH - 1):, OFF_XBC:OFF_XBC + CONV_DIM][None]
    mem_k_prompt = mk.reshape(1, batch, MEM_TOKENS, MEM_HEADS, MEM_HEAD_DIM)
    mem_v_prompt = mv.reshape(1, batch, MEM_TOKENS, MEM_HEADS, MEM_HEAD_DIM)

    xs = jnp.pad(x_sample, ((0, 0), (0, SAMPLE_ROWS - n_new), (0, 0))).reshape(nseq * SAMPLE_ROWS, d)
    proj_s = inproj(xs, norm_mix[0], w_sample, TM_PROJ, TN_PROJ)
    kn_s = headnorm(proj_s, OFF_K // kvw, kvw, k_gain, HEAD_DIM, TM)
    slope_rows = jnp.broadcast_to(
        jnp.repeat(slopes2.reshape(N_KV_HEADS, Q_PER_KV).T.reshape(-1), SAMPLE_ROWS // 2)[:, None],
        (N_HEADS * (SAMPLE_ROWS // 2), LANES))
    ck_t = jnp.transpose(cache_k[0], (0, 2, 3, 1))
    cv_t = jnp.transpose(cache_v[0], (0, 2, 3, 1))
    attn_s = moba_sample(page_table, proj_s, kn_s, ck_t, cv_t, jnp.tile(q_norm[0], N_KV_HEADS).reshape(1, kvw),
                         slope_rows, nseq)
    xbc_new = proj_s[:, OFF_XBC:OFF_XBC + CONV_DIM].reshape(nseq, SAMPLE_ROWS, CONV_DIM)[:, :n_new]
    ext = jnp.concatenate(
        [state_conv[0], xbc_new, jnp.zeros((nseq, SAMPLE_ROWS - (CONV_WIDTH - 1) - n_new, CONV_DIM), f32)], axis=1)
    ssm_s, h_sample = ssd_sample(ext.reshape(nseq * SAMPLE_ROWS, CONV_DIM), proj_s, state_ssm[0], conv_w[0], conv_b_row,
                                 dtb_row, alog_row, dexp_row, ssm_norm_row, nseq, SSD_SAMPLE_SEQS, n_new)
    mem_s = mem_attend(proj_s, cache_mem_k[0], cache_mem_v[0], mq_gain, nseq, SAMPLE_ROWS, SAMPLE_ROWS)
    gates_s = inproj(xs, norm_mix[0], w_gates, TM_PROJ, 3 * d, bf16)
    x1s = merge(xs, attn_s, ssm_s, mem_s, gates_s, wa_rgd, ws, wm, wo, TM)
    y_sample = ffn(x1s, ffn_gain, wg, wu, wd, TM).reshape(nseq, SAMPLE_ROWS, d)[:, :n_new]

    k_sample = kn_s.reshape(nseq, SAMPLE_ROWS, N_KV_HEADS, HEAD_DIM)[:, :n_new][None]
    v_sample = proj_s[:, OFF_V:OFF_V + kvw].reshape(nseq, SAMPLE_ROWS, N_KV_HEADS, HEAD_DIM)[:, :n_new][None]
    conv_sample = ext[:, n_new:n_new + CONV_WIDTH - 1][None]

    return (y_prompt, y_sample, k_prompt, v_prompt, conv_prompt, h_prompt[None], mem_k_prompt, mem_v_prompt,
            k_sample, v_sample, conv_sample, h_sample[None])
```
